```python
import jax, jax.numpy as jnp
from jax import lax
import numpy as np


D_MODEL = 2048
BATCH = 2
SEQ = 16384
DEPTH = 2

HEAD_DIM = 128
A_HEADS = 4
B_GROUPS = 4
C_HEADS = 4
D_GROUPS = 4
A_WIDTH = A_HEADS * HEAD_DIM
B_WIDTH = B_GROUPS * HEAD_DIM
C_WIDTH = C_HEADS * HEAD_DIM
D_WIDTH = D_GROUPS * HEAD_DIM
D_MIX = A_WIDTH + B_WIDTH + C_WIDTH + D_WIDTH

MOBA_BLOCK = 256
MOBA_TOPK = 3
MOBA_QCHUNK = 64
GMLP_CHUNK = 128
MLA_Q_RANK = 448
MLA_KV_RANK = 160
MLA_NOPE = 128
MLA_ROPE = 64
MLA_V = HEAD_DIM
MLA_QK = MLA_NOPE + MLA_ROPE
ROPE_THETA = 10000.0
ATTN_QBLOCK = 128
CONV_WIDTH = 31
N_EXPERTS = 64
TOP_K = 8
D_EXPERT = 512
D_SHARED = 512
ROUTED_SCALE = 2.5
MOE_BLOCK = 256
N_MOD = 6
EPS = 1e-6
NEG = -1e30

OFF_AQ = 0
OFF_AK = OFF_AQ + A_WIDTH
OFF_AV = OFF_AK + A_WIDTH
OFF_B = OFF_AV + A_WIDTH
OFF_CQ = OFF_B + 2 * B_WIDTH
OFF_CKV = OFF_CQ + MLA_Q_RANK
OFF_CKR = OFF_CKV + MLA_KV_RANK
OFF_D = OFF_CKR + MLA_ROPE
IN_COLS = OFF_D + 2 * D_WIDTH

kernel_name = 'hybrid_moba_gmlp_mla_conformer_moe_block'


def rms_norm(x, g):
    xf = x.astype(jnp.float32)
    y = xf * lax.rsqrt(jnp.mean(xf * xf, axis=-1, keepdims=True) + EPS)
    return (y * g.astype(jnp.float32)).astype(x.dtype)


def layer_norm(x, g, b):
    xf = x.astype(jnp.float32)
    mu = jnp.mean(xf, axis=-1, keepdims=True)
    xc = xf - mu
    y = xc * lax.rsqrt(jnp.mean(xc * xc, axis=-1, keepdims=True) + EPS)
    return (y * g.astype(jnp.float32) + b.astype(jnp.float32)).astype(x.dtype)


def _split_heads(a, n):
    b, s, _ = a.shape
    return a.reshape(b, s, n, -1).transpose(0, 2, 1, 3)


def _merge_heads(a):
    b, n, s, d = a.shape
    return a.transpose(0, 2, 1, 3).reshape(b, s, n * d)


def alibi_slopes(n):
    return 2.0 ** (-8.0 * jnp.arange(1, n + 1, dtype=jnp.float32) / n)


def moba_attention(q, k, v, positions):
    b, nh, s, dh = q.shape
    nb = -(-s // MOBA_BLOCK)
    pad = nb * MOBA_BLOCK - s
    kp = jnp.pad(k, ((0, 0), (0, 0), (0, pad), (0, 0)))
    vp = jnp.pad(v, ((0, 0), (0, 0), (0, pad), (0, 0)))
    posp = jnp.pad(positions, ((0, 0), (0, pad)))
    kb = kp.reshape(b, nh, nb, MOBA_BLOCK, dh)
    vb = vp.reshape(b, nh, nb, MOBA_BLOCK, dh)
    pb = posp.reshape(b, nb, MOBA_BLOCK)
    kmean = jnp.mean(kb.astype(jnp.float32), axis=3).astype(q.dtype)
    n_sel = min(MOBA_TOPK, nb)
    slopes = alibi_slopes(nh)
    scale = dh ** -0.5
    n_chunks = s // MOBA_QCHUNK
    qc = q.reshape(b, nh, n_chunks, MOBA_QCHUNK, dh).transpose(2, 0, 1, 3, 4)
    pc = positions.reshape(b, n_chunks, MOBA_QCHUNK).transpose(1, 0, 2)
    bi = jnp.arange(b)[:, None, None, None]
    hi = jnp.arange(nh)[None, :, None, None]

    def one_chunk(args):
        ci, q_c, p_c = args
        t = ci * MOBA_QCHUNK + jnp.arange(MOBA_QCHUNK)
        own = (ci * MOBA_QCHUNK) // MOBA_BLOCK
        gate = jnp.einsum('bhqd,bhnd->bhqn', q_c, kmean).astype(jnp.float32)
        gate = jnp.where(jnp.arange(nb) < own, gate, NEG)
        _, sel = lax.top_k(gate, n_sel)
        sel_ok = sel < own
        ks = kb[bi, hi, sel]
        vs = vb[bi, hi, sel]
        ps = pb[bi, sel]
        s_sel = jnp.einsum('bhqd,bhqnkd->bhqnk', q_c, ks).astype(jnp.float32) * scale
        dist_sel = jnp.abs(p_c[:, None, :, None, None] - ps).astype(jnp.float32)
        s_sel = s_sel - slopes[None, :, None, None, None] * dist_sel
        s_sel = jnp.where(sel_ok[..., None], s_sel, NEG)
        k_own = lax.dynamic_slice_in_dim(kp, own * MOBA_BLOCK, MOBA_BLOCK, axis=2)
        v_own = lax.dynamic_slice_in_dim(vp, own * MOBA_BLOCK, MOBA_BLOCK, axis=2)
        p_own = lax.dynamic_slice_in_dim(posp, own * MOBA_BLOCK, MOBA_BLOCK, axis=1)
        s_own = jnp.einsum('bhqd,bhkd->bhqk', q_c, k_own).astype(jnp.float32) * scale
        dist_own = jnp.abs(p_c[:, None, :, None] - p_own[:, None, None, :]).astype(jnp.float32)
        s_own = s_own - slopes[None, :, None, None] * dist_own
        kidx = own * MOBA_BLOCK + jnp.arange(MOBA_BLOCK)
        s_own = jnp.where(kidx[None, :] <= t[:, None], s_own, NEG)
        n_gath = n_sel * MOBA_BLOCK
        logits = jnp.concatenate([s_sel.reshape(b, nh, MOBA_QCHUNK, n_gath), s_own], axis=-1)
        p = jax.nn.softmax(logits, axis=-1).astype(v.dtype)
        p_sel = p[..., :n_gath].reshape(b, nh, MOBA_QCHUNK, n_sel, MOBA_BLOCK)
        return (jnp.einsum('bhqnk,bhqnkd->bhqd', p_sel, vs)
                + jnp.einsum('bhqk,bhkd->bhqd', p[..., n_gath:], v_own))

    out = lax.map(one_chunk, (jnp.arange(n_chunks), qc, pc))
    return out.transpose(1, 2, 0, 3, 4).reshape(b, nh, s, dh)


def chunk_spatial_gating(z, g_v, w_s, b_s):
    b, s, _ = z.shape
    u, vv = z[..., :B_WIDTH], z[..., B_WIDTH:]
    nc = s // GMLP_CHUNK
    vv = rms_norm(vv.reshape(b, nc, GMLP_CHUNK, B_GROUPS, HEAD_DIM), g_v)
    mask = jnp.tril(jnp.ones((GMLP_CHUNK, GMLP_CHUNK), dtype=bool))
    w = jnp.where(mask[None], w_s, 0.0).astype(vv.dtype)
    mixed = jnp.einsum('gij,bcjgd->bcigd', w, vv) + b_s.T[None, None, :, :, None]
    return u * mixed.reshape(b, s, B_WIDTH)


def rope_angles(positions):
    half = MLA_ROPE // 2
    inv_freq = ROPE_THETA ** (-jnp.arange(half, dtype=jnp.float32) * (2.0 / MLA_ROPE))
    ang = positions.astype(jnp.float32)[..., None] * inv_freq
    return jnp.cos(ang), jnp.sin(ang)


def apply_rope(x, cos, sin):
    half = MLA_ROPE // 2
    xf = x.astype(jnp.float32)
    x1, x2 = xf[..., :half], xf[..., half:]
    return jnp.concatenate([x1 * cos - x2 * sin, x1 * sin + x2 * cos], axis=-1).astype(x.dtype)


def causal_attention(q, k, v, scale):
    b, nh, s, dk = q.shape
    nq = s // ATTN_QBLOCK
    qb = q.reshape(b, nh, nq, ATTN_QBLOCK, dk).transpose(2, 0, 1, 3, 4)
    kidx = jnp.arange(s)

    def blk(args):
        i, q_i = args
        sc = jnp.einsum('bhqd,bhkd->bhqk', q_i, k).astype(jnp.float32) * scale
        qidx = i * ATTN_QBLOCK + jnp.arange(ATTN_QBLOCK)
        sc = jnp.where(kidx[None, :] <= qidx[:, None], sc, NEG)
        p = jax.nn.softmax(sc, axis=-1).astype(v.dtype)
        return jnp.einsum('bhqk,bhkd->bhqd', p, v)

    o = lax.map(blk, (jnp.arange(nq), qb))
    return o.transpose(1, 2, 0, 3, 4).reshape(b, nh, s, v.shape[-1])


def mla_attention(c_q, c_kv, k_r, positions, g_q, w_uq, g_kv, w_ukv):
    b, s, _ = c_q.shape
    q = (rms_norm(c_q, g_q) @ w_uq).reshape(b, s, C_HEADS, MLA_QK)
    kv = (rms_norm(c_kv, g_kv) @ w_ukv).reshape(b, s, C_HEADS, MLA_NOPE + MLA_V)
    cos, sin = rope_angles(positions)
    q_rope = apply_rope(q[..., MLA_NOPE:], cos[:, :, None], sin[:, :, None])
    k_rope = apply_rope(k_r, cos, sin)
    q = jnp.concatenate([q[..., :MLA_NOPE], q_rope], axis=-1).transpose(0, 2, 1, 3)
    k_rope_h = jnp.broadcast_to(k_rope[:, :, None, :], (b, s, C_HEADS, MLA_ROPE))
    k = jnp.concatenate([kv[..., :MLA_NOPE], k_rope_h], axis=-1).transpose(0, 2, 1, 3)
    v = kv[..., MLA_NOPE:].transpose(0, 2, 1, 3)
    return causal_attention(q, k, v, MLA_QK ** -0.5)


def conformer_conv(z, w_dw, b_dw, g_ln, b_ln):
    y = z[..., :D_WIDTH] * jax.nn.sigmoid(z[..., D_WIDTH:])
    y = lax.conv_general_dilated(y, w_dw, window_strides=(1,), padding=((CONV_WIDTH - 1, 0),),
                                 dimension_numbers=('NWC', 'WIO', 'NWC'),
                                 feature_group_count=D_WIDTH) + b_dw
    return jax.nn.silu(layer_norm(y, g_ln, b_ln))


def hybrid_mixer(h, positions, w_in, g_gmlp_v, w_spatial, b_spatial, g_mla_q, w_mla_uq,
                 g_mla_kv, w_mla_ukv, w_conv_dw, b_conv_dw, g_conv_ln, b_conv_ln, w_out):
    z = h @ w_in
    y_a = _merge_heads(moba_attention(_split_heads(z[..., OFF_AQ:OFF_AK], A_HEADS),
                                      _split_heads(z[..., OFF_AK:OFF_AV], A_HEADS),
                                      _split_heads(z[..., OFF_AV:OFF_B], A_HEADS), positions))
    y_b = chunk_spatial_gating(jax.nn.gelu(z[..., OFF_B:OFF_CQ]), g_gmlp_v, w_spatial, b_spatial)
    y_c = _merge_heads(mla_attention(z[..., OFF_CQ:OFF_CKV], z[..., OFF_CKV:OFF_CKR],
                                     z[..., OFF_CKR:OFF_D], positions,
                                     g_mla_q, w_mla_uq, g_mla_kv, w_mla_ukv))
    y_d = conformer_conv(z[..., OFF_D:IN_COLS], w_conv_dw, b_conv_dw, g_conv_ln, b_conv_ln)
    return jnp.concatenate([y_a, y_b, y_c, y_d], axis=-1) @ w_out


def swiglu(x, w_g, w_u, w_d):
    return (jax.nn.silu(x @ w_g) * (x @ w_u)) @ w_d


def grouped_expert_ffn(x, idx, w, w_gate, w_up, w_down):
    t, d = x.shape
    a = t * TOP_K
    m = MOE_BLOCK
    n_blocks = -(-a // m) + N_EXPERTS
    p_rows = n_blocks * m
    flat_e = idx.reshape(a)
    flat_tok = (jnp.arange(a, dtype=jnp.int32) // TOP_K)
    flat_w = w.reshape(a)
    order = jnp.argsort(flat_e)
    se, st, sw = flat_e[order], flat_tok[order], flat_w[order]
    counts = jnp.bincount(flat_e, length=N_EXPERTS)
    padded = (counts + m - 1) // m * m
    start = jnp.cumsum(counts) - counts
    pend = jnp.cumsum(padded)
    pstart = pend - padded
    dest = pstart[se] + (jnp.arange(a) - start[se])
    buf_tok = jnp.zeros((p_rows,), jnp.int32).at[dest].set(st).reshape(n_blocks, m)
    buf_w = jnp.zeros((p_rows,), x.dtype).at[dest].set(sw).reshape(n_blocks, m)
    block_e = jnp.minimum(jnp.searchsorted(pend, jnp.arange(n_blocks) * m, side='right'),
                          N_EXPERTS - 1)

    def body(i, out):
        e = block_e[i]
        tok = buf_tok[i]
        xb = x[tok]
        yb = swiglu(xb, w_gate[e], w_up[e], w_down[e]) * buf_w[i][:, None]
        return out.at[tok].add(yb)

    return lax.fori_loop(0, n_blocks, body, jnp.zeros_like(x))


def moe_ffn(h, w_router, b_router, w_exp_gate, w_exp_up, w_exp_down, w_sh_gate, w_sh_up, w_sh_down):
    b, s, d = h.shape
    x = h.reshape(b * s, d)
    scores = jax.nn.sigmoid((x @ w_router).astype(jnp.float32))
    _, idx = lax.top_k(scores + b_router.astype(jnp.float32), TOP_K)
    wts = jnp.take_along_axis(scores, idx, axis=-1)
    wts = wts / jnp.sum(wts, axis=-1, keepdims=True) * ROUTED_SCALE
    routed = grouped_expert_ffn(x, idx, wts.astype(x.dtype), w_exp_gate, w_exp_up, w_exp_down)
    shared = swiglu(x, w_sh_gate, w_sh_up, w_sh_down)
    return (routed + shared).reshape(b, s, d)


def setup_inputs(seed: int = 0) -> dict:
    key = jax.random.key(seed)
    ks = iter(jax.random.split(key, 32))

    def nrm(shape, scale):
        return jax.random.normal(next(ks), shape, jnp.float32) * scale

    L = DEPTH
    x = nrm((BATCH, SEQ, D_MODEL), 1.0)
    c = nrm((BATCH, D_MODEL), 1.0)
    positions = jnp.broadcast_to(jnp.arange(SEQ, dtype=jnp.int32)[None, :], (BATCH, SEQ))
    w_mod = nrm((L, D_MODEL, N_MOD * D_MODEL), 0.5 * D_MODEL ** -0.5)
    b_mod = nrm((L, N_MOD * D_MODEL), 0.02)
    g_attn = 1.0 + nrm((L, D_MODEL), 0.02)
    w_in = nrm((L, D_MODEL, IN_COLS), D_MODEL ** -0.5)
    g_gmlp_v = 1.0 + nrm((L, B_GROUPS, HEAD_DIM), 0.02)
    w_spatial = nrm((L, B_GROUPS, GMLP_CHUNK, GMLP_CHUNK), GMLP_CHUNK ** -0.5)
    b_spatial = 1.0 + nrm((L, B_GROUPS, GMLP_CHUNK), 0.1)
    g_mla_q = 1.0 + nrm((L, MLA_Q_RANK), 0.02)
    w_mla_uq = nrm((L, MLA_Q_RANK, C_HEADS * MLA_QK), MLA_Q_RANK ** -0.5)
    g_mla_kv = 1.0 + nrm((L, MLA_KV_RANK), 0.02)
    w_mla_ukv = nrm((L, MLA_KV_RANK, C_HEADS * (MLA_NOPE + MLA_V)), MLA_KV_RANK ** -0.5)
    w_conv_dw = nrm((L, CONV_WIDTH, 1, D_WIDTH), CONV_WIDTH ** -0.5)
    b_conv_dw = nrm((L, D_WIDTH), 0.02)
    g_conv_ln = 1.0 + nrm((L, D_WIDTH), 0.02)
    b_conv_ln = nrm((L, D_WIDTH), 0.02)
    w_out = nrm((L, D_MIX, D_MODEL), D_MIX ** -0.5)
    g_ffn = 1.0 + nrm((L, D_MODEL), 0.02)
    w_router = nrm((L, D_MODEL, N_EXPERTS), D_MODEL ** -0.5)
    b_router = nrm((L, N_EXPERTS), 0.01)
    w_exp_gate = nrm((L, N_EXPERTS, D_MODEL, D_EXPERT), D_MODEL ** -0.5)
    w_exp_up = nrm((L, N_EXPERTS, D_MODEL, D_EXPERT), D_MODEL ** -0.5)
    w_exp_down = nrm((L, N_EXPERTS, D_EXPERT, D_MODEL), D_EXPERT ** -0.5)
    w_sh_gate = nrm((L, D_MODEL, D_SHARED), D_MODEL ** -0.5)
    w_sh_up = nrm((L, D_MODEL, D_SHARED), D_MODEL ** -0.5)
    w_sh_down = nrm((L, D_SHARED, D_MODEL), D_SHARED ** -0.5)
    g_final = 1.0 + nrm((D_MODEL,), 0.02)
    return {'x': x, 'c': c, 'positions': positions, 'w_mod': w_mod, 'b_mod': b_mod,
            'g_attn': g_attn, 'w_in': w_in, 'g_gmlp_v': g_gmlp_v, 'w_spatial': w_spatial,
            'b_spatial': b_spatial, 'g_mla_q': g_mla_q, 'w_mla_uq': w_mla_uq,
            'g_mla_kv': g_mla_kv, 'w_mla_ukv': w_mla_ukv, 'w_conv_dw': w_conv_dw,
            'b_conv_dw': b_conv_dw, 'g_conv_ln': g_conv_ln, 'b_conv_ln': b_conv_ln,
            'w_out': w_out, 'g_ffn': g_ffn, 'w_router': w_router, 'b_router': b_router,
            'w_exp_gate': w_exp_gate, 'w_exp_up': w_exp_up, 'w_exp_down': w_exp_down,
            'w_sh_gate': w_sh_gate, 'w_sh_up': w_sh_up, 'w_sh_down': w_sh_down,
            'g_final': g_final}


def reference(x, c, positions, w_mod, b_mod, g_attn, w_in, g_gmlp_v, w_spatial, b_spatial,
              g_mla_q, w_mla_uq, g_mla_kv, w_mla_ukv, w_conv_dw, b_conv_dw, g_conv_ln,
              b_conv_ln, w_out, g_ffn, w_router, b_router, w_exp_gate, w_exp_up, w_exp_down,
              w_sh_gate, w_sh_up, w_sh_down, g_final):
    for l in range(DEPTH):
        mod = jax.nn.silu(c) @ w_mod[l] + b_mod[l]
        sh1, sc1, gt1, sh2, sc2, gt2 = jnp.split(mod[:, None, :], N_MOD, axis=-1)
        h = rms_norm(x, g_attn[l]) * (1.0 + sc1) + sh1
        y = hybrid_mixer(h, positions, w_in[l], g_gmlp_v[l], w_spatial[l], b_spatial[l],
                         g_mla_q[l], w_mla_uq[l], g_mla_kv[l], w_mla_ukv[l], w_conv_dw[l],
                         b_conv_dw[l], g_conv_ln[l], b_conv_ln[l], w_out[l])
        x = x + gt1 * y
        h = rms_norm(x, g_ffn[l]) * (1.0 + sc2) + sh2
        y = moe_ffn(h, w_router[l], b_router[l], w_exp_gate[l], w_exp_up[l], w_exp_down[l],
                    w_sh_gate[l], w_sh_up[l], w_sh_down[l])
        x = x + gt2 * y
    return rms_norm(x, g_final)
```

```python
import functools

import jax
import jax.numpy as jnp
from jax import lax
from jax.experimental import pallas as pl
from jax.experimental.pallas import tpu as pltpu

F32 = jnp.float32
BF16 = jnp.bfloat16
I32 = jnp.int32
U32 = jnp.uint32

D_MODEL = 2048
HEAD_DIM = 128
N_HEADS = 4
MIX_W = N_HEADS * HEAD_DIM
MOBA_BLOCK = 256
MOBA_TOPK = 3
GMLP_CHUNK = 128
MLA_Q_RANK = 448
MLA_KV_RANK = 160
MLA_NOPE = 128
MLA_ROPE = 64
MLA_QK = MLA_NOPE + MLA_ROPE
MLA_QK_PAD = 256
ROPE_THETA = 10000.0
CONV_WIDTH = 31
CONV_HALO = 32
N_EXPERTS = 64
TOP_K = 8
D_EXPERT = 512
D_SHARED = 512
ROUTED_SCALE = 2.5
MOE_BLOCK = 256
N_MOD = 6
EPS = 1e-6
NEG = -1e30

ZA_W = 3 * MIX_W
ZB_W = 2 * MIX_W
CQ_PAD = 512
CKV_PAD = 256
KR_PAD = 128
ZC_W = CQ_PAD + CKV_PAD + KR_PAD
ZD_W = 2 * MIX_W

VMEM_LIMIT = 56 * 1024 * 1024
PACK_W = D_MODEL // 2


def _cparams(sem):
    return pltpu.CompilerParams(dimension_semantics=sem, vmem_limit_bytes=VMEM_LIMIT)


def _resident(shape, index_map):
    return pl.BlockSpec(shape, index_map, pipeline_mode=pl.Buffered(1))


def _dot(a, b):
    return jnp.dot(a, b, preferred_element_type=F32)


def _dot_nt(a, b, precision=None):
    return lax.dot_general(a, b, (((1,), (1,)), ((), ())), preferred_element_type=F32,
                           precision=precision)


def _pack_rows(y):
    half = y.shape[1] // 2
    lo = lax.bitcast_convert_type(y[:, :half].astype(BF16).astype(F32), U32)
    hi = lax.bitcast_convert_type(y[:, half:].astype(BF16).astype(F32), U32)
    return (lo >> 16) | hi


def _unpack_rows(u):
    lo = lax.bitcast_convert_type(u << 16, F32)
    hi = lax.bitcast_convert_type(u & jnp.uint32(0xFFFF0000), F32)
    return lo, hi


def _mod_kernel(c_ref, w_ref, b_ref, o_ref):
    c = c_ref[...]
    a = c * jax.nn.sigmoid(c)
    o_ref[...] = jnp.dot(a, w_ref[...], preferred_element_type=F32,
                         precision=lax.Precision.HIGHEST) + b_ref[...]


def _modulation(c, w_mod, b_mod):
    n_layers, d, _ = w_mod.shape
    b = c.shape[0]
    return pl.pallas_call(
        _mod_kernel,
        grid=(n_layers, N_MOD),
        in_specs=[pl.BlockSpec((b, d), lambda l, j: (0, 0)),
                  pl.BlockSpec((None, d, d), lambda l, j: (l, 0, j)),
                  pl.BlockSpec((None, None, 1, d), lambda l, j: (l, j, 0, 0))],
        out_specs=pl.BlockSpec((None, None, b, d), lambda l, j: (l, j, 0, 0)),
        out_shape=jax.ShapeDtypeStruct((n_layers, N_MOD, b, d), F32),
        compiler_params=_cparams(("arbitrary", "arbitrary")),
        name="modulation",
    )(c, w_mod, b_mod.reshape(n_layers, N_MOD, 1, d))


def _inproj_kernel(x_ref, g_ref, sc_ref, sh_ref, w_ref, za_ref, zb_ref, zc_ref, zd_ref):
    x = x_ref[...]
    ms = jnp.mean(x * x, axis=-1, keepdims=True)
    h = (x * lax.rsqrt(ms + EPS) * g_ref[...]) * (1.0 + sc_ref[...]) + sh_ref[...]
    hb = h.astype(BF16)
    off = 0
    for ref in (za_ref, zb_ref, zc_ref, zd_ref):
        w = ref.shape[1]
        ref[...] = _dot(hb, w_ref[:, off:off + w]).astype(BF16)
        off += w


def _in_projection(x, g, sc, sh, w_in_p, seq, tm=512):
    t, d = x.shape
    per_seq = seq // tm
    widths = (ZA_W, ZB_W, ZC_W, ZD_W)
    mod_spec = pl.BlockSpec((None, 1, d), lambda i: (i // per_seq, 0, 0))
    return pl.pallas_call(
        _inproj_kernel,
        grid=(t // tm,),
        in_specs=[pl.BlockSpec((tm, d), lambda i: (i, 0)),
                  pl.BlockSpec((1, d), lambda i: (0, 0)),
                  mod_spec, mod_spec,
                  _resident((d, sum(widths)), lambda i: (0, 0))],
        out_specs=[pl.BlockSpec((tm, w), lambda i: (i, 0)) for w in widths],
        out_shape=[jax.ShapeDtypeStruct((t, w), BF16) for w in widths],
        compiler_params=_cparams(("arbitrary",)),
        name="in_projection",
    )(x, g, sc, sh, w_in_p)


def _softmax_update(s, v, m_ref, l_ref, acc_ref):
    m_old = m_ref[...]
    m_new = jnp.maximum(m_old, jnp.max(s, axis=-1, keepdims=True))
    alpha = jnp.exp(m_old - m_new)
    p = jnp.exp(s - m_new)
    l_ref[...] = alpha * l_ref[...] + jnp.sum(p, axis=-1, keepdims=True)
    acc_ref[...] = alpha * acc_ref[...] + _dot(p.astype(BF16), v)
    m_ref[...] = m_new


def _moba_kernel(slope_ref, q_ref, k_ref, v_ref, pq_ref, pk_ref, o_ref,
                 kmean_ref, sel_ref, m_ref, l_ref, acc_ref):
    blk = MOBA_BLOCK
    nb = kmean_ref.shape[0]
    hd = pl.program_id(1)
    own = pl.program_id(2)

    @pl.when(own == 0)
    def _():
        def body(n, carry):
            kb = k_ref[pl.ds(pl.multiple_of(n * blk, blk), blk), :].astype(F32)
            kmean_ref[pl.ds(n, 1), :] = jnp.mean(kb, axis=0, keepdims=True)
            return carry
        lax.fori_loop(0, nb, body, 0)

    qf = q_ref[...].astype(F32)
    col = lax.broadcasted_iota(I32, (blk, nb), 1)
    colf = col.astype(F32)
    gate = _dot_nt(qf, kmean_ref[...], precision=lax.Precision.HIGHEST)
    gate = jnp.where(col < own, gate, NEG)
    sel = jnp.zeros((blk, nb), F32)
    for _ in range(MOBA_TOPK):
        gmax = jnp.max(gate, axis=-1, keepdims=True)
        first = jnp.min(jnp.where(gate == gmax, colf, float(nb)), axis=-1, keepdims=True)
        hit = colf == first
        sel = jnp.where(hit, 1.0, sel)
        gate = jnp.where(hit, -jnp.inf, gate)
    sel_ref[...] = jnp.where(col < own, sel, 0.0)

    slope = slope_ref[hd]
    qs = (qf * (HEAD_DIM ** -0.5)).astype(BF16)
    pq = pq_ref[...].astype(F32)

    def scores(n):
        start = pl.multiple_of(n * blk, blk)
        s = _dot_nt(qs, k_ref[pl.ds(start, blk), :])
        pk = pk_ref[pl.ds(n, 1), :].astype(F32)
        return s - slope * jnp.abs(pq - pk), v_ref[pl.ds(start, blk), :]

    s, v = scores(own)
    row = lax.broadcasted_iota(I32, (blk, blk), 0)
    kcol = lax.broadcasted_iota(I32, (blk, blk), 1)
    s = jnp.where(kcol <= row, s, NEG)
    m0 = jnp.max(s, axis=-1, keepdims=True)
    p = jnp.exp(s - m0)
    m_ref[...] = m0
    l_ref[...] = jnp.sum(p, axis=-1, keepdims=True)
    acc_ref[...] = _dot(p.astype(BF16), v)

    def body(n, carry):
        s, v = scores(n)
        picked = jnp.sum(jnp.where(col == n, sel_ref[...], 0.0), axis=-1, keepdims=True)
        s = jnp.where(picked > 0.0, s, NEG)
        _softmax_update(s, v, m_ref, l_ref, acc_ref)
        return carry
    lax.fori_loop(0, own, body, 0)

    o_ref[...] = (acc_ref[...] / l_ref[...]).astype(o_ref.dtype)


def _moba_attention(za, pos_col, pos_blk, slopes, batch, seq):
    t = za.shape[0]
    blk = MOBA_BLOCK
    nb = seq // blk
    return pl.pallas_call(
        _moba_kernel,
        grid_spec=pltpu.PrefetchScalarGridSpec(
            num_scalar_prefetch=1,
            grid=(batch, N_HEADS, nb),
            in_specs=[pl.BlockSpec((blk, HEAD_DIM), lambda b, h, i, s: (b * nb + i, h)),
                      pl.BlockSpec((seq, HEAD_DIM), lambda b, h, i, s: (b, N_HEADS + h)),
                      pl.BlockSpec((seq, HEAD_DIM), lambda b, h, i, s: (b, 2 * N_HEADS + h)),
                      pl.BlockSpec((blk, 1), lambda b, h, i, s: (b * nb + i, 0)),
                      pl.BlockSpec((None, nb, blk), lambda b, h, i, s: (b, 0, 0))],
            out_specs=pl.BlockSpec((blk, HEAD_DIM), lambda b, h, i, s: (b * nb + i, h)),
            scratch_shapes=[pltpu.VMEM((nb, HEAD_DIM), F32),
                            pltpu.VMEM((blk, nb), F32),
                            pltpu.VMEM((blk, 1), F32),
                            pltpu.VMEM((blk, 1), F32),
                            pltpu.VMEM((blk, HEAD_DIM), F32)]),
        out_shape=jax.ShapeDtypeStruct((t, MIX_W), BF16),
        compiler_params=_cparams(("arbitrary", "arbitrary", "arbitrary")),
        name="moba_attention",
    )(slopes, za, za, za, pos_col, pos_blk)


def _gelu_tanh(x):
    return 0.5 * x * (1.0 + jnp.tanh(0.7978845608028654 * (x + 0.044715 * x * x * x)))


def _gmlp_kernel(z_ref, gv_ref, ws_ref, bs_ref, o_ref):
    tm = z_ref.shape[0]
    ck = GMLP_CHUNK
    z = _gelu_tanh(z_ref[...].astype(F32))
    row = lax.broadcasted_iota(I32, (ck, ck), 0)
    colm = lax.broadcasted_iota(I32, (ck, ck), 1)
    for g in range(N_HEADS):
        lanes = slice(g * HEAD_DIM, (g + 1) * HEAD_DIM)
        u = z[:, lanes]
        vv = z[:, MIX_W + g * HEAD_DIM:MIX_W + (g + 1) * HEAD_DIM]
        ms = jnp.mean(vv * vv, axis=-1, keepdims=True)
        vn = (vv * lax.rsqrt(ms + EPS) * gv_ref[g:g + 1, :]).astype(BF16)
        w = jnp.where(colm <= row, ws_ref[g], 0.0).astype(BF16)
        bias = bs_ref[g]
        for c in range(tm // ck):
            rows = slice(c * ck, (c + 1) * ck)
            mixed = _dot(w, vn[rows]) + bias
            o_ref[rows, lanes] = (u[rows] * mixed).astype(o_ref.dtype)


def _spatial_gating(zb, g_v, w_s, b_s, tm=512):
    t = zb.shape[0]
    ck = GMLP_CHUNK
    return pl.pallas_call(
        _gmlp_kernel,
        grid=(t // tm,),
        in_specs=[pl.BlockSpec((tm, ZB_W), lambda i: (i, 0)),
                  pl.BlockSpec((N_HEADS, HEAD_DIM), lambda i: (0, 0)),
                  pl.BlockSpec((N_HEADS, ck, ck), lambda i: (0, 0, 0)),
                  pl.BlockSpec((N_HEADS, ck, 1), lambda i: (0, 0, 0))],
        out_specs=pl.BlockSpec((tm, MIX_W), lambda i: (i, 0)),
        out_shape=jax.ShapeDtypeStruct((t, MIX_W), BF16),
        compiler_params=_cparams(("arbitrary",)),
        name="spatial_gating",
    )(zb, g_v, w_s, b_s.reshape(N_HEADS, ck, 1))


def _mla_prep_kernel(z_ref, pos_ref, invf_ref, gq_ref, gkv_ref, wq_ref, wk_ref, wv_ref,
                     q_ref, k_ref, v_ref):
    z = z_ref[...].astype(F32)
    cq = z[:, :CQ_PAD]
    ckv = z[:, CQ_PAD:CQ_PAD + CKV_PAD]
    kr = z[:, CQ_PAD + CKV_PAD:]
    qn = cq * lax.rsqrt(jnp.sum(cq * cq, -1, keepdims=True) * (1.0 / MLA_Q_RANK) + EPS)
    qn = (qn * gq_ref[...]).astype(BF16)
    kvn = ckv * lax.rsqrt(jnp.sum(ckv * ckv, -1, keepdims=True) * (1.0 / MLA_KV_RANK) + EPS)
    kvn = (kvn * gkv_ref[...]).astype(BF16)

    ang = pos_ref[...].astype(F32) * invf_ref[...]
    lane = lax.broadcasted_iota(I32, ang.shape, 1)
    half = MLA_ROPE // 2
    cos = jnp.cos(ang)
    sin = jnp.sin(ang)
    sin_lo = jnp.where(lane < half, -sin, 0.0)
    sin_hi = jnp.where((lane >= half) & (lane < 2 * half), sin, 0.0)

    def rope(r):
        return (r * cos + pltpu.roll(r, KR_PAD - half, 1) * sin_lo
                + pltpu.roll(r, half, 1) * sin_hi)

    q = _dot(qn, wq_ref[...])
    kn = _dot(kvn, wk_ref[...])
    v_ref[...] = _dot(kvn, wv_ref[...]).astype(v_ref.dtype)
    k_rope = rope(kr)
    scale = MLA_QK ** -0.5
    for h in range(N_HEADS):
        a = h * MLA_QK_PAD
        b = a + MLA_NOPE
        q_ref[:, a:b] = (q[:, a:b] * scale).astype(q_ref.dtype)
        q_ref[:, b:a + MLA_QK_PAD] = (rope(q[:, b:a + MLA_QK_PAD]) * scale).astype(q_ref.dtype)
        k_ref[:, a:b] = kn[:, a:b].astype(k_ref.dtype)
        k_ref[:, b:a + MLA_QK_PAD] = k_rope.astype(k_ref.dtype)


def _mla_prep(zc, pos_col, invf, gq, gkv, wq, wk, wv, tm=512):
    t = zc.shape[0]
    qk_w = N_HEADS * MLA_QK_PAD
    full = lambda i: (0, 0)
    return pl.pallas_call(
        _mla_prep_kernel,
        grid=(t // tm,),
        in_specs=[pl.BlockSpec((tm, ZC_W), lambda i: (i, 0)),
                  pl.BlockSpec((tm, 1), lambda i: (i, 0)),
                  pl.BlockSpec((1, KR_PAD), full),
                  pl.BlockSpec((1, CQ_PAD), full),
                  pl.BlockSpec((1, CKV_PAD), full),
                  pl.BlockSpec((CQ_PAD, qk_w), full),
                  pl.BlockSpec((CKV_PAD, qk_w), full),
                  pl.BlockSpec((CKV_PAD, MIX_W), full)],
        out_specs=[pl.BlockSpec((tm, qk_w), lambda i: (i, 0)),
                   pl.BlockSpec((tm, qk_w), lambda i: (i, 0)),
                   pl.BlockSpec((tm, MIX_W), lambda i: (i, 0))],
        out_shape=[jax.ShapeDtypeStruct((t, qk_w), BF16),
                   jax.ShapeDtypeStruct((t, qk_w), BF16),
                   jax.ShapeDtypeStruct((t, MIX_W), BF16)],
        compiler_params=_cparams(("arbitrary",)),
        name="mla_prep",
    )(zc, pos_col, invf, gq, gkv, wq, wk, wv)


def _mla_attn_kernel(q_ref, k_ref, v_ref, o_ref, m_ref, l_ref, acc_ref):
    tq = q_ref.shape[0]
    i = pl.program_id(2)
    q = q_ref[...]

    def tile(j):
        start = pl.multiple_of(j * tq, tq)
        return _dot_nt(q, k_ref[pl.ds(start, tq), :]), v_ref[pl.ds(start, tq), :]

    s, v = tile(i)
    row = lax.broadcasted_iota(I32, (tq, tq), 0)
    col = lax.broadcasted_iota(I32, (tq, tq), 1)
    s = jnp.where(col <= row, s, NEG)
    m0 = jnp.max(s, axis=-1, keepdims=True)
    p = jnp.exp(s - m0)
    m_ref[...] = m0
    l_ref[...] = jnp.sum(p, axis=-1, keepdims=True)
    acc_ref[...] = _dot(p.astype(BF16), v)

    def body(j, carry):
        s, v = tile(j)
        _softmax_update(s, v, m_ref, l_ref, acc_ref)
        return carry
    lax.fori_loop(0, i, body, 0)

    o_ref[...] = (acc_ref[...] / l_ref[...]).astype(o_ref.dtype)


def _mla_attention(q, k, v, batch, seq, tq=512):
    t = q.shape[0]
    nq = seq // tq
    return pl.pallas_call(
        _mla_attn_kernel,
        grid=(batch, N_HEADS, nq),
        in_specs=[pl.BlockSpec((tq, MLA_QK_PAD), lambda b, h, i: (b * nq + i, h)),
                  pl.BlockSpec((seq, MLA_QK_PAD), lambda b, h, i: (b, h)),
                  pl.BlockSpec((seq, HEAD_DIM), lambda b, h, i: (b, h))],
        out_specs=pl.BlockSpec((tq, HEAD_DIM), lambda b, h, i: (b * nq + i, h)),
        out_shape=jax.ShapeDtypeStruct((t, MIX_W), BF16),
        scratch_shapes=[pltpu.VMEM((tq, 1), F32),
                        pltpu.VMEM((tq, 1), F32),
                        pltpu.VMEM((tq, HEAD_DIM), F32)],
        compiler_params=_cparams(("arbitrary", "arbitrary", "arbitrary")),
        name="mla_attention",
    )(q, k, v)


def _glu(z):
    z = z.astype(F32)
    return z[:, :MIX_W] * jax.nn.sigmoid(z[:, MIX_W:])


def _conv_kernel(z_ref, zprev_ref, w_ref, b_ref, g_ref, beta_ref, o_ref, ybuf_ref, *, per_seq):
    tm = z_ref.shape[0]
    first = (pl.program_id(0) % per_seq) == 0
    ybuf_ref[0:CONV_HALO, :] = jnp.where(first, 0.0, _glu(zprev_ref[...]))
    ybuf_ref[CONV_HALO:, :] = _glu(z_ref[...])
    rows = 64
    shift = CONV_HALO - (CONV_WIDTH - 1)
    for r in range(tm // rows):
        acc = jnp.zeros((rows, MIX_W), F32) + b_ref[...]
        for j in range(CONV_WIDTH):
            acc = acc + w_ref[j:j + 1, :] * ybuf_ref[pl.ds(r * rows + shift + j, rows), :]
        mu = jnp.mean(acc, axis=-1, keepdims=True)
        xc = acc - mu
        y = xc * lax.rsqrt(jnp.mean(xc * xc, axis=-1, keepdims=True) + EPS)
        y = y * g_ref[...] + beta_ref[...]
        o_ref[r * rows:(r + 1) * rows, :] = (y * jax.nn.sigmoid(y)).astype(o_ref.dtype)


def _conformer_conv(zd, w_dw, b_dw, g_ln, b_ln, seq, tm=512):
    t = zd.shape[0]
    per_seq = seq // tm
    halo_blocks = tm // CONV_HALO
    full = lambda i: (0, 0)
    return pl.pallas_call(
        functools.partial(_conv_kernel, per_seq=per_seq),
        grid=(t // tm,),
        in_specs=[pl.BlockSpec((tm, ZD_W), lambda i: (i, 0)),
                  pl.BlockSpec((CONV_HALO, ZD_W),
                               lambda i: (jnp.maximum(i * halo_blocks - 1, 0), 0)),
                  pl.BlockSpec((CONV_WIDTH, MIX_W), full),
                  pl.BlockSpec((1, MIX_W), full),
                  pl.BlockSpec((1, MIX_W), full),
                  pl.BlockSpec((1, MIX_W), full)],
        out_specs=pl.BlockSpec((tm, MIX_W), lambda i: (i, 0)),
        out_shape=jax.ShapeDtypeStruct((t, MIX_W), BF16),
        scratch_shapes=[pltpu.VMEM((tm + CONV_HALO, MIX_W), F32)],
        compiler_params=_cparams(("arbitrary",)),
        name="conformer_conv",
    )(zd, zd, w_dw, b_dw, g_ln, b_ln)


def _outproj_kernel(x_ref, ya_ref, yb_ref, yc_ref, yd_ref, wo_ref, gt1_ref, g_ref, sc_ref, sh_ref,
                    gt2_ref, wr_ref, br_ref, wsg_ref, wsu_ref, wsd_ref,
                    xo_ref, hp_ref, idx_ref, wts_ref, rank_ref, cnt_ref, carry_ref):
    tm = x_ref.shape[0]

    @pl.when(pl.program_id(0) == 0)
    def _():
        carry_ref[...] = jnp.zeros_like(carry_ref)

    y = _dot(ya_ref[...], wo_ref[0:MIX_W, :])
    y = y + _dot(yb_ref[...], wo_ref[MIX_W:2 * MIX_W, :])
    y = y + _dot(yc_ref[...], wo_ref[2 * MIX_W:3 * MIX_W, :])
    y = y + _dot(yd_ref[...], wo_ref[3 * MIX_W:, :])
    x = x_ref[...] + gt1_ref[...] * y
    ms = jnp.mean(x * x, axis=-1, keepdims=True)
    h = (x * lax.rsqrt(ms + EPS) * g_ref[...]) * (1.0 + sc_ref[...]) + sh_ref[...]
    hb = h.astype(BF16)
    hp_ref[...] = _pack_rows(h)

    act = _dot(hb, wsg_ref[...])
    act = (act * jax.nn.sigmoid(act) * _dot(hb, wsu_ref[...])).astype(BF16)
    xo_ref[...] = x + gt2_ref[...] * _dot(act, wsd_ref[...])

    logits = _dot_nt(wr_ref[...], h, precision=lax.Precision.HIGHEST)
    scores = jax.nn.sigmoid(logits)
    cur = scores + br_ref[...]
    erow = lax.broadcasted_iota(I32, (N_EXPERTS, tm), 0).astype(F32)
    picked = jnp.zeros((N_EXPERTS, tm), F32)
    hits, idxs, wts = [], [], []
    for _ in range(TOP_K):
        cmax = jnp.max(cur, axis=0, keepdims=True)
        first = jnp.min(jnp.where(cur == cmax, erow, float(N_EXPERTS)), axis=0, keepdims=True)
        hit = erow == first
        hits.append(hit)
        idxs.append(first)
        wts.append(jnp.sum(jnp.where(hit, scores, 0.0), axis=0, keepdims=True))
        picked = jnp.where(hit, 1.0, picked)
        cur = jnp.where(hit, -jnp.inf, cur)
    wsum = wts[0]
    for w in wts[1:]:
        wsum = wsum + w
    ti = lax.broadcasted_iota(I32, (tm, tm), 0)
    tj = lax.broadcasted_iota(I32, (tm, tm), 1)
    before = jnp.where(ti < tj, 1.0, 0.0).astype(BF16)
    prior = _dot(picked.astype(BF16), before) + carry_ref[...]
    for k in range(TOP_K):
        idx_ref[k:k + 1, :] = idxs[k].astype(I32)
        wts_ref[k:k + 1, :] = wts[k] / wsum * ROUTED_SCALE
        rank_ref[k:k + 1, :] = jnp.sum(jnp.where(hits[k], prior, 0.0), axis=0,
                                       keepdims=True).astype(I32)
    carry_ref[...] = carry_ref[...] + jnp.sum(picked, axis=1, keepdims=True)
    cnt_ref[...] = jnp.broadcast_to(carry_ref[...], cnt_ref.shape).astype(I32)


def _out_projection(x, ys, w_out, gt1, g_ffn, sc2, sh2, gt2, w_rt, b_r, wsg, wsu, wsd, seq, tm=256):
    t, d = x.shape
    per_seq = seq // tm
    full = lambda i: (0, 0)
    mod_spec = pl.BlockSpec((None, 1, d), lambda i: (i // per_seq, 0, 0))
    y_spec = pl.BlockSpec((tm, MIX_W), lambda i: (i, 0))
    tok_spec = pl.BlockSpec((TOP_K, tm), lambda i: (0, i))
    return pl.pallas_call(
        _outproj_kernel,
        grid=(t // tm,),
        in_specs=[pl.BlockSpec((tm, d), lambda i: (i, 0)),
                  y_spec, y_spec, y_spec, y_spec,
                  _resident((4 * MIX_W, d), full),
                  mod_spec,
                  pl.BlockSpec((1, d), full),
                  mod_spec, mod_spec, mod_spec,
                  pl.BlockSpec((N_EXPERTS, d), full),
                  pl.BlockSpec((N_EXPERTS, 1), full),
                  _resident((d, D_SHARED), full),
                  _resident((d, D_SHARED), full),
                  _resident((D_SHARED, d), full)],
        out_specs=[pl.BlockSpec((tm, d), lambda i: (i, 0)),
                   pl.BlockSpec((tm, PACK_W), lambda i: (i, 0)),
                   tok_spec, tok_spec, tok_spec,
                   pl.BlockSpec((N_EXPERTS, 128), full)],
        out_shape=[jax.ShapeDtypeStruct((t, d), F32),
                   jax.ShapeDtypeStruct((t, PACK_W), U32),
                   jax.ShapeDtypeStruct((TOP_K, t), I32),
                   jax.ShapeDtypeStruct((TOP_K, t), F32),
                   jax.ShapeDtypeStruct((TOP_K, t), I32),
                   jax.ShapeDtypeStruct((N_EXPERTS, 128), I32)],
        scratch_shapes=[pltpu.VMEM((N_EXPERTS, 1), F32)],
        compiler_params=_cparams(("arbitrary",)),
        name="out_projection_router",
    )(x, *ys, w_out, gt1, g_ffn, sc2, sh2, gt2, w_rt, b_r, wsg, wsu, wsd)


def _row_copy(src_ref, src_row, dst_ref, dst_row, sem):
    return pltpu.make_async_copy(src_ref.at[pl.ds(src_row, 1)], dst_ref.at[pl.ds(dst_row, 1)], sem)


def _dispatch_kernel(dest_ref, hp_ref, xs_ref, sem):
    tm = hp_ref.shape[0]

    def start(t, carry):
        for k in range(TOP_K):
            _row_copy(hp_ref, t, xs_ref, dest_ref[k, t], sem).start()
        return carry
    lax.fori_loop(0, tm, start, 0)

    def wait(t, carry):
        for k in range(TOP_K):
            _row_copy(hp_ref, t, xs_ref, dest_ref[k, t], sem).wait()
        return carry
    lax.fori_loop(0, tm, wait, 0)


def _dispatch(dest_t, hp, p_rows, tm=512):
    t = hp.shape[0]
    return pl.pallas_call(
        _dispatch_kernel,
        grid=(t // tm,),
        in_specs=[pl.BlockSpec((TOP_K, tm), lambda i: (0, i), memory_space=pltpu.SMEM),
                  pl.BlockSpec((tm, PACK_W), lambda i: (i, 0))],
        out_specs=pl.BlockSpec(memory_space=pl.ANY),
        out_shape=jax.ShapeDtypeStruct((p_rows, PACK_W), U32),
        scratch_shapes=[pltpu.SemaphoreType.DMA(())],
        compiler_params=_cparams(("arbitrary",)),
        name="moe_dispatch",
    )(dest_t, hp)


def _expert_kernel(be_ref, bv_ref, x_ref, wg_ref, wu_ref, wd_ref, y_ref, wgb_ref, wub_ref, wdb_ref):
    i = pl.program_id(0)
    expert = be_ref[i]
    prev = be_ref[jnp.maximum(i - 1, 0)]

    @pl.when((i == 0) | (expert != prev))
    def _():
        wgb_ref[...] = wg_ref[...].astype(BF16)
        wub_ref[...] = wu_ref[...].astype(BF16)
        wdb_ref[...] = wd_ref[...].astype(BF16)

    valid = bv_ref[i]

    @pl.when(valid > 0)
    def _():
        lo, hi = _unpack_rows(x_ref[...])
        rows = lax.broadcasted_iota(I32, (x_ref.shape[0], 1), 0)
        xb = jnp.where(rows < valid, jnp.concatenate([lo, hi], axis=1), 0.0).astype(BF16)
        gate = _dot(xb, wgb_ref[...])
        act = (gate * jax.nn.sigmoid(gate) * _dot(xb, wub_ref[...])).astype(BF16)
        y_ref[...] = _pack_rows(_dot(act, wdb_ref[...]))

    @pl.when(valid == 0)
    def _():
        y_ref[...] = jnp.zeros_like(y_ref)


def _experts(block_e, block_valid, xs, w_gate, w_up, w_down, layer):
    p_rows = xs.shape[0]
    d = D_MODEL
    return pl.pallas_call(
        _expert_kernel,
        grid_spec=pltpu.PrefetchScalarGridSpec(
            num_scalar_prefetch=2,
            grid=(p_rows // MOE_BLOCK,),
            in_specs=[pl.BlockSpec((MOE_BLOCK, PACK_W), lambda i, be, bv: (i, 0)),
                      pl.BlockSpec((None, None, d, D_EXPERT), lambda i, be, bv: (layer, be[i], 0, 0)),
                      pl.BlockSpec((None, None, d, D_EXPERT), lambda i, be, bv: (layer, be[i], 0, 0)),
                      pl.BlockSpec((None, None, D_EXPERT, d), lambda i, be, bv: (layer, be[i], 0, 0))],
            out_specs=pl.BlockSpec((MOE_BLOCK, PACK_W), lambda i, be, bv: (i, 0)),
            scratch_shapes=[pltpu.VMEM((d, D_EXPERT), BF16),
                            pltpu.VMEM((d, D_EXPERT), BF16),
                            pltpu.VMEM((D_EXPERT, d), BF16)]),
        out_shape=jax.ShapeDtypeStruct((p_rows, PACK_W), U32),
        compiler_params=_cparams(("arbitrary",)),
        name="moe_experts",
    )(block_e, block_valid, xs, w_gate, w_up, w_down)


def _combine_kernel(dest_ref, wts_ref, x_ref, gt2_ref, gf_ref, ys_ref, o_ref, buf_ref, sem, *, final):
    tm = x_ref.shape[0]

    def start(t, carry):
        for k in range(TOP_K):
            _row_copy(ys_ref, dest_ref[k, t], buf_ref.at[k], t, sem).start()
        return carry
    lax.fori_loop(0, tm, start, 0)

    def wait(t, carry):
        for k in range(TOP_K):
            _row_copy(ys_ref, dest_ref[k, t], buf_ref.at[k], t, sem).wait()
        return carry
    lax.fori_loop(0, tm, wait, 0)

    acc_lo = jnp.zeros((tm, PACK_W), F32)
    acc_hi = jnp.zeros((tm, PACK_W), F32)
    for k in range(TOP_K):
        lo, hi = _unpack_rows(buf_ref[k])
        w = wts_ref[:, k:k + 1]
        acc_lo = acc_lo + w * lo
        acc_hi = acc_hi + w * hi
    x = x_ref[...] + gt2_ref[...] * jnp.concatenate([acc_lo, acc_hi], axis=1)
    if final:
        ms = jnp.mean(x * x, axis=-1, keepdims=True)
        x = x * lax.rsqrt(ms + EPS) * gf_ref[...]
    o_ref[...] = x


def _combine(dest_t, wts, x_mid, gt2, g_final, ys, seq, final, tm=256):
    t, d = x_mid.shape
    per_seq = seq // tm
    return pl.pallas_call(
        functools.partial(_combine_kernel, final=final),
        grid=(t // tm,),
        in_specs=[pl.BlockSpec((TOP_K, tm), lambda i: (0, i), memory_space=pltpu.SMEM),
                  pl.BlockSpec((tm, TOP_K), lambda i: (i, 0)),
                  pl.BlockSpec((tm, d), lambda i: (i, 0)),
                  pl.BlockSpec((None, 1, d), lambda i: (i // per_seq, 0, 0)),
                  pl.BlockSpec((1, d), lambda i: (0, 0)),
                  pl.BlockSpec(memory_space=pl.ANY)],
        out_specs=pl.BlockSpec((tm, d), lambda i: (i, 0)),
        out_shape=jax.ShapeDtypeStruct((t, d), F32),
        scratch_shapes=[pltpu.VMEM((TOP_K, tm, PACK_W), U32),
                        pltpu.SemaphoreType.DMA(())],
        compiler_params=_cparams(("arbitrary",)),
        name="moe_combine",
    )(dest_t, wts, x_mid, gt2, g_final, ys)


def _pad_to(a, axis, size):
    pad = [(0, 0)] * a.ndim
    pad[axis] = (0, size - a.shape[axis])
    return jnp.pad(a, pad)


def _in_weight(w_in):
    off_b = ZA_W
    off_cq = off_b + ZB_W
    off_ckv = off_cq + MLA_Q_RANK
    off_kr = off_ckv + MLA_KV_RANK
    off_d = off_kr + MLA_ROPE
    parts = [w_in[:, :off_cq],
             _pad_to(w_in[:, off_cq:off_ckv], 1, CQ_PAD),
             _pad_to(w_in[:, off_ckv:off_kr], 1, CKV_PAD),
             _pad_to(w_in[:, off_kr:off_d], 1, KR_PAD),
             w_in[:, off_d:]]
    return jnp.concatenate(parts, axis=1).astype(BF16)


def _mla_weights(w_uq, w_ukv):
    wq = w_uq.reshape(MLA_Q_RANK, N_HEADS, MLA_QK)
    wq = _pad_to(_pad_to(wq, 2, MLA_QK_PAD), 0, CQ_PAD).reshape(CQ_PAD, N_HEADS * MLA_QK_PAD)
    wkv = _pad_to(w_ukv.reshape(MLA_KV_RANK, N_HEADS, MLA_NOPE + HEAD_DIM), 0, CKV_PAD)
    wk = _pad_to(wkv[:, :, :MLA_NOPE], 2, MLA_QK_PAD).reshape(CKV_PAD, N_HEADS * MLA_QK_PAD)
    wv = wkv[:, :, MLA_NOPE:].reshape(CKV_PAD, MIX_W)
    return wq.astype(BF16), wk.astype(BF16), wv.astype(BF16)


def _route_tables(idx_t, rank_t, counts, p_rows):
    m = MOE_BLOCK
    padded = (counts + m - 1) // m * m
    pend = jnp.cumsum(padded)
    pstart = pend - padded
    dest_t = jnp.take(pstart, idx_t) + rank_t
    blk_start = jnp.arange(p_rows // m, dtype=I32) * m
    block_e = jnp.minimum(jnp.searchsorted(pend, blk_start, side="right"), N_EXPERTS - 1).astype(I32)
    seg_end = jnp.take(pstart + counts, block_e)
    block_valid = jnp.clip(seg_end - blk_start, 0, m).astype(I32)
    return dest_t.astype(I32), block_e, block_valid


def kernel(x, c, positions, w_mod, b_mod, g_attn, w_in, g_gmlp_v, w_spatial, b_spatial, g_mla_q, w_mla_uq, g_mla_kv, w_mla_ukv, w_conv_dw, b_conv_dw, g_conv_ln, b_conv_ln, w_out, g_ffn, w_router, b_router, w_exp_gate, w_exp_up, w_exp_down, w_sh_gate, w_sh_up, w_sh_down, g_final):
    batch, seq, d = x.shape
    n_layers = w_mod.shape[0]
    t = batch * seq
    p_rows = t * TOP_K + N_EXPERTS * MOE_BLOCK

    xf = x.reshape(t, d)
    pos_col = positions.reshape(t, 1)
    pos_blk = positions.reshape(batch, seq // MOBA_BLOCK, MOBA_BLOCK)
    slopes = 2.0 ** (-8.0 * jnp.arange(1, N_HEADS + 1, dtype=F32) / N_HEADS)
    half = MLA_ROPE // 2
    inv_freq = ROPE_THETA ** (-jnp.arange(half, dtype=F32) * (2.0 / MLA_ROPE))
    invf = _pad_to(jnp.concatenate([inv_freq, inv_freq]), 0, KR_PAD).reshape(1, KR_PAD)

    mod = _modulation(c, w_mod, b_mod).reshape(n_layers, N_MOD, batch, 1, d)

    routed = None
    for l in range(n_layers):
        sh1, sc1, gt1, sh2, sc2, gt2 = (mod[l, j] for j in range(N_MOD))
        if routed is not None:
            xf = _combine(*routed, seq=seq, final=False)

        za, zb, zc, zd = _in_projection(xf, g_attn[l].reshape(1, d), sc1, sh1,
                                        _in_weight(w_in[l]), seq)
        y_a = _moba_attention(za, pos_col, pos_blk, slopes, batch, seq)
        y_b = _spatial_gating(zb, g_gmlp_v[l], w_spatial[l], b_spatial[l])
        wq, wk, wv = _mla_weights(w_mla_uq[l], w_mla_ukv[l])
        q, k, v = _mla_prep(zc, pos_col, invf,
                            _pad_to(g_mla_q[l], 0, CQ_PAD).reshape(1, CQ_PAD),
                            _pad_to(g_mla_kv[l], 0, CKV_PAD).reshape(1, CKV_PAD), wq, wk, wv)
        y_c = _mla_attention(q, k, v, batch, seq)
        y_d = _conformer_conv(zd, w_conv_dw[l].reshape(CONV_WIDTH, MIX_W),
                              b_conv_dw[l].reshape(1, MIX_W), g_conv_ln[l].reshape(1, MIX_W),
                              b_conv_ln[l].reshape(1, MIX_W), seq)

        x_mid, hp, idx_t, wts_t, rank_t, cnt = _out_projection(
            xf, (y_a, y_b, y_c, y_d), w_out[l].astype(BF16), gt1, g_ffn[l].reshape(1, d),
            sc2, sh2, gt2, w_router[l].T, b_router[l].reshape(N_EXPERTS, 1),
            w_sh_gate[l].astype(BF16), w_sh_up[l].astype(BF16), w_sh_down[l].astype(BF16), seq)

        dest_t, block_e, block_valid = _route_tables(idx_t, rank_t, cnt[:, 0], p_rows)
        xs = _dispatch(dest_t, hp, p_rows)
        ys = _experts(block_e, block_valid, xs, w_exp_gate, w_exp_up, w_exp_down, l)
        routed = (dest_t, wts_t.T, x_mid, gt2, g_final.reshape(1, d), ys)

    out = _combine(*routed, seq=seq, final=True)
    return out.reshape(batch, seq, d)
```

```python
import functools

import jax
import jax.numpy as jnp
from jax import lax
from jax.experimental import pallas as pl
from jax.experimental.pallas import tpu as pltpu

F32 = jnp.float32
BF16 = jnp.bfloat16
I32 = jnp.int32
U32 = jnp.uint32

D_MODEL = 2048
HEAD_DIM = 128
N_HEADS = 4
MIX_W = N_HEADS * HEAD_DIM
MOBA_BLOCK = 256
MOBA_TOPK = 3
GMLP_CHUNK = 128
MLA_Q_RANK = 448
MLA_KV_RANK = 160
MLA_NOPE = 128
MLA_ROPE = 64
MLA_QK = MLA_NOPE + MLA_ROPE
ROPE_THETA = 10000.0
CONV_WIDTH = 31
CONV_HALO = 32
N_EXPERTS = 64
TOP_K = 8
D_EXPERT = 512
D_SHARED = 512
ROUTED_SCALE = 2.5
MOE_BLOCK = 256
N_MOD = 6
EPS = 1e-6
NEG = -1e30

ATT_TQ = 1024
ATT_SUB = 256
ATT_TK = 512
ATT_W = 2 * HEAD_DIM

ZA_W = 3 * MIX_W
ZB_W = 2 * MIX_W
CQ_PAD = 512
CKV_PAD = 256
KR_PAD = 128
ZC_W = CQ_PAD + CKV_PAD + KR_PAD
ZD_W = 2 * MIX_W

VMEM_LIMIT = 56 * 1024 * 1024
PACK_W = D_MODEL // 2


def _cparams(sem):
    return pltpu.CompilerParams(dimension_semantics=sem, vmem_limit_bytes=VMEM_LIMIT)


def _resident(shape, index_map):
    return pl.BlockSpec(shape, index_map, pipeline_mode=pl.Buffered(1))


def _dot(a, b):
    return jnp.dot(a, b, preferred_element_type=F32)


def _dot_nt(a, b, precision=None):
    return lax.dot_general(a, b, (((1,), (1,)), ((), ())), preferred_element_type=F32,
                           precision=precision)


def _pack_rows(y):
    half = y.shape[1] // 2
    lo = lax.bitcast_convert_type(y[:, :half].astype(BF16).astype(F32), U32)
    hi = lax.bitcast_convert_type(y[:, half:].astype(BF16).astype(F32), U32)
    return (lo >> 16) | hi


def _unpack_rows(u):
    lo = lax.bitcast_convert_type(u << 16, F32)
    hi = lax.bitcast_convert_type(u & jnp.uint32(0xFFFF0000), F32)
    return lo, hi


def _mod_kernel(c_ref, w_ref, b_ref, o_ref):
    c = c_ref[...]
    a = c * jax.nn.sigmoid(c)
    o_ref[...] = jnp.dot(a, w_ref[...], preferred_element_type=F32,
                         precision=lax.Precision.HIGHEST) + b_ref[...]


def _modulation(c, w_mod, b_mod):
    n_layers, d, _ = w_mod.shape
    b = c.shape[0]
    return pl.pallas_call(
        _mod_kernel,
        grid=(n_layers, N_MOD),
        in_specs=[pl.BlockSpec((b, d), lambda l, j: (0, 0)),
                  pl.BlockSpec((None, d, d), lambda l, j: (l, 0, j)),
                  pl.BlockSpec((None, None, 1, d), lambda l, j: (l, j, 0, 0))],
        out_specs=pl.BlockSpec((None, None, b, d), lambda l, j: (l, j, 0, 0)),
        out_shape=jax.ShapeDtypeStruct((n_layers, N_MOD, b, d), F32),
        compiler_params=_cparams(("arbitrary", "arbitrary")),
        name="modulation",
    )(c, w_mod, b_mod.reshape(n_layers, N_MOD, 1, d))


def _inproj_kernel(x_ref, g_ref, sc_ref, sh_ref, w_ref, q_ref, ka_ref, va_ref, zb_ref, zc_ref, zd_ref,
                   *, seq):
    tm = x_ref.shape[0]
    x = x_ref[...]
    ms = jnp.mean(x * x, axis=-1, keepdims=True)
    h = (x * lax.rsqrt(ms + EPS) * g_ref[...]) * (1.0 + sc_ref[...]) + sh_ref[...]
    hb = h.astype(BF16)

    q_ref[...] = _dot(hb, w_ref[:, 0:MIX_W]).astype(BF16)
    pos = (pl.program_id(0) * tm) % seq + lax.broadcasted_iota(I32, (tm, HEAD_DIM), 0)
    lane = lax.broadcasted_iota(I32, (tm, HEAD_DIM), 1)
    onehot = jnp.where(pos // MOBA_BLOCK == lane, 1.0, 0.0).astype(BF16)
    ones = jnp.ones((tm, HEAD_DIM), BF16)
    k = _dot(hb, w_ref[:, MIX_W:2 * MIX_W]).astype(BF16)
    v = _dot(hb, w_ref[:, 2 * MIX_W:ZA_W]).astype(BF16)
    for hd in range(N_HEADS):
        src = slice(hd * HEAD_DIM, (hd + 1) * HEAD_DIM)
        ka_ref[:, hd * ATT_W:hd * ATT_W + HEAD_DIM] = k[:, src]
        ka_ref[:, hd * ATT_W + HEAD_DIM:(hd + 1) * ATT_W] = onehot
        va_ref[:, hd * ATT_W:hd * ATT_W + HEAD_DIM] = v[:, src]
        va_ref[:, hd * ATT_W + HEAD_DIM:(hd + 1) * ATT_W] = ones

    off = ZA_W
    for ref in (zb_ref, zc_ref, zd_ref):
        w = ref.shape[1]
        ref[...] = _dot(hb, w_ref[:, off:off + w]).astype(BF16)
        off += w


def _in_projection(x, g, sc, sh, w_in_p, seq, tm=512):
    t, d = x.shape
    per_seq = seq // tm
    widths = (MIX_W, N_HEADS * ATT_W, N_HEADS * ATT_W, ZB_W, ZC_W, ZD_W)
    mod_spec = pl.BlockSpec((None, 1, d), lambda i: (i // per_seq, 0, 0))
    return pl.pallas_call(
        functools.partial(_inproj_kernel, seq=seq),
        grid=(t // tm,),
        in_specs=[pl.BlockSpec((tm, d), lambda i: (i, 0)),
                  pl.BlockSpec((1, d), lambda i: (0, 0)),
                  mod_spec, mod_spec,
                  _resident((d, ZA_W + ZB_W + ZC_W + ZD_W), lambda i: (0, 0))],
        out_specs=[pl.BlockSpec((tm, w), lambda i: (i, 0)) for w in widths],
        out_shape=[jax.ShapeDtypeStruct((t, w), BF16) for w in widths],
        compiler_params=_cparams(("arbitrary",)),
        name="in_projection",
    )(x, g, sc, sh, w_in_p)


def _attn_kernel(*refs, moba):
    if moba:
        slope_ref, q_ref, k_ref, v_ref, pq_ref, pk_ref, o_ref, kmean_ref, qa_ref, m_ref, acc_ref = refs
    else:
        q_ref, k_ref, v_ref, o_ref, m_ref, acc_ref = refs
        qa_ref = q_ref
    tq, sub, tk = ATT_TQ, ATT_SUB, ATT_TK
    nsub = tq // sub
    i = pl.program_id(2)

    if moba:
        nb_pad = kmean_ref.shape[0]
        nb = k_ref.shape[0] // MOBA_BLOCK

        @pl.when(i == 0)
        def _():
            kmean_ref[...] = jnp.zeros_like(kmean_ref)

            def body(n, carry):
                start = pl.multiple_of(n * MOBA_BLOCK, MOBA_BLOCK)
                kb = k_ref[pl.ds(start, MOBA_BLOCK), 0:HEAD_DIM].astype(F32)
                kmean_ref[pl.ds(n, 1), :] = jnp.mean(kb, axis=0, keepdims=True)
                return carry
            lax.fori_loop(0, nb, body, 0)

        slope = slope_ref[pl.program_id(1)]
        col = lax.broadcasted_iota(I32, (sub, nb_pad), 1)
        colf = col.astype(F32)
        pq_s = []
        for c in range(nsub):
            rows = slice(c * sub, (c + 1) * sub)
            own = i * nsub + c
            qf = q_ref[rows, :].astype(F32)
            gate = _dot_nt(qf, kmean_ref[...], precision=lax.Precision.HIGHEST)
            gate = jnp.where(col < own, gate, NEG)
            sel = col == own
            for _ in range(MOBA_TOPK):
                gmax = jnp.max(gate, axis=-1, keepdims=True)
                first = jnp.min(jnp.where(gate == gmax, colf, float(nb_pad)), axis=-1, keepdims=True)
                hit = colf == first
                sel = sel | (hit & (col < own))
                gate = jnp.where(hit, -jnp.inf, gate)
            qa_ref[rows, 0:HEAD_DIM] = (qf * (HEAD_DIM ** -0.5)).astype(BF16)
            qa_ref[rows, HEAD_DIM:] = jnp.where(sel, 0.0, NEG).astype(BF16)
            pq_s.append(slope * pq_ref[rows, :].astype(F32))

    m_ref[...] = jnp.full(m_ref.shape, NEG, F32)
    acc_ref[...] = jnp.zeros_like(acc_ref)

    def step(c, chunk, mask_off):
        rows = slice(c * sub, (c + 1) * sub)
        start = pl.multiple_of(chunk * tk, tk)
        s = _dot_nt(qa_ref[rows, :], k_ref[pl.ds(start, tk), :])
        if moba:
            s = s - jnp.abs(pq_s[c] - slope * pk_ref[pl.ds(chunk, 1), :].astype(F32))
        if mask_off is not None:
            qi = lax.broadcasted_iota(I32, (sub, tk), 0)
            ki = lax.broadcasted_iota(I32, (sub, tk), 1)
            s = jnp.where(ki <= qi + mask_off, s, NEG)
        m_old = m_ref[rows, :]
        m_new = jnp.maximum(m_old, jnp.max(s, axis=-1, keepdims=True))
        p = jnp.exp(s - m_new).astype(BF16)
        acc_ref[rows, :] = jnp.exp(m_old - m_new) * acc_ref[rows, :] + _dot(p, v_ref[pl.ds(start, tk), :])
        m_ref[rows, :] = m_new

    per_tile = tq // tk
    for c in range(nsub):
        q_lo = c * sub
        for jj in reversed(range(per_tile)):
            k_lo = jj * tk
            if k_lo > q_lo + sub - 1:
                continue
            needs_mask = k_lo + tk - 1 > q_lo
            step(c, i * per_tile + jj, q_lo - k_lo if needs_mask else None)

    def body(j, carry):
        for c in range(nsub):
            step(c, j, None)
        return carry
    lax.fori_loop(0, i * per_tile, body, 0)

    acc = acc_ref[...]
    o_ref[...] = (acc[:, :HEAD_DIM] / acc[:, HEAD_DIM:]).astype(o_ref.dtype)


def _attn_scratch():
    return [pltpu.VMEM((ATT_TQ, 1), F32), pltpu.VMEM((ATT_TQ, ATT_W), F32)]


def _moba_attention(q, ka, va, pos_col, pos_chunks, slopes, batch, seq):
    t = q.shape[0]
    nq = seq // ATT_TQ
    kv_spec = pl.BlockSpec((seq, ATT_W), lambda b, h, i, s: (b, h))
    return pl.pallas_call(
        functools.partial(_attn_kernel, moba=True),
        grid_spec=pltpu.PrefetchScalarGridSpec(
            num_scalar_prefetch=1,
            grid=(batch, N_HEADS, nq),
            in_specs=[pl.BlockSpec((ATT_TQ, HEAD_DIM), lambda b, h, i, s: (b * nq + i, h)),
                      kv_spec, kv_spec,
                      pl.BlockSpec((ATT_TQ, 1), lambda b, h, i, s: (b * nq + i, 0)),
                      pl.BlockSpec((None, seq // ATT_TK, ATT_TK), lambda b, h, i, s: (b, 0, 0))],
            out_specs=pl.BlockSpec((ATT_TQ, HEAD_DIM), lambda b, h, i, s: (b * nq + i, h)),
            scratch_shapes=[pltpu.VMEM((HEAD_DIM, HEAD_DIM), F32),
                            pltpu.VMEM((ATT_TQ, ATT_W), BF16)] + _attn_scratch()),
        out_shape=jax.ShapeDtypeStruct((t, MIX_W), BF16),
        compiler_params=_cparams(("arbitrary", "arbitrary", "arbitrary")),
        name="moba_attention",
    )(slopes, q, ka, va, pos_col, pos_chunks)


def _mla_attention(q, k, va, batch, seq):
    t = q.shape[0]
    nq = seq // ATT_TQ
    kv_spec = pl.BlockSpec((seq, ATT_W), lambda b, h, i: (b, h))
    return pl.pallas_call(
        functools.partial(_attn_kernel, moba=False),
        grid=(batch, N_HEADS, nq),
        in_specs=[pl.BlockSpec((ATT_TQ, ATT_W), lambda b, h, i: (b * nq + i, h)),
                  kv_spec, kv_spec],
        out_specs=pl.BlockSpec((ATT_TQ, HEAD_DIM), lambda b, h, i: (b * nq + i, h)),
        out_shape=jax.ShapeDtypeStruct((t, MIX_W), BF16),
        scratch_shapes=_attn_scratch(),
        compiler_params=_cparams(("arbitrary", "arbitrary", "arbitrary")),
        name="mla_attention",
    )(q, k, va)


def _gelu_tanh(x):
    return 0.5 * x * (1.0 + jnp.tanh(0.7978845608028654 * (x + 0.044715 * x * x * x)))


def _gmlp_kernel(z_ref, gv_ref, ws_ref, bs_ref, o_ref):
    tm = z_ref.shape[0]
    ck = GMLP_CHUNK
    z = _gelu_tanh(z_ref[...].astype(F32))
    row = lax.broadcasted_iota(I32, (ck, ck), 0)
    colm = lax.broadcasted_iota(I32, (ck, ck), 1)
    for g in range(N_HEADS):
        lanes = slice(g * HEAD_DIM, (g + 1) * HEAD_DIM)
        u = z[:, lanes]
        vv = z[:, MIX_W + g * HEAD_DIM:MIX_W + (g + 1) * HEAD_DIM]
        ms = jnp.mean(vv * vv, axis=-1, keepdims=True)
        vn = (vv * lax.rsqrt(ms + EPS) * gv_ref[g:g + 1, :]).astype(BF16)
        w = jnp.where(colm <= row, ws_ref[g], 0.0).astype(BF16)
        bias = bs_ref[g]
        for c in range(tm // ck):
            rows = slice(c * ck, (c + 1) * ck)
            mixed = _dot(w, vn[rows]) + bias
            o_ref[rows, lanes] = (u[rows] * mixed).astype(o_ref.dtype)


def _spatial_gating(zb, g_v, w_s, b_s, tm=512):
    t = zb.shape[0]
    ck = GMLP_CHUNK
    return pl.pallas_call(
        _gmlp_kernel,
        grid=(t // tm,),
        in_specs=[pl.BlockSpec((tm, ZB_W), lambda i: (i, 0)),
                  pl.BlockSpec((N_HEADS, HEAD_DIM), lambda i: (0, 0)),
                  pl.BlockSpec((N_HEADS, ck, ck), lambda i: (0, 0, 0)),
                  pl.BlockSpec((N_HEADS, ck, 1), lambda i: (0, 0, 0))],
        out_specs=pl.BlockSpec((tm, MIX_W), lambda i: (i, 0)),
        out_shape=jax.ShapeDtypeStruct((t, MIX_W), BF16),
        compiler_params=_cparams(("arbitrary",)),
        name="spatial_gating",
    )(zb, g_v, w_s, b_s.reshape(N_HEADS, ck, 1))


def _mla_prep_kernel(z_ref, pos_ref, invf_ref, gq_ref, gkv_ref, wq_ref, wk_ref, wv_ref,
                     q_ref, k_ref, v_ref):
    tm = z_ref.shape[0]
    z = z_ref[...].astype(F32)
    cq = z[:, :CQ_PAD]
    ckv = z[:, CQ_PAD:CQ_PAD + CKV_PAD]
    kr = z[:, CQ_PAD + CKV_PAD:]
    qn = cq * lax.rsqrt(jnp.sum(cq * cq, -1, keepdims=True) * (1.0 / MLA_Q_RANK) + EPS)
    qn = (qn * gq_ref[...]).astype(BF16)
    kvn = ckv * lax.rsqrt(jnp.sum(ckv * ckv, -1, keepdims=True) * (1.0 / MLA_KV_RANK) + EPS)
    kvn = (kvn * gkv_ref[...]).astype(BF16)

    ang = pos_ref[...].astype(F32) * invf_ref[...]
    lane = lax.broadcasted_iota(I32, ang.shape, 1)
    half = MLA_ROPE // 2
    cos = jnp.cos(ang)
    sin = jnp.sin(ang)
    sin_lo = jnp.where(lane < half, -sin, 0.0)
    sin_hi = jnp.where((lane >= half) & (lane < 2 * half), sin, 0.0)

    def rope(r):
        return (r * cos + pltpu.roll(r, KR_PAD - half, 1) * sin_lo
                + pltpu.roll(r, half, 1) * sin_hi)

    q = _dot(qn, wq_ref[...])
    kn = _dot(kvn, wk_ref[...])
    v = _dot(kvn, wv_ref[...]).astype(BF16)
    k_rope = rope(kr)
    scale = MLA_QK ** -0.5
    ones = jnp.ones((tm, HEAD_DIM), BF16)
    for h in range(N_HEADS):
        a = h * ATT_W
        b = a + MLA_NOPE
        q_ref[:, a:b] = (q[:, a:b] * scale).astype(q_ref.dtype)
        q_ref[:, b:a + ATT_W] = (rope(q[:, b:a + ATT_W]) * scale).astype(q_ref.dtype)
        k_ref[:, a:b] = kn[:, a:b].astype(k_ref.dtype)
        k_ref[:, b:a + ATT_W] = k_rope.astype(k_ref.dtype)
        v_ref[:, a:b] = v[:, h * HEAD_DIM:(h + 1) * HEAD_DIM]
        v_ref[:, b:a + ATT_W] = ones


def _mla_prep(zc, pos_col, invf, gq, gkv, wq, wk, wv, tm=512):
    t = zc.shape[0]
    qk_w = N_HEADS * ATT_W
    full = lambda i: (0, 0)
    out_spec = pl.BlockSpec((tm, qk_w), lambda i: (i, 0))
    out_shape = jax.ShapeDtypeStruct((t, qk_w), BF16)
    return pl.pallas_call(
        _mla_prep_kernel,
        grid=(t // tm,),
        in_specs=[pl.BlockSpec((tm, ZC_W), lambda i: (i, 0)),
                  pl.BlockSpec((tm, 1), lambda i: (i, 0)),
                  pl.BlockSpec((1, KR_PAD), full),
                  pl.BlockSpec((1, CQ_PAD), full),
                  pl.BlockSpec((1, CKV_PAD), full),
                  pl.BlockSpec((CQ_PAD, qk_w), full),
                  pl.BlockSpec((CKV_PAD, qk_w), full),
                  pl.BlockSpec((CKV_PAD, MIX_W), full)],
        out_specs=[out_spec, out_spec, out_spec],
        out_shape=[out_shape, out_shape, out_shape],
        compiler_params=_cparams(("arbitrary",)),
        name="mla_prep",
    )(zc, pos_col, invf, gq, gkv, wq, wk, wv)


def _glu(z):
    z = z.astype(F32)
    return z[:, :MIX_W] * jax.nn.sigmoid(z[:, MIX_W:])


def _conv_kernel(z_ref, zprev_ref, w_ref, b_ref, g_ref, beta_ref, o_ref, ybuf_ref, *, per_seq):
    tm = z_ref.shape[0]
    first = (pl.program_id(0) % per_seq) == 0
    ybuf_ref[0:CONV_HALO, :] = jnp.where(first, 0.0, _glu(zprev_ref[...]))
    ybuf_ref[CONV_HALO:, :] = _glu(z_ref[...])
    rows = 64
    shift = CONV_HALO - (CONV_WIDTH - 1)
    for r in range(tm // rows):
        acc = jnp.zeros((rows, MIX_W), F32) + b_ref[...]
        for j in range(CONV_WIDTH):
            acc = acc + w_ref[j:j + 1, :] * ybuf_ref[pl.ds(r * rows + shift + j, rows), :]
        mu = jnp.mean(acc, axis=-1, keepdims=True)
        xc = acc - mu
        y = xc * lax.rsqrt(jnp.mean(xc * xc, axis=-1, keepdims=True) + EPS)
        y = y * g_ref[...] + beta_ref[...]
        o_ref[r * rows:(r + 1) * rows, :] = (y * jax.nn.sigmoid(y)).astype(o_ref.dtype)


def _conformer_conv(zd, w_dw, b_dw, g_ln, b_ln, seq, tm=512):
    t = zd.shape[0]
    per_seq = seq // tm
    halo_blocks = tm // CONV_HALO
    full = lambda i: (0, 0)
    return pl.pallas_call(
        functools.partial(_conv_kernel, per_seq=per_seq),
        grid=(t // tm,),
        in_specs=[pl.BlockSpec((tm, ZD_W), lambda i: (i, 0)),
                  pl.BlockSpec((CONV_HALO, ZD_W),
                               lambda i: (jnp.maximum(i * halo_blocks - 1, 0), 0)),
                  pl.BlockSpec((CONV_WIDTH, MIX_W), full),
                  pl.BlockSpec((1, MIX_W), full),
                  pl.BlockSpec((1, MIX_W), full),
                  pl.BlockSpec((1, MIX_W), full)],
        out_specs=pl.BlockSpec((tm, MIX_W), lambda i: (i, 0)),
        out_shape=jax.ShapeDtypeStruct((t, MIX_W), BF16),
        scratch_shapes=[pltpu.VMEM((tm + CONV_HALO, MIX_W), F32)],
        compiler_params=_cparams(("arbitrary",)),
        name="conformer_conv",
    )(zd, zd, w_dw, b_dw, g_ln, b_ln)


def _outproj_kernel(x_ref, ya_ref, yb_ref, yc_ref, yd_ref, wo_ref, gt1_ref, g_ref, sc_ref, sh_ref,
                    gt2_ref, wr_ref, br_ref, wsg_ref, wsu_ref, wsd_ref,
                    xo_ref, hp_ref, idx_ref, wts_ref, rank_ref, cnt_ref, carry_ref):
    tm = x_ref.shape[0]

    @pl.when(pl.program_id(0) == 0)
    def _():
        carry_ref[...] = jnp.zeros_like(carry_ref)

    y = _dot(ya_ref[...], wo_ref[0:MIX_W, :])
    y = y + _dot(yb_ref[...], wo_ref[MIX_W:2 * MIX_W, :])
    y = y + _dot(yc_ref[...], wo_ref[2 * MIX_W:3 * MIX_W, :])
    y = y + _dot(yd_ref[...], wo_ref[3 * MIX_W:, :])
    x = x_ref[...] + gt1_ref[...] * y
    ms = jnp.mean(x * x, axis=-1, keepdims=True)
    h = (x * lax.rsqrt(ms + EPS) * g_ref[...]) * (1.0 + sc_ref[...]) + sh_ref[...]
    hb = h.astype(BF16)
    hp_ref[...] = _pack_rows(h)

    act = _dot(hb, wsg_ref[...])
    act = (act * jax.nn.sigmoid(act) * _dot(hb, wsu_ref[...])).astype(BF16)
    xo_ref[...] = x + gt2_ref[...] * _dot(act, wsd_ref[...])

    logits = _dot_nt(wr_ref[...], h, precision=lax.Precision.HIGHEST)
    scores = jax.nn.sigmoid(logits)
    cur = scores + br_ref[...]
    erow = lax.broadcasted_iota(I32, (N_EXPERTS, tm), 0).astype(F32)
    picked = jnp.zeros((N_EXPERTS, tm), F32)
    hits, idxs, wts = [], [], []
    for _ in range(TOP_K):
        cmax = jnp.max(cur, axis=0, keepdims=True)
        first = jnp.min(jnp.where(cur == cmax, erow, float(N_EXPERTS)), axis=0, keepdims=True)
        hit = erow == first
        hits.append(hit)
        idxs.append(first)
        wts.append(jnp.sum(jnp.where(hit, scores, 0.0), axis=0, keepdims=True))
        picked = jnp.where(hit, 1.0, picked)
        cur = jnp.where(hit, -jnp.inf, cur)
    wsum = wts[0]
    for w in wts[1:]:
        wsum = wsum + w
    ti = lax.broadcasted_iota(I32, (tm, tm), 0)
    tj = lax.broadcasted_iota(I32, (tm, tm), 1)
    before = jnp.where(ti < tj, 1.0, 0.0).astype(BF16)
    prior = _dot(picked.astype(BF16), before) + carry_ref[...]
    for k in range(TOP_K):
        idx_ref[k:k + 1, :] = idxs[k].astype(I32)
        wts_ref[k:k + 1, :] = wts[k] / wsum * ROUTED_SCALE
        rank_ref[k:k + 1, :] = jnp.sum(jnp.where(hits[k], prior, 0.0), axis=0,
                                       keepdims=True).astype(I32)
    carry_ref[...] = carry_ref[...] + jnp.sum(picked, axis=1, keepdims=True)
    cnt_ref[...] = jnp.broadcast_to(carry_ref[...], cnt_ref.shape).astype(I32)


def _out_projection(x, ys, w_out, gt1, g_ffn, sc2, sh2, gt2, w_rt, b_r, wsg, wsu, wsd, seq, tm=256):
    t, d = x.shape
    per_seq = seq // tm
    full = lambda i: (0, 0)
    mod_spec = pl.BlockSpec((None, 1, d), lambda i: (i // per_seq, 0, 0))
    y_spec = pl.BlockSpec((tm, MIX_W), lambda i: (i, 0))
    tok_spec = pl.BlockSpec((TOP_K, tm), lambda i: (0, i))
    return pl.pallas_call(
        _outproj_kernel,
        grid=(t // tm,),
        in_specs=[pl.BlockSpec((tm, d), lambda i: (i, 0)),
                  y_spec, y_spec, y_spec, y_spec,
                  _resident((4 * MIX_W, d), full),
                  mod_spec,
                  pl.BlockSpec((1, d), full),
                  mod_spec, mod_spec, mod_spec,
                  pl.BlockSpec((N_EXPERTS, d), full),
                  pl.BlockSpec((N_EXPERTS, 1), full),
                  _resident((d, D_SHARED), full),
                  _resident((d, D_SHARED), full),
                  _resident((D_SHARED, d), full)],
        out_specs=[pl.BlockSpec((tm, d), lambda i: (i, 0)),
                   pl.BlockSpec((tm, PACK_W), lambda i: (i, 0)),
                   tok_spec, tok_spec, tok_spec,
                   pl.BlockSpec((N_EXPERTS, 128), full)],
        out_shape=[jax.ShapeDtypeStruct((t, d), F32),
                   jax.ShapeDtypeStruct((t, PACK_W), U32),
                   jax.ShapeDtypeStruct((TOP_K, t), I32),
                   jax.ShapeDtypeStruct((TOP_K, t), F32),
                   jax.ShapeDtypeStruct((TOP_K, t), I32),
                   jax.ShapeDtypeStruct((N_EXPERTS, 128), I32)],
        scratch_shapes=[pltpu.VMEM((N_EXPERTS, 1), F32)],
        compiler_params=_cparams(("arbitrary",)),
        name="out_projection_router",
    )(x, *ys, w_out, gt1, g_ffn, sc2, sh2, gt2, w_rt, b_r, wsg, wsu, wsd)


def _row_copy(src_ref, src_row, dst_ref, dst_row, sem):
    return pltpu.make_async_copy(src_ref.at[pl.ds(src_row, 1)], dst_ref.at[pl.ds(dst_row, 1)], sem)


def _dispatch_kernel(dest_ref, hp_ref, xs_ref, sem):
    tm = hp_ref.shape[0]

    def start(t, carry):
        for k in range(TOP_K):
            _row_copy(hp_ref, t, xs_ref, dest_ref[k, t], sem).start()
        return carry
    lax.fori_loop(0, tm, start, 0)

    def wait(t, carry):
        for k in range(TOP_K):
            _row_copy(hp_ref, t, xs_ref, dest_ref[k, t], sem).wait()
        return carry
    lax.fori_loop(0, tm, wait, 0)


def _dispatch(dest_t, hp, p_rows, tm=512):
    t = hp.shape[0]
    return pl.pallas_call(
        _dispatch_kernel,
        grid=(t // tm,),
        in_specs=[pl.BlockSpec((TOP_K, tm), lambda i: (0, i), memory_space=pltpu.SMEM),
                  pl.BlockSpec((tm, PACK_W), lambda i: (i, 0))],
        out_specs=pl.BlockSpec(memory_space=pl.ANY),
        out_shape=jax.ShapeDtypeStruct((p_rows, PACK_W), U32),
        scratch_shapes=[pltpu.SemaphoreType.DMA(())],
        compiler_params=_cparams(("arbitrary",)),
        name="moe_dispatch",
    )(dest_t, hp)


def _expert_kernel(be_ref, bv_ref, x_ref, wg_ref, wu_ref, wd_ref, y_ref, wgb_ref, wub_ref, wdb_ref):
    i = pl.program_id(0)
    expert = be_ref[i]
    prev = be_ref[jnp.maximum(i - 1, 0)]

    @pl.when((i == 0) | (expert != prev))
    def _():
        wgb_ref[...] = wg_ref[...].astype(BF16)
        wub_ref[...] = wu_ref[...].astype(BF16)
        wdb_ref[...] = wd_ref[...].astype(BF16)

    valid = bv_ref[i]

    @pl.when(valid > 0)
    def _():
        lo, hi = _unpack_rows(x_ref[...])
        rows = lax.broadcasted_iota(I32, (x_ref.shape[0], 1), 0)
        xb = jnp.where(rows < valid, jnp.concatenate([lo, hi], axis=1), 0.0).astype(BF16)
        gate = _dot(xb, wgb_ref[...])
        act = (gate * jax.nn.sigmoid(gate) * _dot(xb, wub_ref[...])).astype(BF16)
        y_ref[...] = _pack_rows(_dot(act, wdb_ref[...]))

    @pl.when(valid == 0)
    def _():
        y_ref[...] = jnp.zeros_like(y_ref)


def _experts(block_e, block_valid, xs, w_gate, w_up, w_down, layer):
    p_rows = xs.shape[0]
    d = D_MODEL
    return pl.pallas_call(
        _expert_kernel,
        grid_spec=pltpu.PrefetchScalarGridSpec(
            num_scalar_prefetch=2,
            grid=(p_rows // MOE_BLOCK,),
            in_specs=[pl.BlockSpec((MOE_BLOCK, PACK_W), lambda i, be, bv: (i, 0)),
                      pl.BlockSpec((None, None, d, D_EXPERT), lambda i, be, bv: (layer, be[i], 0, 0)),
                      pl.BlockSpec((None, None, d, D_EXPERT), lambda i, be, bv: (layer, be[i], 0, 0)),
                      pl.BlockSpec((None, None, D_EXPERT, d), lambda i, be, bv: (layer, be[i], 0, 0))],
            out_specs=pl.BlockSpec((MOE_BLOCK, PACK_W), lambda i, be, bv: (i, 0)),
            scratch_shapes=[pltpu.VMEM((d, D_EXPERT), BF16),
                            pltpu.VMEM((d, D_EXPERT), BF16),
                            pltpu.VMEM((D_EXPERT, d), BF16)]),
        out_shape=jax.ShapeDtypeStruct((p_rows, PACK_W), U32),
        compiler_params=_cparams(("arbitrary",)),
        name="moe_experts",
    )(block_e, block_valid, xs, w_gate, w_up, w_down)


def _combine_kernel(dest_ref, wts_ref, x_ref, gt2_ref, gf_ref, ys_ref, o_ref, buf_ref, sem, *, final):
    tm = x_ref.shape[0]

    def start(t, carry):
        for k in range(TOP_K):
            _row_copy(ys_ref, dest_ref[k, t], buf_ref.at[k], t, sem).start()
        return carry
    lax.fori_loop(0, tm, start, 0)

    def wait(t, carry):
        for k in range(TOP_K):
            _row_copy(ys_ref, dest_ref[k, t], buf_ref.at[k], t, sem).wait()
        return carry
    lax.fori_loop(0, tm, wait, 0)

    acc_lo = jnp.zeros((tm, PACK_W), F32)
    acc_hi = jnp.zeros((tm, PACK_W), F32)
    for k in range(TOP_K):
        lo, hi = _unpack_rows(buf_ref[k])
        w = wts_ref[:, k:k + 1]
        acc_lo = acc_lo + w * lo
        acc_hi = acc_hi + w * hi
    x = x_ref[...] + gt2_ref[...] * jnp.concatenate([acc_lo, acc_hi], axis=1)
    if final:
        ms = jnp.mean(x * x, axis=-1, keepdims=True)
        x = x * lax.rsqrt(ms + EPS) * gf_ref[...]
    o_ref[...] = x


def _combine(dest_t, wts, x_mid, gt2, g_final, ys, seq, final, tm=256):
    t, d = x_mid.shape
    per_seq = seq // tm
    return pl.pallas_call(
        functools.partial(_combine_kernel, final=final),
        grid=(t // tm,),
        in_specs=[pl.BlockSpec((TOP_K, tm), lambda i: (0, i), memory_space=pltpu.SMEM),
                  pl.BlockSpec((tm, TOP_K), lambda i: (i, 0)),
                  pl.BlockSpec((tm, d), lambda i: (i, 0)),
                  pl.BlockSpec((None, 1, d), lambda i: (i // per_seq, 0, 0)),
                  pl.BlockSpec((1, d), lambda i: (0, 0)),
                  pl.BlockSpec(memory_space=pl.ANY)],
        out_specs=pl.BlockSpec((tm, d), lambda i: (i, 0)),
        out_shape=jax.ShapeDtypeStruct((t, d), F32),
        scratch_shapes=[pltpu.VMEM((TOP_K, tm, PACK_W), U32),
                        pltpu.SemaphoreType.DMA(())],
        compiler_params=_cparams(("arbitrary",)),
        name="moe_combine",
    )(dest_t, wts, x_mid, gt2, g_final, ys)


def _pad_to(a, axis, size):
    pad = [(0, 0)] * a.ndim
    pad[axis] = (0, size - a.shape[axis])
    return jnp.pad(a, pad)


def _in_weight(w_in):
    off_b = ZA_W
    off_cq = off_b + ZB_W
    off_ckv = off_cq + MLA_Q_RANK
    off_kr = off_ckv + MLA_KV_RANK
    off_d = off_kr + MLA_ROPE
    parts = [w_in[:, :off_cq],
             _pad_to(w_in[:, off_cq:off_ckv], 1, CQ_PAD),
             _pad_to(w_in[:, off_ckv:off_kr], 1, CKV_PAD),
             _pad_to(w_in[:, off_kr:off_d], 1, KR_PAD),
             w_in[:, off_d:]]
    return jnp.concatenate(parts, axis=1).astype(BF16)


def _mla_weights(w_uq, w_ukv):
    wq = w_uq.reshape(MLA_Q_RANK, N_HEADS, MLA_QK)
    wq = _pad_to(_pad_to(wq, 2, ATT_W), 0, CQ_PAD).reshape(CQ_PAD, N_HEADS * ATT_W)
    wkv = _pad_to(w_ukv.reshape(MLA_KV_RANK, N_HEADS, MLA_NOPE + HEAD_DIM), 0, CKV_PAD)
    wk = _pad_to(wkv[:, :, :MLA_NOPE], 2, ATT_W).reshape(CKV_PAD, N_HEADS * ATT_W)
    wv = wkv[:, :, MLA_NOPE:].reshape(CKV_PAD, MIX_W)
    return wq.astype(BF16), wk.astype(BF16), wv.astype(BF16)


def _lookup(table, idx):
    ids = jnp.arange(table.shape[0], dtype=I32).reshape((-1,) + (1,) * idx.ndim)
    vals = table.reshape(ids.shape)
    return jnp.sum(jnp.where(idx[None] == ids, vals, 0), axis=0)


def _route_tables(idx_t, rank_t, counts, p_rows):
    m = MOE_BLOCK
    padded = (counts + m - 1) // m * m
    pend = jnp.cumsum(padded)
    pstart = pend - padded
    dest_t = _lookup(pstart, idx_t) + rank_t
    blk_start = jnp.arange(p_rows // m, dtype=I32) * m
    block_e = jnp.minimum(jnp.sum(pend[None, :] <= blk_start[:, None], axis=1), N_EXPERTS - 1)
    block_e = block_e.astype(I32)
    block_valid = jnp.clip(_lookup(pstart + counts, block_e) - blk_start, 0, m).astype(I32)
    return dest_t.astype(I32), block_e, block_valid


def kernel(x, c, positions, w_mod, b_mod, g_attn, w_in, g_gmlp_v, w_spatial, b_spatial, g_mla_q, w_mla_uq, g_mla_kv, w_mla_ukv, w_conv_dw, b_conv_dw, g_conv_ln, b_conv_ln, w_out, g_ffn, w_router, b_router, w_exp_gate, w_exp_up, w_exp_down, w_sh_gate, w_sh_up, w_sh_down, g_final):
    batch, seq, d = x.shape
    n_layers = w_mod.shape[0]
    t = batch * seq
    p_rows = t * TOP_K + N_EXPERTS * MOE_BLOCK

    xf = x.reshape(t, d)
    pos_col = positions.reshape(t, 1)
    pos_chunks = positions.reshape(batch, seq // ATT_TK, ATT_TK)
    slopes = 2.0 ** (-8.0 * jnp.arange(1, N_HEADS + 1, dtype=F32) / N_HEADS)
    half = MLA_ROPE // 2
    inv_freq = ROPE_THETA ** (-jnp.arange(half, dtype=F32) * (2.0 / MLA_ROPE))
    invf = _pad_to(jnp.concatenate([inv_freq, inv_freq]), 0, KR_PAD).reshape(1, KR_PAD)

    mod = _modulation(c, w_mod, b_mod).reshape(n_layers, N_MOD, batch, 1, d)

    routed = None
    for l in range(n_layers):
        sh1, sc1, gt1, sh2, sc2, gt2 = (mod[l, j] for j in range(N_MOD))
        if routed is not None:
            xf = _combine(*routed, seq=seq, final=False)

        qm, ka, va, zb, zc, zd = _in_projection(xf, g_attn[l].reshape(1, d), sc1, sh1,
                                                _in_weight(w_in[l]), seq)
        y_a = _moba_attention(qm, ka, va, pos_col, pos_chunks, slopes, batch, seq)
        y_b = _spatial_gating(zb, g_gmlp_v[l], w_spatial[l], b_spatial[l])
        wq, wk, wv = _mla_weights(w_mla_uq[l], w_mla_ukv[l])
        q, k, v = _mla_prep(zc, pos_col, invf,
                            _pad_to(g_mla_q[l], 0, CQ_PAD).reshape(1, CQ_PAD),
                            _pad_to(g_mla_kv[l], 0, CKV_PAD).reshape(1, CKV_PAD), wq, wk, wv)
        y_c = _mla_attention(q, k, v, batch, seq)
        y_d = _conformer_conv(zd, w_conv_dw[l].reshape(CONV_WIDTH, MIX_W),
                              b_conv_dw[l].reshape(1, MIX_W), g_conv_ln[l].reshape(1, MIX_W),
                              b_conv_ln[l].reshape(1, MIX_W), seq)

        x_mid, hp, idx_t, wts_t, rank_t, cnt = _out_projection(
            xf, (y_a, y_b, y_c, y_d), w_out[l].astype(BF16), gt1, g_ffn[l].reshape(1, d),
            sc2, sh2, gt2, w_router[l].T, b_router[l].reshape(N_EXPERTS, 1),
            w_sh_gate[l].astype(BF16), w_sh_up[l].astype(BF16), w_sh_down[l].astype(BF16), seq)

        dest_t, block_e, block_valid = _route_tables(idx_t, rank_t, cnt[:, 0], p_rows)
        xs = _dispatch(dest_t, hp, p_rows)
        ys = _experts(block_e, block_valid, xs, w_exp_gate, w_exp_up, w_exp_down, l)
        routed = (dest_t, wts_t.T, x_mid, gt2, g_final.reshape(1, d), ys)

    out = _combine(*routed, seq=seq, final=True)
    return out.reshape(batch, seq, d)
```

```python
import functools

import jax
import jax.numpy as jnp
from jax import lax
from jax.experimental import pallas as pl
from jax.experimental.pallas import tpu as pltpu

F32 = jnp.float32
BF16 = jnp.bfloat16
I32 = jnp.int32
U32 = jnp.uint32

D_MODEL = 2048
HEAD_DIM = 128
N_HEADS = 4
MIX_W = N_HEADS * HEAD_DIM
MOBA_BLOCK = 256
MOBA_TOPK = 3
GMLP_CHUNK = 128
MLA_Q_RANK = 448
MLA_KV_RANK = 160
MLA_NOPE = 128
MLA_ROPE = 64
MLA_QK = MLA_NOPE + MLA_ROPE
ROPE_THETA = 10000.0
CONV_WIDTH = 31
CONV_HALO = 32
SUBLANES = 8
N_EXPERTS = 64
TOP_K = 8
D_EXPERT = 512
D_SHARED = 512
ROUTED_SCALE = 2.5
MOE_BLOCK = 512
N_MOD = 6
EPS = 1e-6
NEG = -1e30
LOG2E = 1.4426950408889634

ATT_TQ = 2048
ATT_SUB = 256
ATT_TK = 1024
ATT_W = 2 * HEAD_DIM

ZA_W = 3 * MIX_W
ZB_W = 2 * MIX_W
CQ_PAD = 512
CKV_PAD = 256
KR_PAD = 128
ZC_W = CQ_PAD + CKV_PAD + KR_PAD
ZD_W = 2 * MIX_W

VMEM_LIMIT = 56 * 1024 * 1024
PACK_W = D_MODEL // 2


def _cparams(sem):
    return pltpu.CompilerParams(dimension_semantics=sem, vmem_limit_bytes=VMEM_LIMIT)


def _resident(shape, index_map):
    return pl.BlockSpec(shape, index_map, pipeline_mode=pl.Buffered(1))


def _dot(a, b):
    return jnp.dot(a, b, preferred_element_type=F32)


def _dot_nt(a, b, precision=None):
    return lax.dot_general(a, b, (((1,), (1,)), ((), ())), preferred_element_type=F32,
                           precision=precision)


def _pack_rows(y):
    half = y.shape[1] // 2
    lo = lax.bitcast_convert_type(y[:, :half].astype(BF16).astype(F32), U32)
    hi = lax.bitcast_convert_type(y[:, half:].astype(BF16).astype(F32), U32)
    return (lo >> 16) | hi


def _unpack_rows(u):
    lo = lax.bitcast_convert_type(u << 16, F32)
    hi = lax.bitcast_convert_type(u & jnp.uint32(0xFFFF0000), F32)
    return lo, hi


def _mod_kernel(c_ref, w_ref, b_ref, o_ref):
    c = c_ref[...]
    a = c * jax.nn.sigmoid(c)
    o_ref[...] = jnp.dot(a, w_ref[...], preferred_element_type=F32,
                         precision=lax.Precision.HIGHEST) + b_ref[...]


def _modulation(c, w_mod, b_mod):
    n_layers, d, _ = w_mod.shape
    b = c.shape[0]
    return pl.pallas_call(
        _mod_kernel,
        grid=(n_layers, N_MOD),
        in_specs=[pl.BlockSpec((b, d), lambda l, j: (0, 0)),
                  pl.BlockSpec((None, d, d), lambda l, j: (l, 0, j)),
                  pl.BlockSpec((None, None, 1, d), lambda l, j: (l, j, 0, 0))],
        out_specs=pl.BlockSpec((None, None, b, d), lambda l, j: (l, j, 0, 0)),
        out_shape=jax.ShapeDtypeStruct((n_layers, N_MOD, b, d), F32),
        compiler_params=_cparams(("arbitrary", "arbitrary")),
        name="modulation",
    )(c, w_mod, b_mod.reshape(n_layers, N_MOD, 1, d))


def _inproj_kernel(x_ref, g_ref, sc_ref, sh_ref, w_ref, q_ref, ka_ref, va_ref, zb_ref, zc_ref, zd_ref,
                   *, seq):
    tm = x_ref.shape[0]
    x = x_ref[...]
    ms = jnp.mean(x * x, axis=-1, keepdims=True)
    h = (x * lax.rsqrt(ms + EPS) * g_ref[...]) * (1.0 + sc_ref[...]) + sh_ref[...]
    hb = h.astype(BF16)

    q_ref[...] = _dot(hb, w_ref[:, 0:MIX_W]).astype(BF16)
    pos = (pl.program_id(0) * tm) % seq + lax.broadcasted_iota(I32, (tm, HEAD_DIM), 0)
    lane = lax.broadcasted_iota(I32, (tm, HEAD_DIM), 1)
    onehot = jnp.where(pos // MOBA_BLOCK == lane, 1.0, 0.0).astype(BF16)
    ones = jnp.ones((tm, HEAD_DIM), BF16)
    k = _dot(hb, w_ref[:, MIX_W:2 * MIX_W]).astype(BF16)
    v = _dot(hb, w_ref[:, 2 * MIX_W:ZA_W]).astype(BF16)
    for hd in range(N_HEADS):
        src = slice(hd * HEAD_DIM, (hd + 1) * HEAD_DIM)
        ka_ref[:, hd * ATT_W:hd * ATT_W + HEAD_DIM] = k[:, src]
        ka_ref[:, hd * ATT_W + HEAD_DIM:(hd + 1) * ATT_W] = onehot
        va_ref[:, hd * ATT_W:hd * ATT_W + HEAD_DIM] = v[:, src]
        va_ref[:, hd * ATT_W + HEAD_DIM:(hd + 1) * ATT_W] = ones

    off = ZA_W
    for ref in (zb_ref, zc_ref, zd_ref):
        w = ref.shape[1]
        ref[...] = _dot(hb, w_ref[:, off:off + w]).astype(BF16)
        off += w


def _in_projection(x, g, sc, sh, w_in_p, seq, tm=512):
    t, d = x.shape
    per_seq = seq // tm
    widths = (MIX_W, N_HEADS * ATT_W, N_HEADS * ATT_W, ZB_W, ZC_W, ZD_W)
    mod_spec = pl.BlockSpec((None, 1, d), lambda i: (i // per_seq, 0, 0))
    return pl.pallas_call(
        functools.partial(_inproj_kernel, seq=seq),
        grid=(t // tm,),
        in_specs=[pl.BlockSpec((tm, d), lambda i: (i, 0)),
                  pl.BlockSpec((1, d), lambda i: (0, 0)),
                  mod_spec, mod_spec,
                  _resident((d, ZA_W + ZB_W + ZC_W + ZD_W), lambda i: (0, 0))],
        out_specs=[pl.BlockSpec((tm, w), lambda i: (i, 0)) for w in widths],
        out_shape=[jax.ShapeDtypeStruct((t, w), BF16) for w in widths],
        compiler_params=_cparams(("arbitrary",)),
        name="in_projection",
    )(x, g, sc, sh, w_in_p)


def _attn_kernel(*refs, moba):
    if moba:
        slope_ref, q_ref, k_ref, v_ref, pq_ref, pk_ref, o_ref, kmean_ref, qa_ref, m_ref, acc_ref = refs
    else:
        q_ref, k_ref, v_ref, o_ref, m_ref, acc_ref = refs
        qa_ref = q_ref
    tq, sub, tk = ATT_TQ, ATT_SUB, ATT_TK
    nsub = tq // sub
    i = pl.program_id(2)

    if moba:
        nb_pad = kmean_ref.shape[0]
        nb = k_ref.shape[0] // MOBA_BLOCK

        @pl.when(i == 0)
        def _():
            kmean_ref[...] = jnp.zeros_like(kmean_ref)

            def body(n, carry):
                start = pl.multiple_of(n * MOBA_BLOCK, MOBA_BLOCK)
                kb = k_ref[pl.ds(start, MOBA_BLOCK), 0:HEAD_DIM].astype(F32)
                kmean_ref[pl.ds(n, 1), :] = jnp.mean(kb, axis=0, keepdims=True)
                return carry
            lax.fori_loop(0, nb, body, 0)

        slope = slope_ref[pl.program_id(1)] * LOG2E
        col = lax.broadcasted_iota(I32, (sub, nb_pad), 1)
        colf = col.astype(F32)
        pq_s = []
        for c in range(nsub):
            rows = slice(c * sub, (c + 1) * sub)
            own = i * nsub + c
            qf = q_ref[rows, :].astype(F32)
            gate = _dot_nt(qf, kmean_ref[...], precision=lax.Precision.HIGHEST)
            gate = jnp.where(col < own, gate, NEG)
            sel = col == own
            for _ in range(MOBA_TOPK):
                gmax = jnp.max(gate, axis=-1, keepdims=True)
                first = jnp.min(jnp.where(gate == gmax, colf, float(nb_pad)), axis=-1, keepdims=True)
                hit = colf == first
                sel = sel | (hit & (col < own))
                gate = jnp.where(hit, -jnp.inf, gate)
            qa_ref[rows, 0:HEAD_DIM] = (qf * (HEAD_DIM ** -0.5 * LOG2E)).astype(BF16)
            qa_ref[rows, HEAD_DIM:] = jnp.where(sel, 0.0, NEG).astype(BF16)
            pq_s.append(slope * pq_ref[rows, :].astype(F32))

    m_ref[...] = jnp.full(m_ref.shape, NEG, F32)
    acc_ref[...] = jnp.zeros_like(acc_ref)

    def step(c, chunk, mask_off):
        rows = slice(c * sub, (c + 1) * sub)
        start = pl.multiple_of(chunk * tk, tk)
        s = _dot_nt(qa_ref[rows, :], k_ref[pl.ds(start, tk), :])
        if moba:
            s = s - jnp.abs(pq_s[c] - slope * pk_ref[pl.ds(chunk, 1), :].astype(F32))
        if mask_off is not None:
            qi = lax.broadcasted_iota(I32, (sub, tk), 0)
            ki = lax.broadcasted_iota(I32, (sub, tk), 1)
            s = jnp.where(ki <= qi + mask_off, s, NEG)
        m_old = m_ref[rows, :]
        m_new = jnp.maximum(m_old, jnp.max(s, axis=-1, keepdims=True))
        p = jnp.exp2(s - m_new).astype(BF16)
        acc_ref[rows, :] = jnp.exp2(m_old - m_new) * acc_ref[rows, :] + _dot(p, v_ref[pl.ds(start, tk), :])
        m_ref[rows, :] = m_new

    per_tile = tq // tk
    for c in range(nsub):
        q_lo = c * sub
        for jj in reversed(range(per_tile)):
            k_lo = jj * tk
            if k_lo > q_lo + sub - 1:
                continue
            needs_mask = k_lo + tk - 1 > q_lo
            step(c, i * per_tile + jj, q_lo - k_lo if needs_mask else None)

    def body(j, carry):
        for c in range(nsub):
            step(c, j, None)
        return carry
    lax.fori_loop(0, i * per_tile, body, 0)

    acc = acc_ref[...]
    o_ref[...] = (acc[:, :HEAD_DIM] / acc[:, HEAD_DIM:]).astype(o_ref.dtype)


def _attn_scratch():
    return [pltpu.VMEM((ATT_TQ, 1), F32), pltpu.VMEM((ATT_TQ, ATT_W), F32)]


def _moba_attention(q, ka, va, pos_col, pos_chunks, slopes, batch, seq):
    t = q.shape[0]
    nq = seq // ATT_TQ
    kv_spec = pl.BlockSpec((seq, ATT_W), lambda b, h, i, s: (b, h))
    return pl.pallas_call(
        functools.partial(_attn_kernel, moba=True),
        grid_spec=pltpu.PrefetchScalarGridSpec(
            num_scalar_prefetch=1,
            grid=(batch, N_HEADS, nq),
            in_specs=[pl.BlockSpec((ATT_TQ, HEAD_DIM), lambda b, h, i, s: (b * nq + i, h)),
                      kv_spec, kv_spec,
                      pl.BlockSpec((ATT_TQ, 1), lambda b, h, i, s: (b * nq + i, 0)),
                      pl.BlockSpec((None, seq // ATT_TK, ATT_TK), lambda b, h, i, s: (b, 0, 0))],
            out_specs=pl.BlockSpec((ATT_TQ, HEAD_DIM), lambda b, h, i, s: (b * nq + i, h)),
            scratch_shapes=[pltpu.VMEM((HEAD_DIM, HEAD_DIM), F32),
                            pltpu.VMEM((ATT_TQ, ATT_W), BF16)] + _attn_scratch()),
        out_shape=jax.ShapeDtypeStruct((t, MIX_W), BF16),
        compiler_params=_cparams(("arbitrary", "arbitrary", "arbitrary")),
        name="moba_attention",
    )(slopes, q, ka, va, pos_col, pos_chunks)


def _mla_attention(q, k, va, batch, seq):
    t = q.shape[0]
    nq = seq // ATT_TQ
    kv_spec = pl.BlockSpec((seq, ATT_W), lambda b, h, i: (b, h))
    return pl.pallas_call(
        functools.partial(_attn_kernel, moba=False),
        grid=(batch, N_HEADS, nq),
        in_specs=[pl.BlockSpec((ATT_TQ, ATT_W), lambda b, h, i: (b * nq + i, h)),
                  kv_spec, kv_spec],
        out_specs=pl.BlockSpec((ATT_TQ, HEAD_DIM), lambda b, h, i: (b * nq + i, h)),
        out_shape=jax.ShapeDtypeStruct((t, MIX_W), BF16),
        scratch_shapes=_attn_scratch(),
        compiler_params=_cparams(("arbitrary", "arbitrary", "arbitrary")),
        name="mla_attention",
    )(q, k, va)


def _gelu_tanh(x):
    return 0.5 * x * (1.0 + jnp.tanh(0.7978845608028654 * (x + 0.044715 * x * x * x)))


def _gmlp_kernel(z_ref, gv_ref, ws_ref, bs_ref, o_ref):
    tm = z_ref.shape[0]
    ck = GMLP_CHUNK
    z = _gelu_tanh(z_ref[...].astype(F32))
    row = lax.broadcasted_iota(I32, (ck, ck), 0)
    colm = lax.broadcasted_iota(I32, (ck, ck), 1)
    for g in range(N_HEADS):
        lanes = slice(g * HEAD_DIM, (g + 1) * HEAD_DIM)
        u = z[:, lanes]
        vv = z[:, MIX_W + g * HEAD_DIM:MIX_W + (g + 1) * HEAD_DIM]
        ms = jnp.mean(vv * vv, axis=-1, keepdims=True)
        vn = (vv * lax.rsqrt(ms + EPS) * gv_ref[g:g + 1, :]).astype(BF16)
        w = jnp.where(colm <= row, ws_ref[g], 0.0).astype(BF16)
        bias = bs_ref[g]
        for c in range(tm // ck):
            rows = slice(c * ck, (c + 1) * ck)
            mixed = _dot(w, vn[rows]) + bias
            o_ref[rows, lanes] = (u[rows] * mixed).astype(o_ref.dtype)


def _spatial_gating(zb, g_v, w_s, b_s, tm=512):
    t = zb.shape[0]
    ck = GMLP_CHUNK
    return pl.pallas_call(
        _gmlp_kernel,
        grid=(t // tm,),
        in_specs=[pl.BlockSpec((tm, ZB_W), lambda i: (i, 0)),
                  pl.BlockSpec((N_HEADS, HEAD_DIM), lambda i: (0, 0)),
                  pl.BlockSpec((N_HEADS, ck, ck), lambda i: (0, 0, 0)),
                  pl.BlockSpec((N_HEADS, ck, 1), lambda i: (0, 0, 0))],
        out_specs=pl.BlockSpec((tm, MIX_W), lambda i: (i, 0)),
        out_shape=jax.ShapeDtypeStruct((t, MIX_W), BF16),
        compiler_params=_cparams(("arbitrary",)),
        name="spatial_gating",
    )(zb, g_v, w_s, b_s.reshape(N_HEADS, ck, 1))


def _mla_prep_kernel(z_ref, pos_ref, invf_ref, gq_ref, gkv_ref, wq_ref, wk_ref, wv_ref,
                     q_ref, k_ref, v_ref):
    tm = z_ref.shape[0]
    z = z_ref[...].astype(F32)
    cq = z[:, :CQ_PAD]
    ckv = z[:, CQ_PAD:CQ_PAD + CKV_PAD]
    kr = z[:, CQ_PAD + CKV_PAD:]
    qn = cq * lax.rsqrt(jnp.sum(cq * cq, -1, keepdims=True) * (1.0 / MLA_Q_RANK) + EPS)
    qn = (qn * gq_ref[...]).astype(BF16)
    kvn = ckv * lax.rsqrt(jnp.sum(ckv * ckv, -1, keepdims=True) * (1.0 / MLA_KV_RANK) + EPS)
    kvn = (kvn * gkv_ref[...]).astype(BF16)

    ang = pos_ref[...].astype(F32) * invf_ref[...]
    lane = lax.broadcasted_iota(I32, ang.shape, 1)
    half = MLA_ROPE // 2
    cos = jnp.cos(ang)
    sin = jnp.sin(ang)
    sin_lo = jnp.where(lane < half, -sin, 0.0)
    sin_hi = jnp.where((lane >= half) & (lane < 2 * half), sin, 0.0)

    def rope(r):
        return (r * cos + pltpu.roll(r, KR_PAD - half, 1) * sin_lo
                + pltpu.roll(r, half, 1) * sin_hi)

    q = _dot(qn, wq_ref[...])
    kn = _dot(kvn, wk_ref[...])
    v = _dot(kvn, wv_ref[...]).astype(BF16)
    k_rope = rope(kr)
    scale = MLA_QK ** -0.5 * LOG2E
    ones = jnp.ones((tm, HEAD_DIM), BF16)
    for h in range(N_HEADS):
        a = h * ATT_W
        b = a + MLA_NOPE
        q_ref[:, a:b] = (q[:, a:b] * scale).astype(q_ref.dtype)
        q_ref[:, b:a + ATT_W] = (rope(q[:, b:a + ATT_W]) * scale).astype(q_ref.dtype)
        k_ref[:, a:b] = kn[:, a:b].astype(k_ref.dtype)
        k_ref[:, b:a + ATT_W] = k_rope.astype(k_ref.dtype)
        v_ref[:, a:b] = v[:, h * HEAD_DIM:(h + 1) * HEAD_DIM]
        v_ref[:, b:a + ATT_W] = ones


def _mla_prep(zc, pos_col, invf, gq, gkv, wq, wk, wv, tm=512):
    t = zc.shape[0]
    qk_w = N_HEADS * ATT_W
    full = lambda i: (0, 0)
    out_spec = pl.BlockSpec((tm, qk_w), lambda i: (i, 0))
    out_shape = jax.ShapeDtypeStruct((t, qk_w), BF16)
    return pl.pallas_call(
        _mla_prep_kernel,
        grid=(t // tm,),
        in_specs=[pl.BlockSpec((tm, ZC_W), lambda i: (i, 0)),
                  pl.BlockSpec((tm, 1), lambda i: (i, 0)),
                  pl.BlockSpec((1, KR_PAD), full),
                  pl.BlockSpec((1, CQ_PAD), full),
                  pl.BlockSpec((1, CKV_PAD), full),
                  pl.BlockSpec((CQ_PAD, qk_w), full),
                  pl.BlockSpec((CKV_PAD, qk_w), full),
                  pl.BlockSpec((CKV_PAD, MIX_W), full)],
        out_specs=[out_spec, out_spec, out_spec],
        out_shape=[out_shape, out_shape, out_shape],
        compiler_params=_cparams(("arbitrary",)),
        name="mla_prep",
    )(zc, pos_col, invf, gq, gkv, wq, wk, wv)


def _glu(z):
    z = z.astype(F32)
    return z[:, :MIX_W] * jax.nn.sigmoid(z[:, MIX_W:])


def _conv_kernel(z_ref, zprev_ref, w_ref, b_ref, g_ref, beta_ref, o_ref, ybuf_ref, ysh_ref, *, per_seq):
    tm = z_ref.shape[0]
    first = (pl.program_id(0) % per_seq) == 0
    ybuf_ref[0:CONV_HALO, :] = jnp.where(first, 0.0, _glu(zprev_ref[...]))
    ybuf_ref[CONV_HALO:, :] = _glu(z_ref[...])
    span = tm + CONV_HALO - SUBLANES
    for o in range(1, SUBLANES):
        ysh_ref[o - 1, :, :] = ybuf_ref[pl.ds(o, span), :]
    rows = 64
    shift = CONV_HALO - (CONV_WIDTH - 1)
    for r in range(tm // rows):
        acc = jnp.zeros((rows, MIX_W), F32) + b_ref[...]
        for j in range(CONV_WIDTH):
            o = (shift + j) % SUBLANES
            base = r * rows + shift + j - o
            tap = ybuf_ref[pl.ds(base, rows), :] if o == 0 else ysh_ref[o - 1, pl.ds(base, rows), :]
            acc = acc + w_ref[j:j + 1, :] * tap
        mu = jnp.mean(acc, axis=-1, keepdims=True)
        xc = acc - mu
        y = xc * lax.rsqrt(jnp.mean(xc * xc, axis=-1, keepdims=True) + EPS)
        y = y * g_ref[...] + beta_ref[...]
        o_ref[r * rows:(r + 1) * rows, :] = (y * jax.nn.sigmoid(y)).astype(o_ref.dtype)


def _conformer_conv(zd, w_dw, b_dw, g_ln, b_ln, seq, tm=512):
    t = zd.shape[0]
    per_seq = seq // tm
    halo_blocks = tm // CONV_HALO
    full = lambda i: (0, 0)
    return pl.pallas_call(
        functools.partial(_conv_kernel, per_seq=per_seq),
        grid=(t // tm,),
        in_specs=[pl.BlockSpec((tm, ZD_W), lambda i: (i, 0)),
                  pl.BlockSpec((CONV_HALO, ZD_W),
                               lambda i: (jnp.maximum(i * halo_blocks - 1, 0), 0)),
                  pl.BlockSpec((CONV_WIDTH, MIX_W), full),
                  pl.BlockSpec((1, MIX_W), full),
                  pl.BlockSpec((1, MIX_W), full),
                  pl.BlockSpec((1, MIX_W), full)],
        out_specs=pl.BlockSpec((tm, MIX_W), lambda i: (i, 0)),
        out_shape=jax.ShapeDtypeStruct((t, MIX_W), BF16),
        scratch_shapes=[pltpu.VMEM((tm + CONV_HALO, MIX_W), F32),
                        pltpu.VMEM((SUBLANES - 1, tm + CONV_HALO - SUBLANES, MIX_W), F32)],
        compiler_params=_cparams(("arbitrary",)),
        name="conformer_conv",
    )(zd, zd, w_dw, b_dw, g_ln, b_ln)


def _outproj_kernel(x_ref, ya_ref, yb_ref, yc_ref, yd_ref, wo_ref, gt1_ref, g_ref, sc_ref, sh_ref,
                    gt2_ref, wr_ref, br_ref, wsg_ref, wsu_ref, wsd_ref,
                    xo_ref, hp_ref, idx_ref, wts_ref, rank_ref, cnt_ref, carry_ref):
    tm = x_ref.shape[0]

    @pl.when(pl.program_id(0) == 0)
    def _():
        carry_ref[...] = jnp.zeros_like(carry_ref)

    y = _dot(ya_ref[...], wo_ref[0:MIX_W, :])
    y = y + _dot(yb_ref[...], wo_ref[MIX_W:2 * MIX_W, :])
    y = y + _dot(yc_ref[...], wo_ref[2 * MIX_W:3 * MIX_W, :])
    y = y + _dot(yd_ref[...], wo_ref[3 * MIX_W:, :])
    x = x_ref[...] + gt1_ref[...] * y
    ms = jnp.mean(x * x, axis=-1, keepdims=True)
    h = (x * lax.rsqrt(ms + EPS) * g_ref[...]) * (1.0 + sc_ref[...]) + sh_ref[...]
    hb = h.astype(BF16)
    hp_ref[...] = _pack_rows(h)

    act = _dot(hb, wsg_ref[...])
    act = (act * jax.nn.sigmoid(act) * _dot(hb, wsu_ref[...])).astype(BF16)
    xo_ref[...] = x + gt2_ref[...] * _dot(act, wsd_ref[...])

    logits = _dot_nt(wr_ref[...], h, precision=lax.Precision.HIGHEST)
    scores = jax.nn.sigmoid(logits)
    cur = scores + br_ref[...]
    erow = lax.broadcasted_iota(I32, (N_EXPERTS, tm), 0).astype(F32)
    picked = jnp.zeros((N_EXPERTS, tm), F32)
    hits, idxs, wts = [], [], []
    for _ in range(TOP_K):
        cmax = jnp.max(cur, axis=0, keepdims=True)
        first = jnp.min(jnp.where(cur == cmax, erow, float(N_EXPERTS)), axis=0, keepdims=True)
        hit = erow == first
        hits.append(hit)
        idxs.append(first)
        wts.append(jnp.sum(jnp.where(hit, scores, 0.0), axis=0, keepdims=True))
        picked = jnp.where(hit, 1.0, picked)
        cur = jnp.where(hit, -jnp.inf, cur)
    wsum = wts[0]
    for w in wts[1:]:
        wsum = wsum + w
    ti = lax.broadcasted_iota(I32, (tm, tm), 0)
    tj = lax.broadcasted_iota(I32, (tm, tm), 1)
    before = jnp.where(ti < tj, 1.0, 0.0).astype(BF16)
    prior = _dot(picked.astype(BF16), before) + carry_ref[...]
    for k in range(TOP_K):
        idx_ref[k:k + 1, :] = idxs[k].astype(I32)
        wts_ref[k:k + 1, :] = wts[k] / wsum * ROUTED_SCALE
        rank_ref[k:k + 1, :] = jnp.sum(jnp.where(hits[k], prior, 0.0), axis=0,
                                       keepdims=True).astype(I32)
    carry_ref[...] = carry_ref[...] + jnp.sum(picked, axis=1, keepdims=True)
    cnt_ref[...] = jnp.broadcast_to(carry_ref[...], cnt_ref.shape).astype(I32)


def _out_projection(x, ys, w_out, gt1, g_ffn, sc2, sh2, gt2, w_rt, b_r, wsg, wsu, wsd, seq, tm=512):
    t, d = x.shape
    per_seq = seq // tm
    full = lambda i: (0, 0)
    mod_spec = pl.BlockSpec((None, 1, d), lambda i: (i // per_seq, 0, 0))
    y_spec = pl.BlockSpec((tm, MIX_W), lambda i: (i, 0))
    tok_spec = pl.BlockSpec((TOP_K, tm), lambda i: (0, i))
    return pl.pallas_call(
        _outproj_kernel,
        grid=(t // tm,),
        in_specs=[pl.BlockSpec((tm, d), lambda i: (i, 0)),
                  y_spec, y_spec, y_spec, y_spec,
                  _resident((4 * MIX_W, d), full),
                  mod_spec,
                  pl.BlockSpec((1, d), full),
                  mod_spec, mod_spec, mod_spec,
                  pl.BlockSpec((N_EXPERTS, d), full),
                  pl.BlockSpec((N_EXPERTS, 1), full),
                  _resident((d, D_SHARED), full),
                  _resident((d, D_SHARED), full),
                  _resident((D_SHARED, d), full)],
        out_specs=[pl.BlockSpec((tm, d), lambda i: (i, 0)),
                   pl.BlockSpec((tm, PACK_W), lambda i: (i, 0)),
                   tok_spec, tok_spec, tok_spec,
                   pl.BlockSpec((N_EXPERTS, 128), full)],
        out_shape=[jax.ShapeDtypeStruct((t, d), F32),
                   jax.ShapeDtypeStruct((t, PACK_W), U32),
                   jax.ShapeDtypeStruct((TOP_K, t), I32),
                   jax.ShapeDtypeStruct((TOP_K, t), F32),
                   jax.ShapeDtypeStruct((TOP_K, t), I32),
                   jax.ShapeDtypeStruct((N_EXPERTS, 128), I32)],
        scratch_shapes=[pltpu.VMEM((N_EXPERTS, 1), F32)],
        compiler_params=_cparams(("arbitrary",)),
        name="out_projection_router",
    )(x, *ys, w_out, gt1, g_ffn, sc2, sh2, gt2, w_rt, b_r, wsg, wsu, wsd)


def _row_copy(src_ref, src_row, dst_ref, dst_row, sem):
    return pltpu.make_async_copy(src_ref.at[pl.ds(src_row, 1)], dst_ref.at[pl.ds(dst_row, 1)], sem)


def _dispatch_kernel(dest_ref, hp_ref, xs_ref, sem):
    tm = hp_ref.shape[0]

    def start(t, carry):
        for k in range(TOP_K):
            _row_copy(hp_ref, t, xs_ref, dest_ref[t * TOP_K + k], sem).start(priority=k % 2)
        return carry
    lax.fori_loop(0, tm, start, 0)

    def wait(t, carry):
        for k in range(TOP_K):
            _row_copy(hp_ref, t, xs_ref, dest_ref[t * TOP_K + k], sem).wait()
        return carry
    lax.fori_loop(0, tm, wait, 0)


def _dispatch(dest_t, hp, p_rows, tm=512):
    t = hp.shape[0]
    return pl.pallas_call(
        _dispatch_kernel,
        grid=(t // tm,),
        in_specs=[pl.BlockSpec((tm * TOP_K,), lambda i: (i,), memory_space=pltpu.SMEM),
                  pl.BlockSpec((tm, PACK_W), lambda i: (i, 0))],
        out_specs=pl.BlockSpec(memory_space=pl.ANY),
        out_shape=jax.ShapeDtypeStruct((p_rows, PACK_W), U32),
        scratch_shapes=[pltpu.SemaphoreType.DMA(())],
        compiler_params=_cparams(("arbitrary",)),
        name="moe_dispatch",
    )(dest_t, hp)


def _expert_kernel(be_ref, bv_ref, x_ref, wg_ref, wu_ref, wd_ref, y_ref, wgb_ref, wub_ref, wdb_ref):
    i = pl.program_id(0)
    expert = be_ref[i]
    prev = be_ref[jnp.maximum(i - 1, 0)]

    @pl.when((i == 0) | (expert != prev))
    def _():
        wgb_ref[...] = wg_ref[...].astype(BF16)
        wub_ref[...] = wu_ref[...].astype(BF16)
        wdb_ref[...] = wd_ref[...].astype(BF16)

    valid = bv_ref[i]

    @pl.when(valid > 0)
    def _():
        lo, hi = _unpack_rows(x_ref[...])
        rows = lax.broadcasted_iota(I32, (x_ref.shape[0], 1), 0)
        xb = jnp.where(rows < valid, jnp.concatenate([lo, hi], axis=1), 0.0).astype(BF16)
        gate = _dot(xb, wgb_ref[...])
        act = (gate * jax.nn.sigmoid(gate) * _dot(xb, wub_ref[...])).astype(BF16)
        y_ref[...] = _pack_rows(_dot(act, wdb_ref[...]))

    @pl.when(valid == 0)
    def _():
        y_ref[...] = jnp.zeros_like(y_ref)


def _experts(block_e, block_valid, xs, w_gate, w_up, w_down, layer):
    p_rows = xs.shape[0]
    d = D_MODEL
    return pl.pallas_call(
        _expert_kernel,
        grid_spec=pltpu.PrefetchScalarGridSpec(
            num_scalar_prefetch=2,
            grid=(p_rows // MOE_BLOCK,),
            in_specs=[pl.BlockSpec((MOE_BLOCK, PACK_W), lambda i, be, bv: (i, 0)),
                      pl.BlockSpec((None, None, d, D_EXPERT), lambda i, be, bv: (layer, be[i], 0, 0)),
                      pl.BlockSpec((None, None, d, D_EXPERT), lambda i, be, bv: (layer, be[i], 0, 0)),
                      pl.BlockSpec((None, None, D_EXPERT, d), lambda i, be, bv: (layer, be[i], 0, 0))],
            out_specs=pl.BlockSpec((MOE_BLOCK, PACK_W), lambda i, be, bv: (i, 0)),
            scratch_shapes=[pltpu.VMEM((d, D_EXPERT), BF16),
                            pltpu.VMEM((d, D_EXPERT), BF16),
                            pltpu.VMEM((D_EXPERT, d), BF16)]),
        out_shape=jax.ShapeDtypeStruct((p_rows, PACK_W), U32),
        compiler_params=_cparams(("arbitrary",)),
        name="moe_experts",
    )(block_e, block_valid, xs, w_gate, w_up, w_down)


def _combine_kernel(dest_ref, wts_ref, x_ref, gt2_ref, gf_ref, ys_ref, o_ref, buf_ref, sem, *, final):
    tm = x_ref.shape[0]

    def start(t, carry):
        for k in range(TOP_K):
            _row_copy(ys_ref, dest_ref[t * TOP_K + k], buf_ref.at[k], t, sem).start(priority=k % 2)
        return carry
    lax.fori_loop(0, tm, start, 0)

    def wait(t, carry):
        for k in range(TOP_K):
            _row_copy(ys_ref, dest_ref[t * TOP_K + k], buf_ref.at[k], t, sem).wait()
        return carry
    lax.fori_loop(0, tm, wait, 0)

    acc_lo = jnp.zeros((tm, PACK_W), F32)
    acc_hi = jnp.zeros((tm, PACK_W), F32)
    for k in range(TOP_K):
        lo, hi = _unpack_rows(buf_ref[k])
        w = wts_ref[:, k:k + 1]
        acc_lo = acc_lo + w * lo
        acc_hi = acc_hi + w * hi
    x = x_ref[...] + gt2_ref[...] * jnp.concatenate([acc_lo, acc_hi], axis=1)
    if final:
        ms = jnp.mean(x * x, axis=-1, keepdims=True)
        x = x * lax.rsqrt(ms + EPS) * gf_ref[...]
    o_ref[...] = x


def _combine(dest_t, wts, x_mid, gt2, g_final, ys, seq, final, tm=256):
    t, d = x_mid.shape
    per_seq = seq // tm
    return pl.pallas_call(
        functools.partial(_combine_kernel, final=final),
        grid=(t // tm,),
        in_specs=[pl.BlockSpec((tm * TOP_K,), lambda i: (i,), memory_space=pltpu.SMEM),
                  pl.BlockSpec((tm, TOP_K), lambda i: (i, 0)),
                  pl.BlockSpec((tm, d), lambda i: (i, 0)),
                  pl.BlockSpec((None, 1, d), lambda i: (i // per_seq, 0, 0)),
                  pl.BlockSpec((1, d), lambda i: (0, 0)),
                  pl.BlockSpec(memory_space=pl.ANY)],
        out_specs=pl.BlockSpec((tm, d), lambda i: (i, 0)),
        out_shape=jax.ShapeDtypeStruct((t, d), F32),
        scratch_shapes=[pltpu.VMEM((TOP_K, tm, PACK_W), U32),
                        pltpu.SemaphoreType.DMA(())],
        compiler_params=_cparams(("arbitrary",)),
        name="moe_combine",
    )(dest_t, wts, x_mid, gt2, g_final, ys)


def _pad_to(a, axis, size):
    pad = [(0, 0)] * a.ndim
    pad[axis] = (0, size - a.shape[axis])
    return jnp.pad(a, pad)


def _in_weight(w_in):
    off_b = ZA_W
    off_cq = off_b + ZB_W
    off_ckv = off_cq + MLA_Q_RANK
    off_kr = off_ckv + MLA_KV_RANK
    off_d = off_kr + MLA_ROPE
    parts = [w_in[:, :off_cq],
             _pad_to(w_in[:, off_cq:off_ckv], 1, CQ_PAD),
             _pad_to(w_in[:, off_ckv:off_kr], 1, CKV_PAD),
             _pad_to(w_in[:, off_kr:off_d], 1, KR_PAD),
             w_in[:, off_d:]]
    return jnp.concatenate(parts, axis=1).astype(BF16)


def _mla_weights(w_uq, w_ukv):
    wq = w_uq.reshape(MLA_Q_RANK, N_HEADS, MLA_QK)
    wq = _pad_to(_pad_to(wq, 2, ATT_W), 0, CQ_PAD).reshape(CQ_PAD, N_HEADS * ATT_W)
    wkv = _pad_to(w_ukv.reshape(MLA_KV_RANK, N_HEADS, MLA_NOPE + HEAD_DIM), 0, CKV_PAD)
    wk = _pad_to(wkv[:, :, :MLA_NOPE], 2, ATT_W).reshape(CKV_PAD, N_HEADS * ATT_W)
    wv = wkv[:, :, MLA_NOPE:].reshape(CKV_PAD, MIX_W)
    return wq.astype(BF16), wk.astype(BF16), wv.astype(BF16)


def _lookup(table, idx):
    ids = jnp.arange(table.shape[0], dtype=I32).reshape((-1,) + (1,) * idx.ndim)
    vals = table.reshape(ids.shape)
    return jnp.sum(jnp.where(idx[None] == ids, vals, 0), axis=0)


def _route_tables(idx_t, rank_t, counts, p_rows):
    m = MOE_BLOCK
    padded = (counts + m - 1) // m * m
    pend = jnp.cumsum(padded)
    pstart = pend - padded
    dest_t = _lookup(pstart, idx_t) + rank_t
    blk_start = jnp.arange(p_rows // m, dtype=I32) * m
    block_e = jnp.minimum(jnp.sum(pend[None, :] <= blk_start[:, None], axis=1), N_EXPERTS - 1)
    block_e = block_e.astype(I32)
    block_valid = jnp.clip(_lookup(pstart + counts, block_e) - blk_start, 0, m).astype(I32)
    dest = dest_t.astype(I32).T.reshape(-1)
    return dest, block_e, block_valid


def kernel(x, c, positions, w_mod, b_mod, g_attn, w_in, g_gmlp_v, w_spatial, b_spatial, g_mla_q, w_mla_uq, g_mla_kv, w_mla_ukv, w_conv_dw, b_conv_dw, g_conv_ln, b_conv_ln, w_out, g_ffn, w_router, b_router, w_exp_gate, w_exp_up, w_exp_down, w_sh_gate, w_sh_up, w_sh_down, g_final):
    batch, seq, d = x.shape
    n_layers = w_mod.shape[0]
    t = batch * seq
    p_rows = t * TOP_K + N_EXPERTS * MOE_BLOCK

    xf = x.reshape(t, d)
    pos_col = positions.reshape(t, 1)
    pos_chunks = positions.reshape(batch, seq // ATT_TK, ATT_TK)
    slopes = 2.0 ** (-8.0 * jnp.arange(1, N_HEADS + 1, dtype=F32) / N_HEADS)
    half = MLA_ROPE // 2
    inv_freq = ROPE_THETA ** (-jnp.arange(half, dtype=F32) * (2.0 / MLA_ROPE))
    invf = _pad_to(jnp.concatenate([inv_freq, inv_freq]), 0, KR_PAD).reshape(1, KR_PAD)

    mod = _modulation(c, w_mod, b_mod).reshape(n_layers, N_MOD, batch, 1, d)

    routed = None
    for l in range(n_layers):
        sh1, sc1, gt1, sh2, sc2, gt2 = (mod[l, j] for j in range(N_MOD))
        if routed is not None:
            xf = _combine(*routed, seq=seq, final=False)

        qm, ka, va, zb, zc, zd = _in_projection(xf, g_attn[l].reshape(1, d), sc1, sh1,
                                                _in_weight(w_in[l]), seq)
        y_a = _moba_attention(qm, ka, va, pos_col, pos_chunks, slopes, batch, seq)
        y_b = _spatial_gating(zb, g_gmlp_v[l], w_spatial[l], b_spatial[l])
        wq, wk, wv = _mla_weights(w_mla_uq[l], w_mla_ukv[l])
        q, k, v = _mla_prep(zc, pos_col, invf,
                            _pad_to(g_mla_q[l], 0, CQ_PAD).reshape(1, CQ_PAD),
                            _pad_to(g_mla_kv[l], 0, CKV_PAD).reshape(1, CKV_PAD), wq, wk, wv)
        y_c = _mla_attention(q, k, v, batch, seq)
        y_d = _conformer_conv(zd, w_conv_dw[l].reshape(CONV_WIDTH, MIX_W),
                              b_conv_dw[l].reshape(1, MIX_W), g_conv_ln[l].reshape(1, MIX_W),
                              b_conv_ln[l].reshape(1, MIX_W), seq)

        x_mid, hp, idx_t, wts_t, rank_t, cnt = _out_projection(
            xf, (y_a, y_b, y_c, y_d), w_out[l].astype(BF16), gt1, g_ffn[l].reshape(1, d),
            sc2, sh2, gt2, w_router[l].T, b_router[l].reshape(N_EXPERTS, 1),
            w_sh_gate[l].astype(BF16), w_sh_up[l].astype(BF16), w_sh_down[l].astype(BF16), seq)

        dest, block_e, block_valid = _route_tables(idx_t, rank_t, cnt[:, 0], p_rows)
        xs = _dispatch(dest, hp, p_rows)
        ys = _experts(block_e, block_valid, xs, w_exp_gate, w_exp_up, w_exp_down, l)
        routed = (dest, wts_t.T, x_mid, gt2, g_final.reshape(1, d), ys)

    out = _combine(*routed, seq=seq, final=True)
    return out.reshape(batch, seq, d)
```

```python
import functools

import jax
import jax.numpy as jnp
from jax import lax
from jax.experimental import pallas as pl
from jax.experimental.pallas import tpu as pltpu

F32 = jnp.float32
BF16 = jnp.bfloat16
I32 = jnp.int32
U32 = jnp.uint32

D_MODEL = 2048
HEAD_DIM = 128
N_HEADS = 4
MIX_W = N_HEADS * HEAD_DIM
MOBA_BLOCK = 256
MOBA_TOPK = 3
GMLP_CHUNK = 128
MLA_Q_RANK = 448
MLA_KV_RANK = 160
MLA_NOPE = 128
MLA_ROPE = 64
MLA_QK = MLA_NOPE + MLA_ROPE
ROPE_THETA = 10000.0
CONV_WIDTH = 31
CONV_HALO = 32
SUBLANES = 8
N_EXPERTS = 64
TOP_K = 8
D_EXPERT = 512
D_SHARED = 512
ROUTED_SCALE = 2.5
MOE_BLOCK = 512
N_MOD = 6
EPS = 1e-6
NEG = -1e30
LOG2E = 1.4426950408889634

ATT_TQ = 2048
ATT_SUB = 256
ATT_TK = 1024
ATT_W = 2 * HEAD_DIM

ZA_W = 3 * MIX_W
ZB_W = 2 * MIX_W
CQ_PAD = 512
CKV_PAD = 256
KR_PAD = 128
ZC_W = CQ_PAD + CKV_PAD + KR_PAD
ZD_W = 2 * MIX_W

VMEM_LIMIT = 56 * 1024 * 1024
PACK_W = D_MODEL // 2


def _cparams(sem):
    return pltpu.CompilerParams(dimension_semantics=sem, vmem_limit_bytes=VMEM_LIMIT)


def _resident(shape, index_map):
    return pl.BlockSpec(shape, index_map, pipeline_mode=pl.Buffered(1))


def _dot(a, b):
    return jnp.dot(a, b, preferred_element_type=F32)


def _dot_nt(a, b, precision=None):
    return lax.dot_general(a, b, (((1,), (1,)), ((), ())), preferred_element_type=F32,
                           precision=precision)


def _pack_rows(y):
    half = y.shape[1] // 2
    lo = lax.bitcast_convert_type(y[:, :half].astype(BF16).astype(F32), U32)
    hi = lax.bitcast_convert_type(y[:, half:].astype(BF16).astype(F32), U32)
    return (lo >> 16) | hi


def _unpack_rows(u):
    lo = lax.bitcast_convert_type(u << 16, F32)
    hi = lax.bitcast_convert_type(u & jnp.uint32(0xFFFF0000), F32)
    return lo, hi


def _mod_kernel(c_ref, w_ref, b_ref, o_ref):
    c = c_ref[...]
    a = c * jax.nn.sigmoid(c)
    o_ref[...] = jnp.dot(a, w_ref[...], preferred_element_type=F32,
                         precision=lax.Precision.HIGHEST) + b_ref[...]


def _modulation(c, w_mod, b_mod):
    n_layers, d, _ = w_mod.shape
    b = c.shape[0]
    return pl.pallas_call(
        _mod_kernel,
        grid=(n_layers, N_MOD),
        in_specs=[pl.BlockSpec((b, d), lambda l, j: (0, 0)),
                  pl.BlockSpec((None, d, d), lambda l, j: (l, 0, j)),
                  pl.BlockSpec((None, None, 1, d), lambda l, j: (l, j, 0, 0))],
        out_specs=pl.BlockSpec((None, None, b, d), lambda l, j: (l, j, 0, 0)),
        out_shape=jax.ShapeDtypeStruct((n_layers, N_MOD, b, d), F32),
        compiler_params=_cparams(("arbitrary", "arbitrary")),
        name="modulation",
    )(c, w_mod, b_mod.reshape(n_layers, N_MOD, 1, d))


def _inproj_kernel(x_ref, g_ref, sc_ref, sh_ref, w_ref, q_ref, ka_ref, va_ref, zb_ref, zc_ref, zd_ref,
                   *, seq):
    tm = x_ref.shape[0]
    x = x_ref[...]
    ms = jnp.mean(x * x, axis=-1, keepdims=True)
    h = (x * lax.rsqrt(ms + EPS) * g_ref[...]) * (1.0 + sc_ref[...]) + sh_ref[...]
    hb = h.astype(BF16)

    q_ref[...] = _dot(hb, w_ref[:, 0:MIX_W]).astype(BF16)
    pos = (pl.program_id(0) * tm) % seq + lax.broadcasted_iota(I32, (tm, HEAD_DIM), 0)
    lane = lax.broadcasted_iota(I32, (tm, HEAD_DIM), 1)
    onehot = jnp.where(pos // MOBA_BLOCK == lane, 1.0, 0.0).astype(BF16)
    ones = jnp.ones((tm, HEAD_DIM), BF16)
    k = _dot(hb, w_ref[:, MIX_W:2 * MIX_W]).astype(BF16)
    v = _dot(hb, w_ref[:, 2 * MIX_W:ZA_W]).astype(BF16)
    for hd in range(N_HEADS):
        src = slice(hd * HEAD_DIM, (hd + 1) * HEAD_DIM)
        ka_ref[:, hd * ATT_W:hd * ATT_W + HEAD_DIM] = k[:, src]
        ka_ref[:, hd * ATT_W + HEAD_DIM:(hd + 1) * ATT_W] = onehot
        va_ref[:, hd * ATT_W:hd * ATT_W + HEAD_DIM] = v[:, src]
        va_ref[:, hd * ATT_W + HEAD_DIM:(hd + 1) * ATT_W] = ones

    off = ZA_W
    for ref in (zb_ref, zc_ref, zd_ref):
        w = ref.shape[1]
        ref[...] = _dot(hb, w_ref[:, off:off + w]).astype(BF16)
        off += w


def _in_projection(x, g, sc, sh, w_in_p, seq, tm=512):
    t, d = x.shape
    per_seq = seq // tm
    widths = (MIX_W, N_HEADS * ATT_W, N_HEADS * ATT_W, ZB_W, ZC_W, ZD_W)
    mod_spec = pl.BlockSpec((None, 1, d), lambda i: (i // per_seq, 0, 0))
    return pl.pallas_call(
        functools.partial(_inproj_kernel, seq=seq),
        grid=(t // tm,),
        in_specs=[pl.BlockSpec((tm, d), lambda i: (i, 0)),
                  pl.BlockSpec((1, d), lambda i: (0, 0)),
                  mod_spec, mod_spec,
                  _resident((d, ZA_W + ZB_W + ZC_W + ZD_W), lambda i: (0, 0))],
        out_specs=[pl.BlockSpec((tm, w), lambda i: (i, 0)) for w in widths],
        out_shape=[jax.ShapeDtypeStruct((t, w), BF16) for w in widths],
        compiler_params=_cparams(("arbitrary",)),
        name="in_projection",
    )(x, g, sc, sh, w_in_p)


def _attn_kernel(*refs, moba):
    if moba:
        slope_ref, q_ref, k_ref, v_ref, pq_ref, pk_ref, o_ref, kmean_ref, qa_ref, m_ref, acc_ref = refs
    else:
        q_ref, k_ref, v_ref, o_ref, m_ref, acc_ref = refs
        qa_ref = q_ref
    tq, sub, tk = ATT_TQ, ATT_SUB, ATT_TK
    nsub = tq // sub
    i = pl.program_id(2)

    if moba:
        nb_pad = kmean_ref.shape[0]
        nb = k_ref.shape[0] // MOBA_BLOCK

        @pl.when(i == 0)
        def _():
            kmean_ref[...] = jnp.zeros_like(kmean_ref)

            def body(n, carry):
                start = pl.multiple_of(n * MOBA_BLOCK, MOBA_BLOCK)
                kb = k_ref[pl.ds(start, MOBA_BLOCK), 0:HEAD_DIM].astype(F32)
                kmean_ref[pl.ds(n, 1), :] = jnp.mean(kb, axis=0, keepdims=True)
                return carry
            lax.fori_loop(0, nb, body, 0)

        slope = slope_ref[pl.program_id(1)] * LOG2E
        col = lax.broadcasted_iota(I32, (sub, nb_pad), 1)
        colf = col.astype(F32)
        pq_s = []
        for c in range(nsub):
            rows = slice(c * sub, (c + 1) * sub)
            own = i * nsub + c
            qf = q_ref[rows, :].astype(F32)
            gate = _dot_nt(qf, kmean_ref[...], precision=lax.Precision.HIGHEST)
            gate = jnp.where(col < own, gate, NEG)
            sel = col == own
            for _ in range(MOBA_TOPK):
                gmax = jnp.max(gate, axis=-1, keepdims=True)
                first = jnp.min(jnp.where(gate == gmax, colf, float(nb_pad)), axis=-1, keepdims=True)
                hit = colf == first
                sel = sel | (hit & (col < own))
                gate = jnp.where(hit, -jnp.inf, gate)
            qa_ref[rows, 0:HEAD_DIM] = (qf * (HEAD_DIM ** -0.5 * LOG2E)).astype(BF16)
            qa_ref[rows, HEAD_DIM:] = jnp.where(sel, 0.0, NEG).astype(BF16)
            pq_s.append(slope * pq_ref[rows, :].astype(F32))

    m_ref[...] = jnp.full(m_ref.shape, NEG, F32)
    acc_ref[...] = jnp.zeros_like(acc_ref)

    def step(c, chunk, mask_off):
        rows = slice(c * sub, (c + 1) * sub)
        start = pl.multiple_of(chunk * tk, tk)
        s = _dot_nt(qa_ref[rows, :], k_ref[pl.ds(start, tk), :])
        if moba:
            s = s - jnp.abs(pq_s[c] - slope * pk_ref[pl.ds(chunk, 1), :].astype(F32))
        if mask_off is not None:
            qi = lax.broadcasted_iota(I32, (sub, tk), 0)
            ki = lax.broadcasted_iota(I32, (sub, tk), 1)
            s = jnp.where(ki <= qi + mask_off, s, NEG)
        m_old = m_ref[rows, :]
        m_new = jnp.maximum(m_old, jnp.max(s, axis=-1, keepdims=True))
        p = jnp.exp2(s - m_new).astype(BF16)
        acc_ref[rows, :] = jnp.exp2(m_old - m_new) * acc_ref[rows, :] + _dot(p, v_ref[pl.ds(start, tk), :])
        m_ref[rows, :] = m_new

    per_tile = tq // tk
    for c in range(nsub):
        q_lo = c * sub
        for jj in reversed(range(per_tile)):
            k_lo = jj * tk
            if k_lo > q_lo + sub - 1:
                continue
            needs_mask = k_lo + tk - 1 > q_lo
            step(c, i * per_tile + jj, q_lo - k_lo if needs_mask else None)

    def body(j, carry):
        for c in range(nsub):
            step(c, j, None)
        return carry
    lax.fori_loop(0, i * per_tile, body, 0)

    acc = acc_ref[...]
    o_ref[...] = (acc[:, :HEAD_DIM] / acc[:, HEAD_DIM:]).astype(o_ref.dtype)


def _attn_scratch():
    return [pltpu.VMEM((ATT_TQ, 1), F32), pltpu.VMEM((ATT_TQ, ATT_W), F32)]


def _moba_attention(q, ka, va, pos_col, pos_chunks, slopes, batch, seq):
    t = q.shape[0]
    nq = seq // ATT_TQ
    kv_spec = pl.BlockSpec((seq, ATT_W), lambda b, h, i, s: (b, h))
    return pl.pallas_call(
        functools.partial(_attn_kernel, moba=True),
        grid_spec=pltpu.PrefetchScalarGridSpec(
            num_scalar_prefetch=1,
            grid=(batch, N_HEADS, nq),
            in_specs=[pl.BlockSpec((ATT_TQ, HEAD_DIM), lambda b, h, i, s: (b * nq + i, h)),
                      kv_spec, kv_spec,
                      pl.BlockSpec((ATT_TQ, 1), lambda b, h, i, s: (b * nq + i, 0)),
                      pl.BlockSpec((None, seq // ATT_TK, ATT_TK), lambda b, h, i, s: (b, 0, 0))],
            out_specs=pl.BlockSpec((ATT_TQ, HEAD_DIM), lambda b, h, i, s: (b * nq + i, h)),
            scratch_shapes=[pltpu.VMEM((HEAD_DIM, HEAD_DIM), F32),
                            pltpu.VMEM((ATT_TQ, ATT_W), BF16)] + _attn_scratch()),
        out_shape=jax.ShapeDtypeStruct((t, MIX_W), BF16),
        compiler_params=_cparams(("arbitrary", "arbitrary", "arbitrary")),
        name="moba_attention",
    )(slopes, q, ka, va, pos_col, pos_chunks)


def _mla_attention(q, k, va, batch, seq):
    t = q.shape[0]
    nq = seq // ATT_TQ
    kv_spec = pl.BlockSpec((seq, ATT_W), lambda b, h, i: (b, h))
    return pl.pallas_call(
        functools.partial(_attn_kernel, moba=False),
        grid=(batch, N_HEADS, nq),
        in_specs=[pl.BlockSpec((ATT_TQ, ATT_W), lambda b, h, i: (b * nq + i, h)),
                  kv_spec, kv_spec],
        out_specs=pl.BlockSpec((ATT_TQ, HEAD_DIM), lambda b, h, i: (b * nq + i, h)),
        out_shape=jax.ShapeDtypeStruct((t, MIX_W), BF16),
        scratch_shapes=_attn_scratch(),
        compiler_params=_cparams(("arbitrary", "arbitrary", "arbitrary")),
        name="mla_attention",
    )(q, k, va)


def _gelu_tanh(x):
    return 0.5 * x * (1.0 + jnp.tanh(0.7978845608028654 * (x + 0.044715 * x * x * x)))


def _gmlp_kernel(z_ref, gv_ref, ws_ref, bs_ref, o_ref):
    tm = z_ref.shape[0]
    ck = GMLP_CHUNK
    z = _gelu_tanh(z_ref[...].astype(F32))
    row = lax.broadcasted_iota(I32, (ck, ck), 0)
    colm = lax.broadcasted_iota(I32, (ck, ck), 1)
    for g in range(N_HEADS):
        lanes = slice(g * HEAD_DIM, (g + 1) * HEAD_DIM)
        u = z[:, lanes]
        vv = z[:, MIX_W + g * HEAD_DIM:MIX_W + (g + 1) * HEAD_DIM]
        ms = jnp.mean(vv * vv, axis=-1, keepdims=True)
        vn = (vv * lax.rsqrt(ms + EPS) * gv_ref[g:g + 1, :]).astype(BF16)
        w = jnp.where(colm <= row, ws_ref[g], 0.0).astype(BF16)
        bias = bs_ref[g]
        for c in range(tm // ck):
            rows = slice(c * ck, (c + 1) * ck)
            mixed = _dot(w, vn[rows]) + bias
            o_ref[rows, lanes] = (u[rows] * mixed).astype(o_ref.dtype)


def _spatial_gating(zb, g_v, w_s, b_s, tm=512):
    t = zb.shape[0]
    ck = GMLP_CHUNK
    return pl.pallas_call(
        _gmlp_kernel,
        grid=(t // tm,),
        in_specs=[pl.BlockSpec((tm, ZB_W), lambda i: (i, 0)),
                  pl.BlockSpec((N_HEADS, HEAD_DIM), lambda i: (0, 0)),
                  pl.BlockSpec((N_HEADS, ck, ck), lambda i: (0, 0, 0)),
                  pl.BlockSpec((N_HEADS, ck, 1), lambda i: (0, 0, 0))],
        out_specs=pl.BlockSpec((tm, MIX_W), lambda i: (i, 0)),
        out_shape=jax.ShapeDtypeStruct((t, MIX_W), BF16),
        compiler_params=_cparams(("arbitrary",)),
        name="spatial_gating",
    )(zb, g_v, w_s, b_s.reshape(N_HEADS, ck, 1))


def _mla_prep_kernel(z_ref, pos_ref, invf_ref, gq_ref, gkv_ref, wq_ref, wk_ref, wv_ref,
                     q_ref, k_ref, v_ref):
    tm = z_ref.shape[0]
    z = z_ref[...].astype(F32)
    cq = z[:, :CQ_PAD]
    ckv = z[:, CQ_PAD:CQ_PAD + CKV_PAD]
    kr = z[:, CQ_PAD + CKV_PAD:]
    qn = cq * lax.rsqrt(jnp.sum(cq * cq, -1, keepdims=True) * (1.0 / MLA_Q_RANK) + EPS)
    qn = (qn * gq_ref[...]).astype(BF16)
    kvn = ckv * lax.rsqrt(jnp.sum(ckv * ckv, -1, keepdims=True) * (1.0 / MLA_KV_RANK) + EPS)
    kvn = (kvn * gkv_ref[...]).astype(BF16)

    ang = pos_ref[...].astype(F32) * invf_ref[...]
    lane = lax.broadcasted_iota(I32, ang.shape, 1)
    half = MLA_ROPE // 2
    cos = jnp.cos(ang)
    sin = jnp.sin(ang)
    sin_lo = jnp.where(lane < half, -sin, 0.0)
    sin_hi = jnp.where((lane >= half) & (lane < 2 * half), sin, 0.0)

    def rope(r):
        return (r * cos + pltpu.roll(r, KR_PAD - half, 1) * sin_lo
                + pltpu.roll(r, half, 1) * sin_hi)

    q = _dot(qn, wq_ref[...])
    kn = _dot(kvn, wk_ref[...])
    v = _dot(kvn, wv_ref[...]).astype(BF16)
    k_rope = rope(kr)
    scale = MLA_QK ** -0.5 * LOG2E
    ones = jnp.ones((tm, HEAD_DIM), BF16)
    for h in range(N_HEADS):
        a = h * ATT_W
        b = a + MLA_NOPE
        q_ref[:, a:b] = (q[:, a:b] * scale).astype(q_ref.dtype)
        q_ref[:, b:a + ATT_W] = (rope(q[:, b:a + ATT_W]) * scale).astype(q_ref.dtype)
        k_ref[:, a:b] = kn[:, a:b].astype(k_ref.dtype)
        k_ref[:, b:a + ATT_W] = k_rope.astype(k_ref.dtype)
        v_ref[:, a:b] = v[:, h * HEAD_DIM:(h + 1) * HEAD_DIM]
        v_ref[:, b:a + ATT_W] = ones


def _mla_prep(zc, pos_col, invf, gq, gkv, wq, wk, wv, tm=512):
    t = zc.shape[0]
    qk_w = N_HEADS * ATT_W
    full = lambda i: (0, 0)
    out_spec = pl.BlockSpec((tm, qk_w), lambda i: (i, 0))
    out_shape = jax.ShapeDtypeStruct((t, qk_w), BF16)
    return pl.pallas_call(
        _mla_prep_kernel,
        grid=(t // tm,),
        in_specs=[pl.BlockSpec((tm, ZC_W), lambda i: (i, 0)),
                  pl.BlockSpec((tm, 1), lambda i: (i, 0)),
                  pl.BlockSpec((1, KR_PAD), full),
                  pl.BlockSpec((1, CQ_PAD), full),
                  pl.BlockSpec((1, CKV_PAD), full),
                  pl.BlockSpec((CQ_PAD, qk_w), full),
                  pl.BlockSpec((CKV_PAD, qk_w), full),
                  pl.BlockSpec((CKV_PAD, MIX_W), full)],
        out_specs=[out_spec, out_spec, out_spec],
        out_shape=[out_shape, out_shape, out_shape],
        compiler_params=_cparams(("arbitrary",)),
        name="mla_prep",
    )(zc, pos_col, invf, gq, gkv, wq, wk, wv)


def _glu(z):
    z = z.astype(F32)
    return z[:, :MIX_W] * jax.nn.sigmoid(z[:, MIX_W:])


def _conv_kernel(z_ref, zprev_ref, w_ref, b_ref, g_ref, beta_ref, o_ref, ybuf_ref, ysh_ref, *, per_seq):
    tm = z_ref.shape[0]
    first = (pl.program_id(0) % per_seq) == 0
    ybuf_ref[0:CONV_HALO, :] = jnp.where(first, 0.0, _glu(zprev_ref[...]))
    ybuf_ref[CONV_HALO:, :] = _glu(z_ref[...])
    span = tm + CONV_HALO - SUBLANES
    for o in range(1, SUBLANES):
        ysh_ref[o - 1, :, :] = ybuf_ref[pl.ds(o, span), :]
    rows = 64
    shift = CONV_HALO - (CONV_WIDTH - 1)
    for r in range(tm // rows):
        acc = jnp.zeros((rows, MIX_W), F32) + b_ref[...]
        for j in range(CONV_WIDTH):
            o = (shift + j) % SUBLANES
            base = r * rows + shift + j - o
            tap = ybuf_ref[pl.ds(base, rows), :] if o == 0 else ysh_ref[o - 1, pl.ds(base, rows), :]
            acc = acc + w_ref[j:j + 1, :] * tap
        mu = jnp.mean(acc, axis=-1, keepdims=True)
        xc = acc - mu
        y = xc * lax.rsqrt(jnp.mean(xc * xc, axis=-1, keepdims=True) + EPS)
        y = y * g_ref[...] + beta_ref[...]
        o_ref[r * rows:(r + 1) * rows, :] = (y * jax.nn.sigmoid(y)).astype(o_ref.dtype)


def _conformer_conv(zd, w_dw, b_dw, g_ln, b_ln, seq, tm=512):
    t = zd.shape[0]
    per_seq = seq // tm
    halo_blocks = tm // CONV_HALO
    full = lambda i: (0, 0)
    return pl.pallas_call(
        functools.partial(_conv_kernel, per_seq=per_seq),
        grid=(t // tm,),
        in_specs=[pl.BlockSpec((tm, ZD_W), lambda i: (i, 0)),
                  pl.BlockSpec((CONV_HALO, ZD_W),
                               lambda i: (jnp.maximum(i * halo_blocks - 1, 0), 0)),
                  pl.BlockSpec((CONV_WIDTH, MIX_W), full),
                  pl.BlockSpec((1, MIX_W), full),
                  pl.BlockSpec((1, MIX_W), full),
                  pl.BlockSpec((1, MIX_W), full)],
        out_specs=pl.BlockSpec((tm, MIX_W), lambda i: (i, 0)),
        out_shape=jax.ShapeDtypeStruct((t, MIX_W), BF16),
        scratch_shapes=[pltpu.VMEM((tm + CONV_HALO, MIX_W), F32),
                        pltpu.VMEM((SUBLANES - 1, tm + CONV_HALO - SUBLANES, MIX_W), F32)],
        compiler_params=_cparams(("arbitrary",)),
        name="conformer_conv",
    )(zd, zd, w_dw, b_dw, g_ln, b_ln)


def _outproj_kernel(x_ref, ya_ref, yb_ref, yc_ref, yd_ref, wo_ref, gt1_ref, g_ref, sc_ref, sh_ref,
                    gt2_ref, wr_ref, br_ref, wsg_ref, wsu_ref, wsd_ref,
                    xo_ref, hp_ref, idx_ref, wts_ref, rank_ref, cnt_ref, carry_ref):
    tm = x_ref.shape[0]

    @pl.when(pl.program_id(0) == 0)
    def _():
        carry_ref[...] = jnp.zeros_like(carry_ref)

    y = _dot(ya_ref[...], wo_ref[0:MIX_W, :])
    y = y + _dot(yb_ref[...], wo_ref[MIX_W:2 * MIX_W, :])
    y = y + _dot(yc_ref[...], wo_ref[2 * MIX_W:3 * MIX_W, :])
    y = y + _dot(yd_ref[...], wo_ref[3 * MIX_W:, :])
    x = x_ref[...] + gt1_ref[...] * y
    ms = jnp.mean(x * x, axis=-1, keepdims=True)
    h = (x * lax.rsqrt(ms + EPS) * g_ref[...]) * (1.0 + sc_ref[...]) + sh_ref[...]
    hb = h.astype(BF16)
    hp_ref[...] = _pack_rows(h)

    act = _dot(hb, wsg_ref[...])
    act = (act * jax.nn.sigmoid(act) * _dot(hb, wsu_ref[...])).astype(BF16)
    xo_ref[...] = x + gt2_ref[...] * _dot(act, wsd_ref[...])

    logits = _dot_nt(wr_ref[...], h, precision=lax.Precision.HIGHEST)
    scores = jax.nn.sigmoid(logits)
    cur = scores + br_ref[...]
    erow = lax.broadcasted_iota(I32, (N_EXPERTS, tm), 0).astype(F32)
    picked = jnp.zeros((N_EXPERTS, tm), F32)
    hits, idxs, wts = [], [], []
    for _ in range(TOP_K):
        cmax = jnp.max(cur, axis=0, keepdims=True)
        first = jnp.min(jnp.where(cur == cmax, erow, float(N_EXPERTS)), axis=0, keepdims=True)
        hit = erow == first
        hits.append(hit)
        idxs.append(first)
        wts.append(jnp.sum(jnp.where(hit, scores, 0.0), axis=0, keepdims=True))
        picked = jnp.where(hit, 1.0, picked)
        cur = jnp.where(hit, -jnp.inf, cur)
    wsum = wts[0]
    for w in wts[1:]:
        wsum = wsum + w
    ti = lax.broadcasted_iota(I32, (tm, tm), 0)
    tj = lax.broadcasted_iota(I32, (tm, tm), 1)
    before = jnp.where(ti < tj, 1.0, 0.0).astype(BF16)
    prior = _dot(picked.astype(BF16), before) + carry_ref[...]
    for k in range(TOP_K):
        idx_ref[k:k + 1, :] = idxs[k].astype(I32)
        wts_ref[k:k + 1, :] = wts[k] / wsum * ROUTED_SCALE
        rank_ref[k:k + 1, :] = jnp.sum(jnp.where(hits[k], prior, 0.0), axis=0,
                                       keepdims=True).astype(I32)
    carry_ref[...] = carry_ref[...] + jnp.sum(picked, axis=1, keepdims=True)
    cnt_ref[...] = jnp.broadcast_to(carry_ref[...], cnt_ref.shape).astype(I32)


def _out_projection(x, ys, w_out, gt1, g_ffn, sc2, sh2, gt2, w_rt, b_r, wsg, wsu, wsd, seq, tm=512):
    t, d = x.shape
    per_seq = seq // tm
    full = lambda i: (0, 0)
    mod_spec = pl.BlockSpec((None, 1, d), lambda i: (i // per_seq, 0, 0))
    y_spec = pl.BlockSpec((tm, MIX_W), lambda i: (i, 0))
    tok_spec = pl.BlockSpec((TOP_K, tm), lambda i: (0, i))
    return pl.pallas_call(
        _outproj_kernel,
        grid=(t // tm,),
        in_specs=[pl.BlockSpec((tm, d), lambda i: (i, 0)),
                  y_spec, y_spec, y_spec, y_spec,
                  _resident((4 * MIX_W, d), full),
                  mod_spec,
                  pl.BlockSpec((1, d), full),
                  mod_spec, mod_spec, mod_spec,
                  pl.BlockSpec((N_EXPERTS, d), full),
                  pl.BlockSpec((N_EXPERTS, 1), full),
                  _resident((d, D_SHARED), full),
                  _resident((d, D_SHARED), full),
                  _resident((D_SHARED, d), full)],
        out_specs=[pl.BlockSpec((tm, d), lambda i: (i, 0)),
                   pl.BlockSpec((tm, PACK_W), lambda i: (i, 0)),
                   tok_spec, tok_spec, tok_spec,
                   pl.BlockSpec((N_EXPERTS, 128), full)],
        out_shape=[jax.ShapeDtypeStruct((t, d), F32),
                   jax.ShapeDtypeStruct((t, PACK_W), U32),
                   jax.ShapeDtypeStruct((TOP_K, t), I32),
                   jax.ShapeDtypeStruct((TOP_K, t), F32),
                   jax.ShapeDtypeStruct((TOP_K, t), I32),
                   jax.ShapeDtypeStruct((N_EXPERTS, 128), I32)],
        scratch_shapes=[pltpu.VMEM((N_EXPERTS, 1), F32)],
        compiler_params=_cparams(("arbitrary",)),
        name="out_projection_router",
    )(x, *ys, w_out, gt1, g_ffn, sc2, sh2, gt2, w_rt, b_r, wsg, wsu, wsd)


def _row_copy(src_ref, src_row, dst_ref, dst_row, sem):
    return pltpu.make_async_copy(src_ref.at[pl.ds(src_row, 1)], dst_ref.at[pl.ds(dst_row, 1)], sem)


def _dispatch_kernel(dest_ref, hp_ref, xs_ref, sem):
    tm = hp_ref.shape[0]

    def start(t, carry):
        for k in range(TOP_K):
            _row_copy(hp_ref, t, xs_ref, dest_ref[t * TOP_K + k], sem).start(priority=k % 2)
        return carry
    lax.fori_loop(0, tm, start, 0)

    def wait(t, carry):
        for k in range(TOP_K):
            _row_copy(hp_ref, t, xs_ref, dest_ref[t * TOP_K + k], sem).wait()
        return carry
    lax.fori_loop(0, tm, wait, 0)


def _dispatch(dest, hp, p_rows, tm=512):
    t = hp.shape[0]
    return pl.pallas_call(
        _dispatch_kernel,
        grid=(t // tm,),
        in_specs=[pl.BlockSpec((tm * TOP_K,), lambda i: (i,), memory_space=pltpu.SMEM),
                  pl.BlockSpec((tm, PACK_W), lambda i: (i, 0))],
        out_specs=pl.BlockSpec(memory_space=pl.ANY),
        out_shape=jax.ShapeDtypeStruct((p_rows, PACK_W), U32),
        scratch_shapes=[pltpu.SemaphoreType.DMA(())],
        compiler_params=_cparams(("arbitrary",)),
        name="moe_dispatch",
    )(dest, hp)


def _expert_kernel(be_ref, bv_ref, x_ref, wg_ref, wu_ref, wd_ref, y_ref, wgb_ref, wub_ref, wdb_ref):
    i = pl.program_id(0)
    expert = be_ref[i]
    prev = be_ref[jnp.maximum(i - 1, 0)]

    @pl.when((i == 0) | (expert != prev))
    def _():
        wgb_ref[...] = wg_ref[...].astype(BF16)
        wub_ref[...] = wu_ref[...].astype(BF16)
        wdb_ref[...] = wd_ref[...].astype(BF16)

    valid = bv_ref[i]

    @pl.when(valid > 0)
    def _():
        lo, hi = _unpack_rows(x_ref[...])
        rows = lax.broadcasted_iota(I32, (x_ref.shape[0], 1), 0)
        xb = jnp.where(rows < valid, jnp.concatenate([lo, hi], axis=1), 0.0).astype(BF16)
        gate = _dot(xb, wgb_ref[...])
        act = (gate * jax.nn.sigmoid(gate) * _dot(xb, wub_ref[...])).astype(BF16)
        y_ref[...] = _pack_rows(_dot(act, wdb_ref[...]))

    @pl.when(valid == 0)
    def _():
        y_ref[...] = jnp.zeros_like(y_ref)


def _experts(block_e, block_valid, xs, w_gate, w_up, w_down, layer):
    p_rows = xs.shape[0]
    d = D_MODEL
    return pl.pallas_call(
        _expert_kernel,
        grid_spec=pltpu.PrefetchScalarGridSpec(
            num_scalar_prefetch=2,
            grid=(p_rows // MOE_BLOCK,),
            in_specs=[pl.BlockSpec((MOE_BLOCK, PACK_W), lambda i, be, bv: (i, 0)),
                      pl.BlockSpec((None, None, d, D_EXPERT), lambda i, be, bv: (layer, be[i], 0, 0)),
                      pl.BlockSpec((None, None, d, D_EXPERT), lambda i, be, bv: (layer, be[i], 0, 0)),
                      pl.BlockSpec((None, None, D_EXPERT, d), lambda i, be, bv: (layer, be[i], 0, 0))],
            out_specs=pl.BlockSpec((MOE_BLOCK, PACK_W), lambda i, be, bv: (i, 0)),
            scratch_shapes=[pltpu.VMEM((d, D_EXPERT), BF16),
                            pltpu.VMEM((d, D_EXPERT), BF16),
                            pltpu.VMEM((D_EXPERT, d), BF16)]),
        out_shape=jax.ShapeDtypeStruct((p_rows, PACK_W), U32),
        compiler_params=_cparams(("arbitrary",)),
        name="moe_experts",
    )(block_e, block_valid, xs, w_gate, w_up, w_down)


def _combine_kernel(dest_ref, dnext_ref, wts_ref, x_ref, gt2_ref, gf_ref, ys_ref, o_ref, buf_ref, sem,
                    *, final):
    tm = x_ref.shape[0]
    i = pl.program_id(0)
    slot = lax.rem(i, 2)

    def gather(rows_ref, s, start):
        def body(t, carry):
            for k in range(TOP_K):
                copy = _row_copy(ys_ref, rows_ref[t * TOP_K + k], buf_ref.at[s, k], t, sem.at[s])
                if start:
                    copy.start(priority=k % 2)
                else:
                    copy.wait()
            return carry
        lax.fori_loop(0, tm, body, 0)

    @pl.when(i == 0)
    def _():
        gather(dest_ref, 0, True)

    @pl.when(i + 1 < pl.num_programs(0))
    def _():
        gather(dnext_ref, 1 - slot, True)

    gather(dest_ref, slot, False)

    acc_lo = jnp.zeros((tm, PACK_W), F32)
    acc_hi = jnp.zeros((tm, PACK_W), F32)
    for k in range(TOP_K):
        lo, hi = _unpack_rows(buf_ref[slot, k])
        w = wts_ref[:, k:k + 1]
        acc_lo = acc_lo + w * lo
        acc_hi = acc_hi + w * hi
    x = x_ref[...] + gt2_ref[...] * jnp.concatenate([acc_lo, acc_hi], axis=1)
    if final:
        ms = jnp.mean(x * x, axis=-1, keepdims=True)
        x = x * lax.rsqrt(ms + EPS) * gf_ref[...]
    o_ref[...] = x


def _combine(dest, wts, x_mid, gt2, g_final, ys, seq, final, tm=256):
    t, d = x_mid.shape
    per_seq = seq // tm
    steps = t // tm
    return pl.pallas_call(
        functools.partial(_combine_kernel, final=final),
        grid=(steps,),
        in_specs=[pl.BlockSpec((tm * TOP_K,), lambda i: (i,), memory_space=pltpu.SMEM),
                  pl.BlockSpec((tm * TOP_K,), lambda i: (jnp.minimum(i + 1, steps - 1),),
                               memory_space=pltpu.SMEM),
                  pl.BlockSpec((tm, TOP_K), lambda i: (i, 0)),
                  pl.BlockSpec((tm, d), lambda i: (i, 0)),
                  pl.BlockSpec((None, 1, d), lambda i: (i // per_seq, 0, 0)),
                  pl.BlockSpec((1, d), lambda i: (0, 0)),
                  pl.BlockSpec(memory_space=pl.ANY)],
        out_specs=pl.BlockSpec((tm, d), lambda i: (i, 0)),
        out_shape=jax.ShapeDtypeStruct((t, d), F32),
        scratch_shapes=[pltpu.VMEM((2, TOP_K, tm, PACK_W), U32),
                        pltpu.SemaphoreType.DMA((2,))],
        compiler_params=_cparams(("arbitrary",)),
        name="moe_combine",
    )(dest, dest, wts, x_mid, gt2, g_final, ys)


def _pad_to(a, axis, size):
    pad = [(0, 0)] * a.ndim
    pad[axis] = (0, size - a.shape[axis])
    return jnp.pad(a, pad)


def _in_weight(w_in):
    off_b = ZA_W
    off_cq = off_b + ZB_W
    off_ckv = off_cq + MLA_Q_RANK
    off_kr = off_ckv + MLA_KV_RANK
    off_d = off_kr + MLA_ROPE
    parts = [w_in[:, :off_cq],
             _pad_to(w_in[:, off_cq:off_ckv], 1, CQ_PAD),
             _pad_to(w_in[:, off_ckv:off_kr], 1, CKV_PAD),
             _pad_to(w_in[:, off_kr:off_d], 1, KR_PAD),
             w_in[:, off_d:]]
    return jnp.concatenate(parts, axis=1).astype(BF16)


def _mla_weights(w_uq, w_ukv):
    wq = w_uq.reshape(MLA_Q_RANK, N_HEADS, MLA_QK)
    wq = _pad_to(_pad_to(wq, 2, ATT_W), 0, CQ_PAD).reshape(CQ_PAD, N_HEADS * ATT_W)
    wkv = _pad_to(w_ukv.reshape(MLA_KV_RANK, N_HEADS, MLA_NOPE + HEAD_DIM), 0, CKV_PAD)
    wk = _pad_to(wkv[:, :, :MLA_NOPE], 2, ATT_W).reshape(CKV_PAD, N_HEADS * ATT_W)
    wv = wkv[:, :, MLA_NOPE:].reshape(CKV_PAD, MIX_W)
    return wq.astype(BF16), wk.astype(BF16), wv.astype(BF16)


def _lookup(table, idx):
    ids = jnp.arange(table.shape[0], dtype=I32).reshape((-1,) + (1,) * idx.ndim)
    vals = table.reshape(ids.shape)
    return jnp.sum(jnp.where(idx[None] == ids, vals, 0), axis=0)


def _route_tables(idx_t, rank_t, counts, p_rows):
    m = MOE_BLOCK
    padded = (counts + m - 1) // m * m
    pend = jnp.cumsum(padded)
    pstart = pend - padded
    dest_t = _lookup(pstart, idx_t) + rank_t
    blk_start = jnp.arange(p_rows // m, dtype=I32) * m
    block_e = jnp.minimum(jnp.sum(pend[None, :] <= blk_start[:, None], axis=1), N_EXPERTS - 1)
    block_e = block_e.astype(I32)
    block_valid = jnp.clip(_lookup(pstart + counts, block_e) - blk_start, 0, m).astype(I32)
    dest = dest_t.astype(I32).T.reshape(-1)
    return dest, block_e, block_valid


def kernel(x, c, positions, w_mod, b_mod, g_attn, w_in, g_gmlp_v, w_spatial, b_spatial, g_mla_q, w_mla_uq, g_mla_kv, w_mla_ukv, w_conv_dw, b_conv_dw, g_conv_ln, b_conv_ln, w_out, g_ffn, w_router, b_router, w_exp_gate, w_exp_up, w_exp_down, w_sh_gate, w_sh_up, w_sh_down, g_final):
    batch, seq, d = x.shape
    n_layers = w_mod.shape[0]
    t = batch * seq
    p_rows = t * TOP_K + N_EXPERTS * MOE_BLOCK

    xf = x.reshape(t, d)
    pos_col = positions.reshape(t, 1)
    pos_chunks = positions.reshape(batch, seq // ATT_TK, ATT_TK)
    slopes = 2.0 ** (-8.0 * jnp.arange(1, N_HEADS + 1, dtype=F32) / N_HEADS)
    half = MLA_ROPE // 2
    inv_freq = ROPE_THETA ** (-jnp.arange(half, dtype=F32) * (2.0 / MLA_ROPE))
    invf = _pad_to(jnp.concatenate([inv_freq, inv_freq]), 0, KR_PAD).reshape(1, KR_PAD)

    mod = _modulation(c, w_mod, b_mod).reshape(n_layers, N_MOD, batch, 1, d)

    routed = None
    for l in range(n_layers):
        sh1, sc1, gt1, sh2, sc2, gt2 = (mod[l, j] for j in range(N_MOD))
        if routed is not None:
            xf = _combine(*routed, seq=seq, final=False)

        qm, ka, va, zb, zc, zd = _in_projection(xf, g_attn[l].reshape(1, d), sc1, sh1,
                                                _in_weight(w_in[l]), seq)
        y_a = _moba_attention(qm, ka, va, pos_col, pos_chunks, slopes, batch, seq)
        y_b = _spatial_gating(zb, g_gmlp_v[l], w_spatial[l], b_spatial[l])
        wq, wk, wv = _mla_weights(w_mla_uq[l], w_mla_ukv[l])
        q, k, v = _mla_prep(zc, pos_col, invf,
                            _pad_to(g_mla_q[l], 0, CQ_PAD).reshape(1, CQ_PAD),
                            _pad_to(g_mla_kv[l], 0, CKV_PAD).reshape(1, CKV_PAD), wq, wk, wv)
        y_c = _mla_attention(q, k, v, batch, seq)
        y_d = _conformer_conv(zd, w_conv_dw[l].reshape(CONV_WIDTH, MIX_W),
                              b_conv_dw[l].reshape(1, MIX_W), g_conv_ln[l].reshape(1, MIX_W),
                              b_conv_ln[l].reshape(1, MIX_W), seq)

        x_mid, hp, idx_t, wts_t, rank_t, cnt = _out_projection(
            xf, (y_a, y_b, y_c, y_d), w_out[l].astype(BF16), gt1, g_ffn[l].reshape(1, d),
            sc2, sh2, gt2, w_router[l].T, b_router[l].reshape(N_EXPERTS, 1),
            w_sh_gate[l].astype(BF16), w_sh_up[l].astype(BF16), w_sh_down[l].astype(BF16), seq)

        dest, block_e, block_valid = _route_tables(idx_t, rank_t, cnt[:, 0], p_rows)
        xs = _dispatch(dest, hp, p_rows)
        ys = _experts(block_e, block_valid, xs, w_exp_gate, w_exp_up, w_exp_down, l)
        routed = (dest, wts_t.T, x_mid, gt2, g_final.reshape(1, d), ys)

    out = _combine(*routed, seq=seq, final=True)
    return out.reshape(batch, seq, d)
```

```python
import functools

import jax
import jax.numpy as jnp
from jax import lax
from jax.experimental import pallas as pl
from jax.experimental.pallas import tpu as pltpu

F32 = jnp.float32
BF16 = jnp.bfloat16
I32 = jnp.int32
U32 = jnp.uint32

D_MODEL = 2048
HEAD_DIM = 128
N_HEADS = 4
MIX_W = N_HEADS * HEAD_DIM
MOBA_BLOCK = 256
MOBA_TOPK = 3
GMLP_CHUNK = 128
MLA_Q_RANK = 448
MLA_KV_RANK = 160
MLA_NOPE = 128
MLA_ROPE = 64
MLA_QK = MLA_NOPE + MLA_ROPE
ROPE_THETA = 10000.0
CONV_WIDTH = 31
CONV_HALO = 32
SUBLANES = 8
N_EXPERTS = 64
TOP_K = 8
D_EXPERT = 512
D_SHARED = 512
ROUTED_SCALE = 2.5
MOE_BLOCK = 512
N_MOD = 6
EPS = 1e-6
NEG = -1e30
LOG2E = 1.4426950408889634

ATT_TQ = 2048
ATT_SUB = 256
ATT_TK = 1024
ATT_W = 2 * HEAD_DIM

ZA_W = 3 * MIX_W
ZB_W = 2 * MIX_W
CQ_PAD = 512
CKV_PAD = 256
KR_PAD = 128
ZC_W = CQ_PAD + CKV_PAD + KR_PAD
ZD_W = 2 * MIX_W

VMEM_LIMIT = 56 * 1024 * 1024
PACK_W = D_MODEL // 2


def _cparams(sem):
    return pltpu.CompilerParams(dimension_semantics=sem, vmem_limit_bytes=VMEM_LIMIT)


def _resident(shape, index_map):
    return pl.BlockSpec(shape, index_map, pipeline_mode=pl.Buffered(1))


def _dot(a, b):
    return jnp.dot(a, b, preferred_element_type=F32)


def _dot_nt(a, b, precision=None):
    return lax.dot_general(a, b, (((1,), (1,)), ((), ())), preferred_element_type=F32,
                           precision=precision)


def _pack_rows(y):
    half = y.shape[1] // 2
    lo = lax.bitcast_convert_type(y[:, :half].astype(BF16).astype(F32), U32)
    hi = lax.bitcast_convert_type(y[:, half:].astype(BF16).astype(F32), U32)
    return (lo >> 16) | hi


def _unpack_rows(u):
    lo = lax.bitcast_convert_type(u << 16, F32)
    hi = lax.bitcast_convert_type(u & jnp.uint32(0xFFFF0000), F32)
    return lo, hi


def _mod_kernel(c_ref, w_ref, b_ref, o_ref):
    c = c_ref[...]
    a = c * jax.nn.sigmoid(c)
    o_ref[...] = jnp.dot(a, w_ref[...], preferred_element_type=F32,
                         precision=lax.Precision.HIGHEST) + b_ref[...]


def _modulation(c, w_mod, b_mod):
    n_layers, d, _ = w_mod.shape
    b = c.shape[0]
    return pl.pallas_call(
        _mod_kernel,
        grid=(n_layers, N_MOD),
        in_specs=[pl.BlockSpec((b, d), lambda l, j: (0, 0)),
                  pl.BlockSpec((None, d, d), lambda l, j: (l, 0, j)),
                  pl.BlockSpec((None, None, 1, d), lambda l, j: (l, j, 0, 0))],
        out_specs=pl.BlockSpec((None, None, b, d), lambda l, j: (l, j, 0, 0)),
        out_shape=jax.ShapeDtypeStruct((n_layers, N_MOD, b, d), F32),
        compiler_params=_cparams(("arbitrary", "arbitrary")),
        name="modulation",
    )(c, w_mod, b_mod.reshape(n_layers, N_MOD, 1, d))


def _inproj_kernel(x_ref, g_ref, sc_ref, sh_ref, w_ref, q_ref, ka_ref, va_ref, zb_ref, zc_ref, zd_ref,
                   *, seq):
    tm = x_ref.shape[0]
    x = x_ref[...]
    ms = jnp.mean(x * x, axis=-1, keepdims=True)
    h = (x * lax.rsqrt(ms + EPS) * g_ref[...]) * (1.0 + sc_ref[...]) + sh_ref[...]
    hb = h.astype(BF16)

    q_ref[...] = _dot(hb, w_ref[:, 0:MIX_W]).astype(BF16)
    pos = (pl.program_id(0) * tm) % seq + lax.broadcasted_iota(I32, (tm, HEAD_DIM), 0)
    lane = lax.broadcasted_iota(I32, (tm, HEAD_DIM), 1)
    onehot = jnp.where(pos // MOBA_BLOCK == lane, 1.0, 0.0).astype(BF16)
    ones = jnp.ones((tm, HEAD_DIM), BF16)
    k = _dot(hb, w_ref[:, MIX_W:2 * MIX_W]).astype(BF16)
    v = _dot(hb, w_ref[:, 2 * MIX_W:ZA_W]).astype(BF16)
    for hd in range(N_HEADS):
        src = slice(hd * HEAD_DIM, (hd + 1) * HEAD_DIM)
        ka_ref[:, hd * ATT_W:hd * ATT_W + HEAD_DIM] = k[:, src]
        ka_ref[:, hd * ATT_W + HEAD_DIM:(hd + 1) * ATT_W] = onehot
        va_ref[:, hd * ATT_W:hd * ATT_W + HEAD_DIM] = v[:, src]
        va_ref[:, hd * ATT_W + HEAD_DIM:(hd + 1) * ATT_W] = ones

    off = ZA_W
    for ref in (zb_ref, zc_ref, zd_ref):
        w = ref.shape[1]
        ref[...] = _dot(hb, w_ref[:, off:off + w]).astype(BF16)
        off += w


def _in_projection(x, g, sc, sh, w_in_p, seq, tm=512):
    t, d = x.shape
    per_seq = seq // tm
    widths = (MIX_W, N_HEADS * ATT_W, N_HEADS * ATT_W, ZB_W, ZC_W, ZD_W)
    mod_spec = pl.BlockSpec((None, 1, d), lambda i: (i // per_seq, 0, 0))
    return pl.pallas_call(
        functools.partial(_inproj_kernel, seq=seq),
        grid=(t // tm,),
        in_specs=[pl.BlockSpec((tm, d), lambda i: (i, 0)),
                  pl.BlockSpec((1, d), lambda i: (0, 0)),
                  mod_spec, mod_spec,
                  _resident((d, ZA_W + ZB_W + ZC_W + ZD_W), lambda i: (0, 0))],
        out_specs=[pl.BlockSpec((tm, w), lambda i: (i, 0)) for w in widths],
        out_shape=[jax.ShapeDtypeStruct((t, w), BF16) for w in widths],
        compiler_params=_cparams(("arbitrary",)),
        name="in_projection",
    )(x, g, sc, sh, w_in_p)


def _attn_kernel(*refs, moba):
    if moba:
        slope_ref, q_ref, k_ref, v_ref, pq_ref, pk_ref, o_ref, kmean_ref, qa_ref, m_ref, acc_ref = refs
    else:
        q_ref, k_ref, v_ref, o_ref, m_ref, acc_ref = refs
        qa_ref = q_ref
    tq, sub, tk = ATT_TQ, ATT_SUB, ATT_TK
    nsub = tq // sub
    i = pl.program_id(2)

    if moba:
        nb_pad = kmean_ref.shape[0]
        nb = k_ref.shape[0] // MOBA_BLOCK

        @pl.when(i == 0)
        def _():
            kmean_ref[...] = jnp.zeros_like(kmean_ref)

            def body(n, carry):
                start = pl.multiple_of(n * MOBA_BLOCK, MOBA_BLOCK)
                kb = k_ref[pl.ds(start, MOBA_BLOCK), 0:HEAD_DIM].astype(F32)
                kmean_ref[pl.ds(n, 1), :] = jnp.mean(kb, axis=0, keepdims=True)
                return carry
            lax.fori_loop(0, nb, body, 0)

        slope = slope_ref[pl.program_id(1)] * LOG2E
        col = lax.broadcasted_iota(I32, (sub, nb_pad), 1)
        colf = col.astype(F32)
        pq_s = []
        for c in range(nsub):
            rows = slice(c * sub, (c + 1) * sub)
            own = i * nsub + c
            qf = q_ref[rows, :].astype(F32)
            gate = _dot_nt(qf, kmean_ref[...], precision=lax.Precision.HIGHEST)
            gate = jnp.where(col < own, gate, NEG)
            sel = col == own
            for _ in range(MOBA_TOPK):
                gmax = jnp.max(gate, axis=-1, keepdims=True)
                first = jnp.min(jnp.where(gate == gmax, colf, float(nb_pad)), axis=-1, keepdims=True)
                hit = colf == first
                sel = sel | (hit & (col < own))
                gate = jnp.where(hit, -jnp.inf, gate)
            qa_ref[rows, 0:HEAD_DIM] = (qf * (HEAD_DIM ** -0.5 * LOG2E)).astype(BF16)
            qa_ref[rows, HEAD_DIM:] = jnp.where(sel, 0.0, NEG).astype(BF16)
            pq_s.append(slope * pq_ref[rows, :].astype(F32))

    m_ref[...] = jnp.full(m_ref.shape, NEG, F32)
    acc_ref[...] = jnp.zeros_like(acc_ref)

    def step(c, chunk, mask_off):
        rows = slice(c * sub, (c + 1) * sub)
        start = pl.multiple_of(chunk * tk, tk)
        s = _dot_nt(qa_ref[rows, :], k_ref[pl.ds(start, tk), :])
        if moba:
            s = s - jnp.abs(pq_s[c] - slope * pk_ref[pl.ds(chunk, 1), :].astype(F32))
        if mask_off is not None:
            qi = lax.broadcasted_iota(I32, (sub, tk), 0)
            ki = lax.broadcasted_iota(I32, (sub, tk), 1)
            s = jnp.where(ki <= qi + mask_off, s, NEG)
        m_old = m_ref[rows, :]
        m_new = jnp.maximum(m_old, jnp.max(s, axis=-1, keepdims=True))
        p = jnp.exp2(s - m_new).astype(BF16)
        acc_ref[rows, :] = jnp.exp2(m_old - m_new) * acc_ref[rows, :] + _dot(p, v_ref[pl.ds(start, tk), :])
        m_ref[rows, :] = m_new

    per_tile = tq // tk
    for c in range(nsub):
        q_lo = c * sub
        for jj in reversed(range(per_tile)):
            k_lo = jj * tk
            if k_lo > q_lo + sub - 1:
                continue
            needs_mask = k_lo + tk - 1 > q_lo
            step(c, i * per_tile + jj, q_lo - k_lo if needs_mask else None)

    def body(j, carry):
        for c in range(nsub):
            step(c, j, None)
        return carry
    lax.fori_loop(0, i * per_tile, body, 0)

    acc = acc_ref[...]
    o_ref[...] = (acc[:, :HEAD_DIM] / acc[:, HEAD_DIM:]).astype(o_ref.dtype)


def _attn_scratch():
    return [pltpu.VMEM((ATT_TQ, 1), F32), pltpu.VMEM((ATT_TQ, ATT_W), F32)]


def _moba_attention(q, ka, va, pos_col, pos_chunks, slopes, batch, seq):
    t = q.shape[0]
    nq = seq // ATT_TQ
    kv_spec = pl.BlockSpec((seq, ATT_W), lambda b, h, i, s: (b, h))
    return pl.pallas_call(
        functools.partial(_attn_kernel, moba=True),
        grid_spec=pltpu.PrefetchScalarGridSpec(
            num_scalar_prefetch=1,
            grid=(batch, N_HEADS, nq),
            in_specs=[pl.BlockSpec((ATT_TQ, HEAD_DIM), lambda b, h, i, s: (b * nq + i, h)),
                      kv_spec, kv_spec,
                      pl.BlockSpec((ATT_TQ, 1), lambda b, h, i, s: (b * nq + i, 0)),
                      pl.BlockSpec((None, seq // ATT_TK, ATT_TK), lambda b, h, i, s: (b, 0, 0))],
            out_specs=pl.BlockSpec((ATT_TQ, HEAD_DIM), lambda b, h, i, s: (b * nq + i, h)),
            scratch_shapes=[pltpu.VMEM((HEAD_DIM, HEAD_DIM), F32),
                            pltpu.VMEM((ATT_TQ, ATT_W), BF16)] + _attn_scratch()),
        out_shape=jax.ShapeDtypeStruct((t, MIX_W), BF16),
        compiler_params=_cparams(("arbitrary", "arbitrary", "arbitrary")),
        name="moba_attention",
    )(slopes, q, ka, va, pos_col, pos_chunks)


def _mla_attention(q, k, va, batch, seq):
    t = q.shape[0]
    nq = seq // ATT_TQ
    kv_spec = pl.BlockSpec((seq, ATT_W), lambda b, h, i: (b, h))
    return pl.pallas_call(
        functools.partial(_attn_kernel, moba=False),
        grid=(batch, N_HEADS, nq),
        in_specs=[pl.BlockSpec((ATT_TQ, ATT_W), lambda b, h, i: (b * nq + i, h)),
                  kv_spec, kv_spec],
        out_specs=pl.BlockSpec((ATT_TQ, HEAD_DIM), lambda b, h, i: (b * nq + i, h)),
        out_shape=jax.ShapeDtypeStruct((t, MIX_W), BF16),
        scratch_shapes=_attn_scratch(),
        compiler_params=_cparams(("arbitrary", "arbitrary", "arbitrary")),
        name="mla_attention",
    )(q, k, va)


def _gelu_tanh(x):
    return 0.5 * x * (1.0 + jnp.tanh(0.7978845608028654 * (x + 0.044715 * x * x * x)))


def _gmlp_kernel(z_ref, gv_ref, ws_ref, bs_ref, o_ref):
    tm = z_ref.shape[0]
    ck = GMLP_CHUNK
    z = _gelu_tanh(z_ref[...].astype(F32))
    row = lax.broadcasted_iota(I32, (ck, ck), 0)
    colm = lax.broadcasted_iota(I32, (ck, ck), 1)
    for g in range(N_HEADS):
        lanes = slice(g * HEAD_DIM, (g + 1) * HEAD_DIM)
        u = z[:, lanes]
        vv = z[:, MIX_W + g * HEAD_DIM:MIX_W + (g + 1) * HEAD_DIM]
        ms = jnp.mean(vv * vv, axis=-1, keepdims=True)
        vn = (vv * lax.rsqrt(ms + EPS) * gv_ref[g:g + 1, :]).astype(BF16)
        w = jnp.where(colm <= row, ws_ref[g], 0.0).astype(BF16)
        bias = bs_ref[g]
        for c in range(tm // ck):
            rows = slice(c * ck, (c + 1) * ck)
            mixed = _dot(w, vn[rows]) + bias
            o_ref[rows, lanes] = (u[rows] * mixed).astype(o_ref.dtype)


def _spatial_gating(zb, g_v, w_s, b_s, tm=512):
    t = zb.shape[0]
    ck = GMLP_CHUNK
    return pl.pallas_call(
        _gmlp_kernel,
        grid=(t // tm,),
        in_specs=[pl.BlockSpec((tm, ZB_W), lambda i: (i, 0)),
                  pl.BlockSpec((N_HEADS, HEAD_DIM), lambda i: (0, 0)),
                  pl.BlockSpec((N_HEADS, ck, ck), lambda i: (0, 0, 0)),
                  pl.BlockSpec((N_HEADS, ck, 1), lambda i: (0, 0, 0))],
        out_specs=pl.BlockSpec((tm, MIX_W), lambda i: (i, 0)),
        out_shape=jax.ShapeDtypeStruct((t, MIX_W), BF16),
        compiler_params=_cparams(("arbitrary",)),
        name="spatial_gating",
    )(zb, g_v, w_s, b_s.reshape(N_HEADS, ck, 1))


def _mla_prep_kernel(z_ref, pos_ref, invf_ref, gq_ref, gkv_ref, wq_ref, wk_ref, wv_ref,
                     q_ref, k_ref, v_ref):
    tm = z_ref.shape[0]
    z = z_ref[...].astype(F32)
    cq = z[:, :CQ_PAD]
    ckv = z[:, CQ_PAD:CQ_PAD + CKV_PAD]
    kr = z[:, CQ_PAD + CKV_PAD:]
    qn = cq * lax.rsqrt(jnp.sum(cq * cq, -1, keepdims=True) * (1.0 / MLA_Q_RANK) + EPS)
    qn = (qn * gq_ref[...]).astype(BF16)
    kvn = ckv * lax.rsqrt(jnp.sum(ckv * ckv, -1, keepdims=True) * (1.0 / MLA_KV_RANK) + EPS)
    kvn = (kvn * gkv_ref[...]).astype(BF16)

    ang = pos_ref[...].astype(F32) * invf_ref[...]
    lane = lax.broadcasted_iota(I32, ang.shape, 1)
    half = MLA_ROPE // 2
    cos = jnp.cos(ang)
    sin = jnp.sin(ang)
    sin_lo = jnp.where(lane < half, -sin, 0.0)
    sin_hi = jnp.where((lane >= half) & (lane < 2 * half), sin, 0.0)

    def rope(r):
        return (r * cos + pltpu.roll(r, KR_PAD - half, 1) * sin_lo
                + pltpu.roll(r, half, 1) * sin_hi)

    q = _dot(qn, wq_ref[...])
    kn = _dot(kvn, wk_ref[...])
    v = _dot(kvn, wv_ref[...]).astype(BF16)
    k_rope = rope(kr)
    scale = MLA_QK ** -0.5 * LOG2E
    ones = jnp.ones((tm, HEAD_DIM), BF16)
    for h in range(N_HEADS):
        a = h * ATT_W
        b = a + MLA_NOPE
        q_ref[:, a:b] = (q[:, a:b] * scale).astype(q_ref.dtype)
        q_ref[:, b:a + ATT_W] = (rope(q[:, b:a + ATT_W]) * scale).astype(q_ref.dtype)
        k_ref[:, a:b] = kn[:, a:b].astype(k_ref.dtype)
        k_ref[:, b:a + ATT_W] = k_rope.astype(k_ref.dtype)
        v_ref[:, a:b] = v[:, h * HEAD_DIM:(h + 1) * HEAD_DIM]
        v_ref[:, b:a + ATT_W] = ones


def _mla_prep(zc, pos_col, invf, gq, gkv, wq, wk, wv, tm=512):
    t = zc.shape[0]
    qk_w = N_HEADS * ATT_W
    full = lambda i: (0, 0)
    out_spec = pl.BlockSpec((tm, qk_w), lambda i: (i, 0))
    out_shape = jax.ShapeDtypeStruct((t, qk_w), BF16)
    return pl.pallas_call(
        _mla_prep_kernel,
        grid=(t // tm,),
        in_specs=[pl.BlockSpec((tm, ZC_W), lambda i: (i, 0)),
                  pl.BlockSpec((tm, 1), lambda i: (i, 0)),
                  pl.BlockSpec((1, KR_PAD), full),
                  pl.BlockSpec((1, CQ_PAD), full),
                  pl.BlockSpec((1, CKV_PAD), full),
                  pl.BlockSpec((CQ_PAD, qk_w), full),
                  pl.BlockSpec((CKV_PAD, qk_w), full),
                  pl.BlockSpec((CKV_PAD, MIX_W), full)],
        out_specs=[out_spec, out_spec, out_spec],
        out_shape=[out_shape, out_shape, out_shape],
        compiler_params=_cparams(("arbitrary",)),
        name="mla_prep",
    )(zc, pos_col, invf, gq, gkv, wq, wk, wv)


def _glu(z):
    z = z.astype(F32)
    return z[:, :MIX_W] * jax.nn.sigmoid(z[:, MIX_W:])


def _conv_kernel(z_ref, zprev_ref, w_ref, b_ref, g_ref, beta_ref, o_ref, ybuf_ref, ysh_ref, *, per_seq):
    tm = z_ref.shape[0]
    first = (pl.program_id(0) % per_seq) == 0
    ybuf_ref[0:CONV_HALO, :] = jnp.where(first, 0.0, _glu(zprev_ref[...]))
    ybuf_ref[CONV_HALO:, :] = _glu(z_ref[...])
    span = tm + CONV_HALO - SUBLANES
    for o in range(1, SUBLANES):
        ysh_ref[o - 1, :, :] = ybuf_ref[pl.ds(o, span), :]
    rows = 64
    shift = CONV_HALO - (CONV_WIDTH - 1)
    for r in range(tm // rows):
        acc = jnp.zeros((rows, MIX_W), F32) + b_ref[...]
        for j in range(CONV_WIDTH):
            o = (shift + j) % SUBLANES
            base = r * rows + shift + j - o
            tap = ybuf_ref[pl.ds(base, rows), :] if o == 0 else ysh_ref[o - 1, pl.ds(base, rows), :]
            acc = acc + w_ref[j:j + 1, :] * tap
        mu = jnp.mean(acc, axis=-1, keepdims=True)
        xc = acc - mu
        y = xc * lax.rsqrt(jnp.mean(xc * xc, axis=-1, keepdims=True) + EPS)
        y = y * g_ref[...] + beta_ref[...]
        o_ref[r * rows:(r + 1) * rows, :] = (y * jax.nn.sigmoid(y)).astype(o_ref.dtype)


def _conformer_conv(zd, w_dw, b_dw, g_ln, b_ln, seq, tm=512):
    t = zd.shape[0]
    per_seq = seq // tm
    halo_blocks = tm // CONV_HALO
    full = lambda i: (0, 0)
    return pl.pallas_call(
        functools.partial(_conv_kernel, per_seq=per_seq),
        grid=(t // tm,),
        in_specs=[pl.BlockSpec((tm, ZD_W), lambda i: (i, 0)),
                  pl.BlockSpec((CONV_HALO, ZD_W),
                               lambda i: (jnp.maximum(i * halo_blocks - 1, 0), 0)),
                  pl.BlockSpec((CONV_WIDTH, MIX_W), full),
                  pl.BlockSpec((1, MIX_W), full),
                  pl.BlockSpec((1, MIX_W), full),
                  pl.BlockSpec((1, MIX_W), full)],
        out_specs=pl.BlockSpec((tm, MIX_W), lambda i: (i, 0)),
        out_shape=jax.ShapeDtypeStruct((t, MIX_W), BF16),
        scratch_shapes=[pltpu.VMEM((tm + CONV_HALO, MIX_W), F32),
                        pltpu.VMEM((SUBLANES - 1, tm + CONV_HALO - SUBLANES, MIX_W), F32)],
        compiler_params=_cparams(("arbitrary",)),
        name="conformer_conv",
    )(zd, zd, w_dw, b_dw, g_ln, b_ln)


def _outproj_kernel(x_ref, ya_ref, yb_ref, yc_ref, yd_ref, wo_ref, gt1_ref, g_ref, sc_ref, sh_ref,
                    gt2_ref, wr_ref, br_ref, wsg_ref, wsu_ref, wsd_ref,
                    xo_ref, xs_ref, idx_ref, wts_ref, rank_ref, cnt_ref,
                    carry_ref, hp_ref, destv_ref, dests_ref, row_sem, tab_sem, *, n_tok, n_steps):
    tm = x_ref.shape[0]
    i = pl.program_id(0)
    slot = lax.rem(i, 2)

    def drain(s):
        def body(t, carry):
            for k in range(TOP_K):
                _row_copy(hp_ref.at[s], t, xs_ref, 0, row_sem.at[s]).wait()
            return carry
        lax.fori_loop(0, tm, body, 0)

    @pl.when(i == 0)
    def _():
        carry_ref[...] = jnp.zeros_like(carry_ref)

    for s in range(2):
        @pl.when((i >= 2) & (slot == s))
        def _():
            drain(s)

    y = _dot(ya_ref[...], wo_ref[0:MIX_W, :])
    y = y + _dot(yb_ref[...], wo_ref[MIX_W:2 * MIX_W, :])
    y = y + _dot(yc_ref[...], wo_ref[2 * MIX_W:3 * MIX_W, :])
    y = y + _dot(yd_ref[...], wo_ref[3 * MIX_W:, :])
    x = x_ref[...] + gt1_ref[...] * y
    ms = jnp.mean(x * x, axis=-1, keepdims=True)
    h = (x * lax.rsqrt(ms + EPS) * g_ref[...]) * (1.0 + sc_ref[...]) + sh_ref[...]
    hb = h.astype(BF16)
    hp_ref[slot] = _pack_rows(h)

    act = _dot(hb, wsg_ref[...])
    act = (act * jax.nn.sigmoid(act) * _dot(hb, wsu_ref[...])).astype(BF16)
    xo_ref[...] = x + gt2_ref[...] * _dot(act, wsd_ref[...])

    logits = _dot_nt(wr_ref[...], h, precision=lax.Precision.HIGHEST)
    scores = jax.nn.sigmoid(logits)
    cur = scores + br_ref[...]
    erow = lax.broadcasted_iota(I32, (N_EXPERTS, tm), 0).astype(F32)
    picked = jnp.zeros((N_EXPERTS, tm), F32)
    hits, idxs, wts = [], [], []
    for _ in range(TOP_K):
        cmax = jnp.max(cur, axis=0, keepdims=True)
        first = jnp.min(jnp.where(cur == cmax, erow, float(N_EXPERTS)), axis=0, keepdims=True)
        hit = erow == first
        hits.append(hit)
        idxs.append(first)
        wts.append(jnp.sum(jnp.where(hit, scores, 0.0), axis=0, keepdims=True))
        picked = jnp.where(hit, 1.0, picked)
        cur = jnp.where(hit, -jnp.inf, cur)
    wsum = wts[0]
    for w in wts[1:]:
        wsum = wsum + w
    ti = lax.broadcasted_iota(I32, (tm, tm), 0)
    tj = lax.broadcasted_iota(I32, (tm, tm), 1)
    before = jnp.where(ti < tj, 1.0, 0.0).astype(BF16)
    prior = _dot(picked.astype(BF16), before) + carry_ref[...]
    for k in range(TOP_K):
        expert = idxs[k].astype(I32)
        rank = jnp.sum(jnp.where(hits[k], prior, 0.0), axis=0, keepdims=True).astype(I32)
        idx_ref[k:k + 1, :] = expert
        wts_ref[k:k + 1, :] = wts[k] / wsum * ROUTED_SCALE
        rank_ref[k:k + 1, :] = rank
        destv_ref[k:k + 1, :] = expert * n_tok + rank
    carry_ref[...] = carry_ref[...] + jnp.sum(picked, axis=1, keepdims=True)
    cnt_ref[...] = jnp.broadcast_to(carry_ref[...], cnt_ref.shape).astype(I32)

    table = pltpu.make_async_copy(destv_ref, dests_ref, tab_sem)
    table.start()
    table.wait()

    def issue(t, carry):
        for k in range(TOP_K):
            _row_copy(hp_ref.at[slot], t, xs_ref, dests_ref[k, t], row_sem.at[slot]).start(priority=k % 2)
        return carry
    lax.fori_loop(0, tm, issue, 0)

    @pl.when(i == n_steps - 1)
    def _():
        for s in range(min(2, n_steps)):
            drain(s)


def _out_projection(x, ys, w_out, gt1, g_ffn, sc2, sh2, gt2, w_rt, b_r, wsg, wsu, wsd, seq, tm=512):
    t, d = x.shape
    per_seq = seq // tm
    full = lambda i: (0, 0)
    mod_spec = pl.BlockSpec((None, 1, d), lambda i: (i // per_seq, 0, 0))
    y_spec = pl.BlockSpec((tm, MIX_W), lambda i: (i, 0))
    tok_spec = pl.BlockSpec((TOP_K, tm), lambda i: (0, i))
    return pl.pallas_call(
        functools.partial(_outproj_kernel, n_tok=t, n_steps=t // tm),
        grid=(t // tm,),
        in_specs=[pl.BlockSpec((tm, d), lambda i: (i, 0)),
                  y_spec, y_spec, y_spec, y_spec,
                  _resident((4 * MIX_W, d), full),
                  mod_spec,
                  pl.BlockSpec((1, d), full),
                  mod_spec, mod_spec, mod_spec,
                  pl.BlockSpec((N_EXPERTS, d), full),
                  pl.BlockSpec((N_EXPERTS, 1), full),
                  _resident((d, D_SHARED), full),
                  _resident((d, D_SHARED), full),
                  _resident((D_SHARED, d), full)],
        out_specs=[pl.BlockSpec((tm, d), lambda i: (i, 0)),
                   pl.BlockSpec(memory_space=pl.ANY),
                   tok_spec, tok_spec, tok_spec,
                   pl.BlockSpec((N_EXPERTS, 128), full)],
        out_shape=[jax.ShapeDtypeStruct((t, d), F32),
                   jax.ShapeDtypeStruct((N_EXPERTS * t, PACK_W), U32),
                   jax.ShapeDtypeStruct((TOP_K, t), I32),
                   jax.ShapeDtypeStruct((TOP_K, t), F32),
                   jax.ShapeDtypeStruct((TOP_K, t), I32),
                   jax.ShapeDtypeStruct((N_EXPERTS, 128), I32)],
        scratch_shapes=[pltpu.VMEM((N_EXPERTS, 1), F32),
                        pltpu.VMEM((2, tm, PACK_W), U32),
                        pltpu.VMEM((TOP_K, tm), I32),
                        pltpu.SMEM((TOP_K, tm), I32),
                        pltpu.SemaphoreType.DMA((2,)),
                        pltpu.SemaphoreType.DMA(())],
        compiler_params=_cparams(("arbitrary",)),
        name="out_projection_router",
    )(x, *ys, w_out, gt1, g_ffn, sc2, sh2, gt2, w_rt, b_r, wsg, wsu, wsd)


def _row_copy(src_ref, src_row, dst_ref, dst_row, sem):
    return pltpu.make_async_copy(src_ref.at[pl.ds(src_row, 1)], dst_ref.at[pl.ds(dst_row, 1)], sem)


def _expert_kernel(be_ref, bi_ref, bv_ref, x_ref, wg_ref, wu_ref, wd_ref, y_ref,
                   wgb_ref, wub_ref, wdb_ref):
    i = pl.program_id(0)
    expert = be_ref[i]
    prev = be_ref[jnp.maximum(i - 1, 0)]

    @pl.when((i == 0) | (expert != prev))
    def _():
        wgb_ref[...] = wg_ref[...].astype(BF16)
        wub_ref[...] = wu_ref[...].astype(BF16)
        wdb_ref[...] = wd_ref[...].astype(BF16)

    valid = bv_ref[i]

    @pl.when(valid > 0)
    def _():
        lo, hi = _unpack_rows(x_ref[...])
        rows = lax.broadcasted_iota(I32, (x_ref.shape[0], 1), 0)
        xb = jnp.where(rows < valid, jnp.concatenate([lo, hi], axis=1), 0.0).astype(BF16)
        gate = _dot(xb, wgb_ref[...])
        act = (gate * jax.nn.sigmoid(gate) * _dot(xb, wub_ref[...])).astype(BF16)
        y_ref[...] = _pack_rows(_dot(act, wdb_ref[...]))

    @pl.when(valid == 0)
    def _():
        y_ref[...] = jnp.zeros_like(y_ref)


def _experts(block_e, block_i, block_valid, xs, w_gate, w_up, w_down, layer, p_rows):
    d = D_MODEL
    per_expert = xs.shape[0] // N_EXPERTS // MOE_BLOCK
    w_map = lambda i, be, bi, bv: (layer, be[i], 0, 0)
    return pl.pallas_call(
        _expert_kernel,
        grid_spec=pltpu.PrefetchScalarGridSpec(
            num_scalar_prefetch=3,
            grid=(p_rows // MOE_BLOCK,),
            in_specs=[pl.BlockSpec((MOE_BLOCK, PACK_W),
                                   lambda i, be, bi, bv: (be[i] * per_expert + bi[i], 0)),
                      pl.BlockSpec((None, None, d, D_EXPERT), w_map),
                      pl.BlockSpec((None, None, d, D_EXPERT), w_map),
                      pl.BlockSpec((None, None, D_EXPERT, d), w_map)],
            out_specs=pl.BlockSpec((MOE_BLOCK, PACK_W), lambda i, be, bi, bv: (i, 0)),
            scratch_shapes=[pltpu.VMEM((d, D_EXPERT), BF16),
                            pltpu.VMEM((d, D_EXPERT), BF16),
                            pltpu.VMEM((D_EXPERT, d), BF16)]),
        out_shape=jax.ShapeDtypeStruct((p_rows, PACK_W), U32),
        compiler_params=_cparams(("arbitrary",)),
        name="moe_experts",
    )(block_e, block_i, block_valid, xs, w_gate, w_up, w_down)


def _combine_kernel(dest_ref, wts_ref, x_ref, gt2_ref, gf_ref, ys_ref, o_ref, buf_ref, sem, *, final):
    tm = x_ref.shape[0]

    def start(t, carry):
        for k in range(TOP_K):
            _row_copy(ys_ref, dest_ref[t * TOP_K + k], buf_ref.at[k], t, sem).start(priority=k % 2)
        return carry
    lax.fori_loop(0, tm, start, 0)

    def wait(t, carry):
        for k in range(TOP_K):
            _row_copy(ys_ref, dest_ref[t * TOP_K + k], buf_ref.at[k], t, sem).wait()
        return carry
    lax.fori_loop(0, tm, wait, 0)

    acc_lo = jnp.zeros((tm, PACK_W), F32)
    acc_hi = jnp.zeros((tm, PACK_W), F32)
    for k in range(TOP_K):
        lo, hi = _unpack_rows(buf_ref[k])
        w = wts_ref[:, k:k + 1]
        acc_lo = acc_lo + w * lo
        acc_hi = acc_hi + w * hi
    x = x_ref[...] + gt2_ref[...] * jnp.concatenate([acc_lo, acc_hi], axis=1)
    if final:
        ms = jnp.mean(x * x, axis=-1, keepdims=True)
        x = x * lax.rsqrt(ms + EPS) * gf_ref[...]
    o_ref[...] = x


def _combine(dest, wts, x_mid, gt2, g_final, ys, seq, final, tm=256):
    t, d = x_mid.shape
    per_seq = seq // tm
    return pl.pallas_call(
        functools.partial(_combine_kernel, final=final),
        grid=(t // tm,),
        in_specs=[pl.BlockSpec((tm * TOP_K,), lambda i: (i,), memory_space=pltpu.SMEM),
                  pl.BlockSpec((tm, TOP_K), lambda i: (i, 0)),
                  pl.BlockSpec((tm, d), lambda i: (i, 0)),
                  pl.BlockSpec((None, 1, d), lambda i: (i // per_seq, 0, 0)),
                  pl.BlockSpec((1, d), lambda i: (0, 0)),
                  pl.BlockSpec(memory_space=pl.ANY)],
        out_specs=pl.BlockSpec((tm, d), lambda i: (i, 0)),
        out_shape=jax.ShapeDtypeStruct((t, d), F32),
        scratch_shapes=[pltpu.VMEM((TOP_K, tm, PACK_W), U32),
                        pltpu.SemaphoreType.DMA(())],
        compiler_params=_cparams(("arbitrary",)),
        name="moe_combine",
    )(dest, wts, x_mid, gt2, g_final, ys)


def _pad_to(a, axis, size):
    pad = [(0, 0)] * a.ndim
    pad[axis] = (0, size - a.shape[axis])
    return jnp.pad(a, pad)


def _in_weight(w_in):
    off_b = ZA_W
    off_cq = off_b + ZB_W
    off_ckv = off_cq + MLA_Q_RANK
    off_kr = off_ckv + MLA_KV_RANK
    off_d = off_kr + MLA_ROPE
    parts = [w_in[:, :off_cq],
             _pad_to(w_in[:, off_cq:off_ckv], 1, CQ_PAD),
             _pad_to(w_in[:, off_ckv:off_kr], 1, CKV_PAD),
             _pad_to(w_in[:, off_kr:off_d], 1, KR_PAD),
             w_in[:, off_d:]]
    return jnp.concatenate(parts, axis=1).astype(BF16)


def _mla_weights(w_uq, w_ukv):
    wq = w_uq.reshape(MLA_Q_RANK, N_HEADS, MLA_QK)
    wq = _pad_to(_pad_to(wq, 2, ATT_W), 0, CQ_PAD).reshape(CQ_PAD, N_HEADS * ATT_W)
    wkv = _pad_to(w_ukv.reshape(MLA_KV_RANK, N_HEADS, MLA_NOPE + HEAD_DIM), 0, CKV_PAD)
    wk = _pad_to(wkv[:, :, :MLA_NOPE], 2, ATT_W).reshape(CKV_PAD, N_HEADS * ATT_W)
    wv = wkv[:, :, MLA_NOPE:].reshape(CKV_PAD, MIX_W)
    return wq.astype(BF16), wk.astype(BF16), wv.astype(BF16)


def _lookup(table, idx):
    ids = jnp.arange(table.shape[0], dtype=I32).reshape((-1,) + (1,) * idx.ndim)
    vals = table.reshape(ids.shape)
    return jnp.sum(jnp.where(idx[None] == ids, vals, 0), axis=0)


def _route_tables(idx_t, rank_t, counts, p_rows, n_tok):
    m = MOE_BLOCK
    nblk = (counts + m - 1) // m
    bend = jnp.cumsum(nblk)
    bstart = bend - nblk
    blk = jnp.arange(p_rows // m, dtype=I32)
    block_e = jnp.minimum(jnp.sum(bend[None, :] <= blk[:, None], axis=1), N_EXPERTS - 1).astype(I32)
    block_i = blk - _lookup(bstart, block_e)
    block_valid = jnp.clip(_lookup(counts, block_e) - block_i * m, 0, m).astype(I32)
    block_i = jnp.clip(block_i, 0, n_tok // m - 1).astype(I32)
    dest_t = _lookup(bstart * m, idx_t) + rank_t
    dest = dest_t.astype(I32).T.reshape(-1)
    return dest, block_e, block_i, block_valid


def kernel(x, c, positions, w_mod, b_mod, g_attn, w_in, g_gmlp_v, w_spatial, b_spatial, g_mla_q, w_mla_uq, g_mla_kv, w_mla_ukv, w_conv_dw, b_conv_dw, g_conv_ln, b_conv_ln, w_out, g_ffn, w_router, b_router, w_exp_gate, w_exp_up, w_exp_down, w_sh_gate, w_sh_up, w_sh_down, g_final):
    batch, seq, d = x.shape
    n_layers = w_mod.shape[0]
    t = batch * seq
    p_rows = t * TOP_K + N_EXPERTS * MOE_BLOCK

    xf = x.reshape(t, d)
    pos_col = positions.reshape(t, 1)
    pos_chunks = positions.reshape(batch, seq // ATT_TK, ATT_TK)
    slopes = 2.0 ** (-8.0 * jnp.arange(1, N_HEADS + 1, dtype=F32) / N_HEADS)
    half = MLA_ROPE // 2
    inv_freq = ROPE_THETA ** (-jnp.arange(half, dtype=F32) * (2.0 / MLA_ROPE))
    invf = _pad_to(jnp.concatenate([inv_freq, inv_freq]), 0, KR_PAD).reshape(1, KR_PAD)

    mod = _modulation(c, w_mod, b_mod).reshape(n_layers, N_MOD, batch, 1, d)

    routed = None
    for l in range(n_layers):
        sh1, sc1, gt1, sh2, sc2, gt2 = (mod[l, j] for j in range(N_MOD))
        if routed is not None:
            xf = _combine(*routed, seq=seq, final=False)

        qm, ka, va, zb, zc, zd = _in_projection(xf, g_attn[l].reshape(1, d), sc1, sh1,
                                                _in_weight(w_in[l]), seq)
        y_a = _moba_attention(qm, ka, va, pos_col, pos_chunks, slopes, batch, seq)
        y_b = _spatial_gating(zb, g_gmlp_v[l], w_spatial[l], b_spatial[l])
        wq, wk, wv = _mla_weights(w_mla_uq[l], w_mla_ukv[l])
        q, k, v = _mla_prep(zc, pos_col, invf,
                            _pad_to(g_mla_q[l], 0, CQ_PAD).reshape(1, CQ_PAD),
                            _pad_to(g_mla_kv[l], 0, CKV_PAD).reshape(1, CKV_PAD), wq, wk, wv)
        y_c = _mla_attention(q, k, v, batch, seq)
        y_d = _conformer_conv(zd, w_conv_dw[l].reshape(CONV_WIDTH, MIX_W),
                              b_conv_dw[l].reshape(1, MIX_W), g_conv_ln[l].reshape(1, MIX_W),
                              b_conv_ln[l].reshape(1, MIX_W), seq)

        x_mid, xs, idx_t, wts_t, rank_t, cnt = _out_projection(
            xf, (y_a, y_b, y_c, y_d), w_out[l].astype(BF16), gt1, g_ffn[l].reshape(1, d),
            sc2, sh2, gt2, w_router[l].T, b_router[l].reshape(N_EXPERTS, 1),
            w_sh_gate[l].astype(BF16), w_sh_up[l].astype(BF16), w_sh_down[l].astype(BF16), seq)

        dest, block_e, block_i, block_valid = _route_tables(idx_t, rank_t, cnt[:, 0], p_rows, t)
        ys = _experts(block_e, block_i, block_valid, xs, w_exp_gate, w_exp_up, w_exp_down, l, p_rows)
        routed = (dest, wts_t.T, x_mid, gt2, g_final.reshape(1, d), ys)

    out = _combine(*routed, seq=seq, final=True)
    return out.reshape(batch, seq, d)
```

```python
import functools

import jax
import jax.numpy as jnp
from jax import lax
from jax.experimental import pallas as pl
from jax.experimental.pallas import tpu as pltpu

F32 = jnp.float32
BF16 = jnp.bfloat16
I32 = jnp.int32
U32 = jnp.uint32

D_MODEL = 2048
HEAD_DIM = 128
N_HEADS = 4
MIX_W = N_HEADS * HEAD_DIM
MOBA_BLOCK = 256
MOBA_TOPK = 3
GMLP_CHUNK = 128
MLA_Q_RANK = 448
MLA_KV_RANK = 160
MLA_NOPE = 128
MLA_ROPE = 64
MLA_QK = MLA_NOPE + MLA_ROPE
ROPE_THETA = 10000.0
CONV_WIDTH = 31
CONV_HALO = 32
SUBLANES = 8
N_EXPERTS = 64
TOP_K = 8
D_EXPERT = 512
D_SHARED = 512
ROUTED_SCALE = 2.5
MOE_BLOCK = 512
N_MOD = 6
EPS = 1e-6
NEG = -1e30
LOG2E = 1.4426950408889634

ATT_TQ = 2048
ATT_SUB = 256
ATT_TK = 1024
ATT_W = 2 * HEAD_DIM

ZA_W = 3 * MIX_W
ZB_W = 2 * MIX_W
CQ_PAD = 512
CKV_PAD = 256
KR_PAD = 128
ZC_W = CQ_PAD + CKV_PAD + KR_PAD
ZD_W = 2 * MIX_W

VMEM_LIMIT = 56 * 1024 * 1024
PACK_W = D_MODEL // 2


def _cparams(sem):
    return pltpu.CompilerParams(dimension_semantics=sem, vmem_limit_bytes=VMEM_LIMIT)


def _resident(shape, index_map):
    return pl.BlockSpec(shape, index_map, pipeline_mode=pl.Buffered(1))


def _dot(a, b):
    return jnp.dot(a, b, preferred_element_type=F32)


def _dot_nt(a, b, precision=None):
    return lax.dot_general(a, b, (((1,), (1,)), ((), ())), preferred_element_type=F32,
                           precision=precision)


def _pack_rows(y):
    half = y.shape[1] // 2
    lo = lax.bitcast_convert_type(y[:, :half].astype(BF16).astype(F32), U32)
    hi = lax.bitcast_convert_type(y[:, half:].astype(BF16).astype(F32), U32)
    return (lo >> 16) | hi


def _unpack_rows(u):
    lo = lax.bitcast_convert_type(u << 16, F32)
    hi = lax.bitcast_convert_type(u & jnp.uint32(0xFFFF0000), F32)
    return lo, hi


def _mod_kernel(c_ref, w_ref, b_ref, o_ref):
    c = c_ref[...]
    a = c * jax.nn.sigmoid(c)
    o_ref[...] = jnp.dot(a, w_ref[...], preferred_element_type=F32,
                         precision=lax.Precision.HIGHEST) + b_ref[...]


def _modulation(c, w_mod, b_mod):
    n_layers, d, _ = w_mod.shape
    b = c.shape[0]
    return pl.pallas_call(
        _mod_kernel,
        grid=(n_layers, N_MOD),
        in_specs=[pl.BlockSpec((b, d), lambda l, j: (0, 0)),
                  pl.BlockSpec((None, d, d), lambda l, j: (l, 0, j)),
                  pl.BlockSpec((None, None, 1, d), lambda l, j: (l, j, 0, 0))],
        out_specs=pl.BlockSpec((None, None, b, d), lambda l, j: (l, j, 0, 0)),
        out_shape=jax.ShapeDtypeStruct((n_layers, N_MOD, b, d), F32),
        compiler_params=_cparams(("arbitrary", "arbitrary")),
        name="modulation",
    )(c, w_mod, b_mod.reshape(n_layers, N_MOD, 1, d))


def _inproj_kernel(x_ref, g_ref, sc_ref, sh_ref, w_ref, q_ref, ka_ref, va_ref, zb_ref, zc_ref, zd_ref,
                   *, seq):
    tm = x_ref.shape[0]
    x = x_ref[...]
    ms = jnp.mean(x * x, axis=-1, keepdims=True)
    h = (x * lax.rsqrt(ms + EPS) * g_ref[...]) * (1.0 + sc_ref[...]) + sh_ref[...]
    hb = h.astype(BF16)

    q_ref[...] = _dot(hb, w_ref[:, 0:MIX_W]).astype(BF16)
    pos = (pl.program_id(0) * tm) % seq + lax.broadcasted_iota(I32, (tm, HEAD_DIM), 0)
    lane = lax.broadcasted_iota(I32, (tm, HEAD_DIM), 1)
    onehot = jnp.where(pos // MOBA_BLOCK == lane, 1.0, 0.0).astype(BF16)
    ones = jnp.ones((tm, HEAD_DIM), BF16)
    k = _dot(hb, w_ref[:, MIX_W:2 * MIX_W]).astype(BF16)
    v = _dot(hb, w_ref[:, 2 * MIX_W:ZA_W]).astype(BF16)
    for hd in range(N_HEADS):
        src = slice(hd * HEAD_DIM, (hd + 1) * HEAD_DIM)
        ka_ref[:, hd * ATT_W:hd * ATT_W + HEAD_DIM] = k[:, src]
        ka_ref[:, hd * ATT_W + HEAD_DIM:(hd + 1) * ATT_W] = onehot
        va_ref[:, hd * ATT_W:hd * ATT_W + HEAD_DIM] = v[:, src]
        va_ref[:, hd * ATT_W + HEAD_DIM:(hd + 1) * ATT_W] = ones

    off = ZA_W
    for ref in (zb_ref, zc_ref, zd_ref):
        w = ref.shape[1]
        ref[...] = _dot(hb, w_ref[:, off:off + w]).astype(BF16)
        off += w


def _in_projection(x, g, sc, sh, w_in_p, seq, tm=512):
    t, d = x.shape
    per_seq = seq // tm
    widths = (MIX_W, N_HEADS * ATT_W, N_HEADS * ATT_W, ZB_W, ZC_W, ZD_W)
    mod_spec = pl.BlockSpec((None, 1, d), lambda i: (i // per_seq, 0, 0))
    return pl.pallas_call(
        functools.partial(_inproj_kernel, seq=seq),
        grid=(t // tm,),
        in_specs=[pl.BlockSpec((tm, d), lambda i: (i, 0)),
                  pl.BlockSpec((1, d), lambda i: (0, 0)),
                  mod_spec, mod_spec,
                  _resident((d, ZA_W + ZB_W + ZC_W + ZD_W), lambda i: (0, 0))],
        out_specs=[pl.BlockSpec((tm, w), lambda i: (i, 0)) for w in widths],
        out_shape=[jax.ShapeDtypeStruct((t, w), BF16) for w in widths],
        compiler_params=_cparams(("arbitrary",)),
        name="in_projection",
    )(x, g, sc, sh, w_in_p)


def _attn_kernel(*refs, moba):
    if moba:
        slope_ref, q_ref, k_ref, v_ref, pq_ref, pk_ref, o_ref, kmean_ref, qa_ref, m_ref, acc_ref = refs
    else:
        q_ref, k_ref, v_ref, o_ref, m_ref, acc_ref = refs
        qa_ref = q_ref
    tq, sub, tk = ATT_TQ, ATT_SUB, ATT_TK
    nsub = tq // sub
    i = pl.program_id(2)

    if moba:
        nb_pad = kmean_ref.shape[0]
        nb = k_ref.shape[0] // MOBA_BLOCK

        @pl.when(i == 0)
        def _():
            kmean_ref[...] = jnp.zeros_like(kmean_ref)

            def body(n, carry):
                start = pl.multiple_of(n * MOBA_BLOCK, MOBA_BLOCK)
                kb = k_ref[pl.ds(start, MOBA_BLOCK), 0:HEAD_DIM].astype(F32)
                kmean_ref[pl.ds(n, 1), :] = jnp.mean(kb, axis=0, keepdims=True)
                return carry
            lax.fori_loop(0, nb, body, 0)

        slope = slope_ref[pl.program_id(1)] * LOG2E
        col = lax.broadcasted_iota(I32, (sub, nb_pad), 1)
        colf = col.astype(F32)
        pq_s = []
        for c in range(nsub):
            rows = slice(c * sub, (c + 1) * sub)
            own = i * nsub + c
            qf = q_ref[rows, :].astype(F32)
            gate = _dot_nt(qf, kmean_ref[...], precision=lax.Precision.HIGHEST)
            gate = jnp.where(col < own, gate, NEG)
            sel = col == own
            for _ in range(MOBA_TOPK):
                gmax = jnp.max(gate, axis=-1, keepdims=True)
                first = jnp.min(jnp.where(gate == gmax, colf, float(nb_pad)), axis=-1, keepdims=True)
                hit = colf == first
                sel = sel | (hit & (col < own))
                gate = jnp.where(hit, -jnp.inf, gate)
            qa_ref[rows, 0:HEAD_DIM] = (qf * (HEAD_DIM ** -0.5 * LOG2E)).astype(BF16)
            qa_ref[rows, HEAD_DIM:] = jnp.where(sel, 0.0, NEG).astype(BF16)
            pq_s.append(slope * pq_ref[rows, :].astype(F32))

    m_ref[...] = jnp.full(m_ref.shape, NEG, F32)
    acc_ref[...] = jnp.zeros_like(acc_ref)

    def step(c, chunk, mask_off):
        rows = slice(c * sub, (c + 1) * sub)
        start = pl.multiple_of(chunk * tk, tk)
        s = _dot_nt(qa_ref[rows, :], k_ref[pl.ds(start, tk), :])
        if moba:
            s = s - jnp.abs(pq_s[c] - slope * pk_ref[pl.ds(chunk, 1), :].astype(F32))
        if mask_off is not None:
            qi = lax.broadcasted_iota(I32, (sub, tk), 0)
            ki = lax.broadcasted_iota(I32, (sub, tk), 1)
            s = jnp.where(ki <= qi + mask_off, s, NEG)
        m_old = m_ref[rows, :]
        m_new = jnp.maximum(m_old, jnp.max(s, axis=-1, keepdims=True))
        p = jnp.exp2(s - m_new).astype(BF16)
        acc_ref[rows, :] = jnp.exp2(m_old - m_new) * acc_ref[rows, :] + _dot(p, v_ref[pl.ds(start, tk), :])
        m_ref[rows, :] = m_new

    per_tile = tq // tk
    for c in range(nsub):
        q_lo = c * sub
        for jj in reversed(range(per_tile)):
            k_lo = jj * tk
            if k_lo > q_lo + sub - 1:
                continue
            needs_mask = k_lo + tk - 1 > q_lo
            step(c, i * per_tile + jj, q_lo - k_lo if needs_mask else None)

    def body(j, carry):
        for c in range(nsub):
            step(c, j, None)
        return carry
    lax.fori_loop(0, i * per_tile, body, 0)

    acc = acc_ref[...]
    o_ref[...] = (acc[:, :HEAD_DIM] / acc[:, HEAD_DIM:]).astype(o_ref.dtype)


def _attn_scratch():
    return [pltpu.VMEM((ATT_TQ, 1), F32), pltpu.VMEM((ATT_TQ, ATT_W), F32)]


def _moba_attention(q, ka, va, pos_col, pos_chunks, slopes, batch, seq):
    t = q.shape[0]
    nq = seq // ATT_TQ
    kv_spec = pl.BlockSpec((seq, ATT_W), lambda b, h, i, s: (b, h))
    return pl.pallas_call(
        functools.partial(_attn_kernel, moba=True),
        grid_spec=pltpu.PrefetchScalarGridSpec(
            num_scalar_prefetch=1,
            grid=(batch, N_HEADS, nq),
            in_specs=[pl.BlockSpec((ATT_TQ, HEAD_DIM), lambda b, h, i, s: (b * nq + i, h)),
                      kv_spec, kv_spec,
                      pl.BlockSpec((ATT_TQ, 1), lambda b, h, i, s: (b * nq + i, 0)),
                      pl.BlockSpec((None, seq // ATT_TK, ATT_TK), lambda b, h, i, s: (b, 0, 0))],
            out_specs=pl.BlockSpec((ATT_TQ, HEAD_DIM), lambda b, h, i, s: (b * nq + i, h)),
            scratch_shapes=[pltpu.VMEM((HEAD_DIM, HEAD_DIM), F32),
                            pltpu.VMEM((ATT_TQ, ATT_W), BF16)] + _attn_scratch()),
        out_shape=jax.ShapeDtypeStruct((t, MIX_W), BF16),
        compiler_params=_cparams(("arbitrary", "arbitrary", "arbitrary")),
        name="moba_attention",
    )(slopes, q, ka, va, pos_col, pos_chunks)


def _mla_attention(q, k, va, batch, seq):
    t = q.shape[0]
    nq = seq // ATT_TQ
    kv_spec = pl.BlockSpec((seq, ATT_W), lambda b, h, i: (b, h))
    return pl.pallas_call(
        functools.partial(_attn_kernel, moba=False),
        grid=(batch, N_HEADS, nq),
        in_specs=[pl.BlockSpec((ATT_TQ, ATT_W), lambda b, h, i: (b * nq + i, h)),
                  kv_spec, kv_spec],
        out_specs=pl.BlockSpec((ATT_TQ, HEAD_DIM), lambda b, h, i: (b * nq + i, h)),
        out_shape=jax.ShapeDtypeStruct((t, MIX_W), BF16),
        scratch_shapes=_attn_scratch(),
        compiler_params=_cparams(("arbitrary", "arbitrary", "arbitrary")),
        name="mla_attention",
    )(q, k, va)


def _gelu_tanh(x):
    return 0.5 * x * (1.0 + jnp.tanh(0.7978845608028654 * (x + 0.044715 * x * x * x)))


def _gmlp_kernel(z_ref, gv_ref, ws_ref, bs_ref, o_ref):
    tm = z_ref.shape[0]
    ck = GMLP_CHUNK
    z = _gelu_tanh(z_ref[...].astype(F32))
    row = lax.broadcasted_iota(I32, (ck, ck), 0)
    colm = lax.broadcasted_iota(I32, (ck, ck), 1)
    for g in range(N_HEADS):
        lanes = slice(g * HEAD_DIM, (g + 1) * HEAD_DIM)
        u = z[:, lanes]
        vv = z[:, MIX_W + g * HEAD_DIM:MIX_W + (g + 1) * HEAD_DIM]
        ms = jnp.mean(vv * vv, axis=-1, keepdims=True)
        vn = (vv * lax.rsqrt(ms + EPS) * gv_ref[g:g + 1, :]).astype(BF16)
        w = jnp.where(colm <= row, ws_ref[g], 0.0).astype(BF16)
        bias = bs_ref[g]
        for c in range(tm // ck):
            rows = slice(c * ck, (c + 1) * ck)
            mixed = _dot(w, vn[rows]) + bias
            o_ref[rows, lanes] = (u[rows] * mixed).astype(o_ref.dtype)


def _spatial_gating(zb, g_v, w_s, b_s, tm=512):
    t = zb.shape[0]
    ck = GMLP_CHUNK
    return pl.pallas_call(
        _gmlp_kernel,
        grid=(t // tm,),
        in_specs=[pl.BlockSpec((tm, ZB_W), lambda i: (i, 0)),
                  pl.BlockSpec((N_HEADS, HEAD_DIM), lambda i: (0, 0)),
                  pl.BlockSpec((N_HEADS, ck, ck), lambda i: (0, 0, 0)),
                  pl.BlockSpec((N_HEADS, ck, 1), lambda i: (0, 0, 0))],
        out_specs=pl.BlockSpec((tm, MIX_W), lambda i: (i, 0)),
        out_shape=jax.ShapeDtypeStruct((t, MIX_W), BF16),
        compiler_params=_cparams(("arbitrary",)),
        name="spatial_gating",
    )(zb, g_v, w_s, b_s.reshape(N_HEADS, ck, 1))


def _mla_prep_kernel(z_ref, pos_ref, invf_ref, gq_ref, gkv_ref, wq_ref, wk_ref, wv_ref,
                     q_ref, k_ref, v_ref):
    tm = z_ref.shape[0]
    z = z_ref[...].astype(F32)
    cq = z[:, :CQ_PAD]
    ckv = z[:, CQ_PAD:CQ_PAD + CKV_PAD]
    kr = z[:, CQ_PAD + CKV_PAD:]
    qn = cq * lax.rsqrt(jnp.sum(cq * cq, -1, keepdims=True) * (1.0 / MLA_Q_RANK) + EPS)
    qn = (qn * gq_ref[...]).astype(BF16)
    kvn = ckv * lax.rsqrt(jnp.sum(ckv * ckv, -1, keepdims=True) * (1.0 / MLA_KV_RANK) + EPS)
    kvn = (kvn * gkv_ref[...]).astype(BF16)

    ang = pos_ref[...].astype(F32) * invf_ref[...]
    lane = lax.broadcasted_iota(I32, ang.shape, 1)
    half = MLA_ROPE // 2
    cos = jnp.cos(ang)
    sin = jnp.sin(ang)
    sin_lo = jnp.where(lane < half, -sin, 0.0)
    sin_hi = jnp.where((lane >= half) & (lane < 2 * half), sin, 0.0)

    def rope(r):
        return (r * cos + pltpu.roll(r, KR_PAD - half, 1) * sin_lo
                + pltpu.roll(r, half, 1) * sin_hi)

    q = _dot(qn, wq_ref[...])
    kn = _dot(kvn, wk_ref[...])
    v = _dot(kvn, wv_ref[...]).astype(BF16)
    k_rope = rope(kr)
    scale = MLA_QK ** -0.5 * LOG2E
    ones = jnp.ones((tm, HEAD_DIM), BF16)
    for h in range(N_HEADS):
        a = h * ATT_W
        b = a + MLA_NOPE
        q_ref[:, a:b] = (q[:, a:b] * scale).astype(q_ref.dtype)
        q_ref[:, b:a + ATT_W] = (rope(q[:, b:a + ATT_W]) * scale).astype(q_ref.dtype)
        k_ref[:, a:b] = kn[:, a:b].astype(k_ref.dtype)
        k_ref[:, b:a + ATT_W] = k_rope.astype(k_ref.dtype)
        v_ref[:, a:b] = v[:, h * HEAD_DIM:(h + 1) * HEAD_DIM]
        v_ref[:, b:a + ATT_W] = ones


def _mla_prep(zc, pos_col, invf, gq, gkv, wq, wk, wv, tm=512):
    t = zc.shape[0]
    qk_w = N_HEADS * ATT_W
    full = lambda i: (0, 0)
    out_spec = pl.BlockSpec((tm, qk_w), lambda i: (i, 0))
    out_shape = jax.ShapeDtypeStruct((t, qk_w), BF16)
    return pl.pallas_call(
        _mla_prep_kernel,
        grid=(t // tm,),
        in_specs=[pl.BlockSpec((tm, ZC_W), lambda i: (i, 0)),
                  pl.BlockSpec((tm, 1), lambda i: (i, 0)),
                  pl.BlockSpec((1, KR_PAD), full),
                  pl.BlockSpec((1, CQ_PAD), full),
                  pl.BlockSpec((1, CKV_PAD), full),
                  pl.BlockSpec((CQ_PAD, qk_w), full),
                  pl.BlockSpec((CKV_PAD, qk_w), full),
                  pl.BlockSpec((CKV_PAD, MIX_W), full)],
        out_specs=[out_spec, out_spec, out_spec],
        out_shape=[out_shape, out_shape, out_shape],
        compiler_params=_cparams(("arbitrary",)),
        name="mla_prep",
    )(zc, pos_col, invf, gq, gkv, wq, wk, wv)


def _glu(z):
    z = z.astype(F32)
    return z[:, :MIX_W] * jax.nn.sigmoid(z[:, MIX_W:])


def _conv_kernel(z_ref, zprev_ref, w_ref, b_ref, g_ref, beta_ref, o_ref, ybuf_ref, ysh_ref, *, per_seq):
    tm = z_ref.shape[0]
    first = (pl.program_id(0) % per_seq) == 0
    ybuf_ref[0:CONV_HALO, :] = jnp.where(first, 0.0, _glu(zprev_ref[...]))
    ybuf_ref[CONV_HALO:, :] = _glu(z_ref[...])
    span = tm + CONV_HALO - SUBLANES
    for o in range(1, SUBLANES):
        ysh_ref[o - 1, :, :] = ybuf_ref[pl.ds(o, span), :]
    rows = 64
    shift = CONV_HALO - (CONV_WIDTH - 1)
    for r in range(tm // rows):
        acc = jnp.zeros((rows, MIX_W), F32) + b_ref[...]
        for j in range(CONV_WIDTH):
            o = (shift + j) % SUBLANES
            base = r * rows + shift + j - o
            tap = ybuf_ref[pl.ds(base, rows), :] if o == 0 else ysh_ref[o - 1, pl.ds(base, rows), :]
            acc = acc + w_ref[j:j + 1, :] * tap
        mu = jnp.mean(acc, axis=-1, keepdims=True)
        xc = acc - mu
        y = xc * lax.rsqrt(jnp.mean(xc * xc, axis=-1, keepdims=True) + EPS)
        y = y * g_ref[...] + beta_ref[...]
        o_ref[r * rows:(r + 1) * rows, :] = (y * jax.nn.sigmoid(y)).astype(o_ref.dtype)


def _conformer_conv(zd, w_dw, b_dw, g_ln, b_ln, seq, tm=512):
    t = zd.shape[0]
    per_seq = seq // tm
    halo_blocks = tm // CONV_HALO
    full = lambda i: (0, 0)
    return pl.pallas_call(
        functools.partial(_conv_kernel, per_seq=per_seq),
        grid=(t // tm,),
        in_specs=[pl.BlockSpec((tm, ZD_W), lambda i: (i, 0)),
                  pl.BlockSpec((CONV_HALO, ZD_W),
                               lambda i: (jnp.maximum(i * halo_blocks - 1, 0), 0)),
                  pl.BlockSpec((CONV_WIDTH, MIX_W), full),
                  pl.BlockSpec((1, MIX_W), full),
                  pl.BlockSpec((1, MIX_W), full),
                  pl.BlockSpec((1, MIX_W), full)],
        out_specs=pl.BlockSpec((tm, MIX_W), lambda i: (i, 0)),
        out_shape=jax.ShapeDtypeStruct((t, MIX_W), BF16),
        scratch_shapes=[pltpu.VMEM((tm + CONV_HALO, MIX_W), F32),
                        pltpu.VMEM((SUBLANES - 1, tm + CONV_HALO - SUBLANES, MIX_W), F32)],
        compiler_params=_cparams(("arbitrary",)),
        name="conformer_conv",
    )(zd, zd, w_dw, b_dw, g_ln, b_ln)


def _outproj_kernel(x_ref, ya_ref, yb_ref, yc_ref, yd_ref, wo_ref, gt1_ref, g_ref, sc_ref, sh_ref,
                    gt2_ref, wr_ref, br_ref, wsg_ref, wsu_ref, wsd_ref,
                    xo_ref, hp_ref, idx_ref, wts_ref, rank_ref, cnt_ref, carry_ref):
    tm = x_ref.shape[0]

    @pl.when(pl.program_id(0) == 0)
    def _():
        carry_ref[...] = jnp.zeros_like(carry_ref)

    y = _dot(ya_ref[...], wo_ref[0:MIX_W, :])
    y = y + _dot(yb_ref[...], wo_ref[MIX_W:2 * MIX_W, :])
    y = y + _dot(yc_ref[...], wo_ref[2 * MIX_W:3 * MIX_W, :])
    y = y + _dot(yd_ref[...], wo_ref[3 * MIX_W:, :])
    x = x_ref[...] + gt1_ref[...] * y
    ms = jnp.mean(x * x, axis=-1, keepdims=True)
    h = (x * lax.rsqrt(ms + EPS) * g_ref[...]) * (1.0 + sc_ref[...]) + sh_ref[...]
    hb = h.astype(BF16)
    hp_ref[...] = _pack_rows(h)

    act = _dot(hb, wsg_ref[...])
    act = (act * jax.nn.sigmoid(act) * _dot(hb, wsu_ref[...])).astype(BF16)
    xo_ref[...] = x + gt2_ref[...] * _dot(act, wsd_ref[...])

    logits = _dot_nt(wr_ref[...], h, precision=lax.Precision.HIGHEST)
    scores = jax.nn.sigmoid(logits)
    cur = scores + br_ref[...]
    erow = lax.broadcasted_iota(I32, (N_EXPERTS, tm), 0).astype(F32)
    picked = jnp.zeros((N_EXPERTS, tm), F32)
    hits, idxs, wts = [], [], []
    for _ in range(TOP_K):
        cmax = jnp.max(cur, axis=0, keepdims=True)
        first = jnp.min(jnp.where(cur == cmax, erow, float(N_EXPERTS)), axis=0, keepdims=True)
        hit = erow == first
        hits.append(hit)
        idxs.append(first)
        wts.append(jnp.sum(jnp.where(hit, scores, 0.0), axis=0, keepdims=True))
        picked = jnp.where(hit, 1.0, picked)
        cur = jnp.where(hit, -jnp.inf, cur)
    wsum = wts[0]
    for w in wts[1:]:
        wsum = wsum + w
    ti = lax.broadcasted_iota(I32, (tm, tm), 0)
    tj = lax.broadcasted_iota(I32, (tm, tm), 1)
    before = jnp.where(ti < tj, 1.0, 0.0).astype(BF16)
    prior = _dot(picked.astype(BF16), before) + carry_ref[...]
    for k in range(TOP_K):
        idx_ref[k:k + 1, :] = idxs[k].astype(I32)
        wts_ref[k:k + 1, :] = wts[k] / wsum * ROUTED_SCALE
        rank_ref[k:k + 1, :] = jnp.sum(jnp.where(hits[k], prior, 0.0), axis=0,
                                       keepdims=True).astype(I32)
    carry_ref[...] = carry_ref[...] + jnp.sum(picked, axis=1, keepdims=True)
    cnt_ref[...] = jnp.broadcast_to(carry_ref[...], cnt_ref.shape).astype(I32)


def _out_projection(x, ys, w_out, gt1, g_ffn, sc2, sh2, gt2, w_rt, b_r, wsg, wsu, wsd, seq, tm=512):
    t, d = x.shape
    per_seq = seq // tm
    full = lambda i: (0, 0)
    mod_spec = pl.BlockSpec((None, 1, d), lambda i: (i // per_seq, 0, 0))
    y_spec = pl.BlockSpec((tm, MIX_W), lambda i: (i, 0))
    tok_spec = pl.BlockSpec((TOP_K, tm), lambda i: (0, i))
    return pl.pallas_call(
        _outproj_kernel,
        grid=(t // tm,),
        in_specs=[pl.BlockSpec((tm, d), lambda i: (i, 0)),
                  y_spec, y_spec, y_spec, y_spec,
                  _resident((4 * MIX_W, d), full),
                  mod_spec,
                  pl.BlockSpec((1, d), full),
                  mod_spec, mod_spec, mod_spec,
                  pl.BlockSpec((N_EXPERTS, d), full),
                  pl.BlockSpec((N_EXPERTS, 1), full),
                  _resident((d, D_SHARED), full),
                  _resident((d, D_SHARED), full),
                  _resident((D_SHARED, d), full)],
        out_specs=[pl.BlockSpec((tm, d), lambda i: (i, 0)),
                   pl.BlockSpec((tm, PACK_W), lambda i: (i, 0)),
                   tok_spec, tok_spec, tok_spec,
                   pl.BlockSpec((N_EXPERTS, 128), full)],
        out_shape=[jax.ShapeDtypeStruct((t, d), F32),
                   jax.ShapeDtypeStruct((t, PACK_W), U32),
                   jax.ShapeDtypeStruct((TOP_K, t), I32),
                   jax.ShapeDtypeStruct((TOP_K, t), F32),
                   jax.ShapeDtypeStruct((TOP_K, t), I32),
                   jax.ShapeDtypeStruct((N_EXPERTS, 128), I32)],
        scratch_shapes=[pltpu.VMEM((N_EXPERTS, 1), F32)],
        compiler_params=_cparams(("arbitrary",)),
        name="out_projection_router",
    )(x, *ys, w_out, gt1, g_ffn, sc2, sh2, gt2, w_rt, b_r, wsg, wsu, wsd)


def _row_copy(src_ref, src_row, dst_ref, dst_row, sem):
    return pltpu.make_async_copy(src_ref.at[pl.ds(src_row, 1)], dst_ref.at[pl.ds(dst_row, 1)], sem)


def _dispatch_kernel(dest_ref, hp_ref, xs_ref, sem):
    tm = hp_ref.shape[0]

    def start(t, carry):
        for k in range(TOP_K):
            _row_copy(hp_ref, t, xs_ref, dest_ref[t * TOP_K + k], sem).start(priority=k % 2)
        return carry
    lax.fori_loop(0, tm, start, 0)

    def wait(t, carry):
        for k in range(TOP_K):
            _row_copy(hp_ref, t, xs_ref, dest_ref[t * TOP_K + k], sem).wait()
        return carry
    lax.fori_loop(0, tm, wait, 0)


def _dispatch(dest_t, hp, p_rows, tm=512):
    t = hp.shape[0]
    return pl.pallas_call(
        _dispatch_kernel,
        grid=(t // tm,),
        in_specs=[pl.BlockSpec((tm * TOP_K,), lambda i: (i,), memory_space=pltpu.SMEM),
                  pl.BlockSpec((tm, PACK_W), lambda i: (i, 0))],
        out_specs=pl.BlockSpec(memory_space=pl.ANY),
        out_shape=jax.ShapeDtypeStruct((p_rows, PACK_W), U32),
        scratch_shapes=[pltpu.SemaphoreType.DMA(())],
        compiler_params=_cparams(("arbitrary",)),
        name="moe_dispatch",
    )(dest_t, hp)


def _expert_kernel(be_ref, bv_ref, x_ref, wg_ref, wu_ref, wd_ref, y_ref, wgb_ref, wub_ref, wdb_ref):
    i = pl.program_id(0)
    expert = be_ref[i]
    prev = be_ref[jnp.maximum(i - 1, 0)]

    @pl.when((i == 0) | (expert != prev))
    def _():
        wgb_ref[...] = wg_ref[...].astype(BF16)
        wub_ref[...] = wu_ref[...].astype(BF16)
        wdb_ref[...] = wd_ref[...].astype(BF16)

    valid = bv_ref[i]

    @pl.when(valid > 0)
    def _():
        lo, hi = _unpack_rows(x_ref[...])
        rows = lax.broadcasted_iota(I32, (x_ref.shape[0], 1), 0)
        xb = jnp.where(rows < valid, jnp.concatenate([lo, hi], axis=1), 0.0).astype(BF16)
        gate = _dot(xb, wgb_ref[...])
        act = (gate * jax.nn.sigmoid(gate) * _dot(xb, wub_ref[...])).astype(BF16)
        y_ref[...] = _pack_rows(_dot(act, wdb_ref[...]))

    @pl.when(valid == 0)
    def _():
        y_ref[...] = jnp.zeros_like(y_ref)


def _experts(block_e, block_valid, xs, w_gate, w_up, w_down, layer):
    p_rows = xs.shape[0]
    d = D_MODEL
    return pl.pallas_call(
        _expert_kernel,
        grid_spec=pltpu.PrefetchScalarGridSpec(
            num_scalar_prefetch=2,
            grid=(p_rows // MOE_BLOCK,),
            in_specs=[pl.BlockSpec((MOE_BLOCK, PACK_W), lambda i, be, bv: (i, 0)),
                      pl.BlockSpec((None, None, d, D_EXPERT), lambda i, be, bv: (layer, be[i], 0, 0)),
                      pl.BlockSpec((None, None, d, D_EXPERT), lambda i, be, bv: (layer, be[i], 0, 0)),
                      pl.BlockSpec((None, None, D_EXPERT, d), lambda i, be, bv: (layer, be[i], 0, 0))],
            out_specs=pl.BlockSpec((MOE_BLOCK, PACK_W), lambda i, be, bv: (i, 0)),
            scratch_shapes=[pltpu.VMEM((d, D_EXPERT), BF16),
                            pltpu.VMEM((d, D_EXPERT), BF16),
                            pltpu.VMEM((D_EXPERT, d), BF16)]),
        out_shape=jax.ShapeDtypeStruct((p_rows, PACK_W), U32),
        compiler_params=_cparams(("arbitrary",)),
        name="moe_experts",
    )(block_e, block_valid, xs, w_gate, w_up, w_down)


def _combine_kernel(dest_ref, wts_ref, x_ref, gt2_ref, gf_ref, ys_ref, o_ref, buf_ref, sem, *, final):
    tm = x_ref.shape[0]

    def start(g, carry):
        base = pl.multiple_of(g * SUBLANES, SUBLANES)
        for r in range(SUBLANES):
            for k in range(TOP_K):
                _row_copy(ys_ref, dest_ref[(base + r) * TOP_K + k], buf_ref.at[k], base + r,
                          sem).start(priority=k % 2)
        return carry
    lax.fori_loop(0, tm // SUBLANES, start, 0)

    def wait(t, carry):
        for k in range(TOP_K):
            _row_copy(ys_ref, dest_ref[t * TOP_K + k], buf_ref.at[k], t, sem).wait()
        return carry
    lax.fori_loop(0, tm, wait, 0)

    acc_lo = jnp.zeros((tm, PACK_W), F32)
    acc_hi = jnp.zeros((tm, PACK_W), F32)
    for k in range(TOP_K):
        lo, hi = _unpack_rows(buf_ref[k])
        w = wts_ref[:, k:k + 1]
        acc_lo = acc_lo + w * lo
        acc_hi = acc_hi + w * hi
    x = x_ref[...] + gt2_ref[...] * jnp.concatenate([acc_lo, acc_hi], axis=1)
    if final:
        ms = jnp.mean(x * x, axis=-1, keepdims=True)
        x = x * lax.rsqrt(ms + EPS) * gf_ref[...]
    o_ref[...] = x


def _combine(dest_t, wts, x_mid, gt2, g_final, ys, seq, final, tm=256):
    t, d = x_mid.shape
    per_seq = seq // tm
    return pl.pallas_call(
        functools.partial(_combine_kernel, final=final),
        grid=(t // tm,),
        in_specs=[pl.BlockSpec((tm * TOP_K,), lambda i: (i,), memory_space=pltpu.SMEM),
                  pl.BlockSpec((tm, TOP_K), lambda i: (i, 0)),
                  pl.BlockSpec((tm, d), lambda i: (i, 0)),
                  pl.BlockSpec((None, 1, d), lambda i: (i // per_seq, 0, 0)),
                  pl.BlockSpec((1, d), lambda i: (0, 0)),
                  pl.BlockSpec(memory_space=pl.ANY)],
        out_specs=pl.BlockSpec((tm, d), lambda i: (i, 0)),
        out_shape=jax.ShapeDtypeStruct((t, d), F32),
        scratch_shapes=[pltpu.VMEM((TOP_K, tm, PACK_W), U32),
                        pltpu.SemaphoreType.DMA(())],
        compiler_params=_cparams(("arbitrary",)),
        name="moe_combine",
    )(dest_t, wts, x_mid, gt2, g_final, ys)


def _pad_to(a, axis, size):
    pad = [(0, 0)] * a.ndim
    pad[axis] = (0, size - a.shape[axis])
    return jnp.pad(a, pad)


def _in_weight(w_in):
    off_b = ZA_W
    off_cq = off_b + ZB_W
    off_ckv = off_cq + MLA_Q_RANK
    off_kr = off_ckv + MLA_KV_RANK
    off_d = off_kr + MLA_ROPE
    parts = [w_in[:, :off_cq],
             _pad_to(w_in[:, off_cq:off_ckv], 1, CQ_PAD),
             _pad_to(w_in[:, off_ckv:off_kr], 1, CKV_PAD),
             _pad_to(w_in[:, off_kr:off_d], 1, KR_PAD),
             w_in[:, off_d:]]
    return jnp.concatenate(parts, axis=1).astype(BF16)


def _mla_weights(w_uq, w_ukv):
    wq = w_uq.reshape(MLA_Q_RANK, N_HEADS, MLA_QK)
    wq = _pad_to(_pad_to(wq, 2, ATT_W), 0, CQ_PAD).reshape(CQ_PAD, N_HEADS * ATT_W)
    wkv = _pad_to(w_ukv.reshape(MLA_KV_RANK, N_HEADS, MLA_NOPE + HEAD_DIM), 0, CKV_PAD)
    wk = _pad_to(wkv[:, :, :MLA_NOPE], 2, ATT_W).reshape(CKV_PAD, N_HEADS * ATT_W)
    wv = wkv[:, :, MLA_NOPE:].reshape(CKV_PAD, MIX_W)
    return wq.astype(BF16), wk.astype(BF16), wv.astype(BF16)


def _lookup(table, idx):
    ids = jnp.arange(table.shape[0], dtype=I32).reshape((-1,) + (1,) * idx.ndim)
    vals = table.reshape(ids.shape)
    return jnp.sum(jnp.where(idx[None] == ids, vals, 0), axis=0)


def _route_tables(idx_t, rank_t, counts, p_rows):
    m = MOE_BLOCK
    padded = (counts + m - 1) // m * m
    pend = jnp.cumsum(padded)
    pstart = pend - padded
    dest_t = _lookup(pstart, idx_t) + rank_t
    blk_start = jnp.arange(p_rows // m, dtype=I32) * m
    block_e = jnp.minimum(jnp.sum(pend[None, :] <= blk_start[:, None], axis=1), N_EXPERTS - 1)
    block_e = block_e.astype(I32)
    block_valid = jnp.clip(_lookup(pstart + counts, block_e) - blk_start, 0, m).astype(I32)
    dest = dest_t.astype(I32).T.reshape(-1)
    return dest, block_e, block_valid


def kernel(x, c, positions, w_mod, b_mod, g_attn, w_in, g_gmlp_v, w_spatial, b_spatial, g_mla_q, w_mla_uq, g_mla_kv, w_mla_ukv, w_conv_dw, b_conv_dw, g_conv_ln, b_conv_ln, w_out, g_ffn, w_router, b_router, w_exp_gate, w_exp_up, w_exp_down, w_sh_gate, w_sh_up, w_sh_down, g_final):
    batch, seq, d = x.shape
    n_layers = w_mod.shape[0]
    t = batch * seq
    p_rows = t * TOP_K + N_EXPERTS * MOE_BLOCK

    xf = x.reshape(t, d)
    pos_col = positions.reshape(t, 1)
    pos_chunks = positions.reshape(batch, seq // ATT_TK, ATT_TK)
    slopes = 2.0 ** (-8.0 * jnp.arange(1, N_HEADS + 1, dtype=F32) / N_HEADS)
    half = MLA_ROPE // 2
    inv_freq = ROPE_THETA ** (-jnp.arange(half, dtype=F32) * (2.0 / MLA_ROPE))
    invf = _pad_to(jnp.concatenate([inv_freq, inv_freq]), 0, KR_PAD).reshape(1, KR_PAD)

    mod = _modulation(c, w_mod, b_mod).reshape(n_layers, N_MOD, batch, 1, d)

    routed = None
    for l in range(n_layers):
        sh1, sc1, gt1, sh2, sc2, gt2 = (mod[l, j] for j in range(N_MOD))
        if routed is not None:
            xf = _combine(*routed, seq=seq, final=False)

        qm, ka, va, zb, zc, zd = _in_projection(xf, g_attn[l].reshape(1, d), sc1, sh1,
                                                _in_weight(w_in[l]), seq)
        y_a = _moba_attention(qm, ka, va, pos_col, pos_chunks, slopes, batch, seq)
        y_b = _spatial_gating(zb, g_gmlp_v[l], w_spatial[l], b_spatial[l])
        wq, wk, wv = _mla_weights(w_mla_uq[l], w_mla_ukv[l])
        q, k, v = _mla_prep(zc, pos_col, invf,
                            _pad_to(g_mla_q[l], 0, CQ_PAD).reshape(1, CQ_PAD),
                            _pad_to(g_mla_kv[l], 0, CKV_PAD).reshape(1, CKV_PAD), wq, wk, wv)
        y_c = _mla_attention(q, k, v, batch, seq)
        y_d = _conformer_conv(zd, w_conv_dw[l].reshape(CONV_WIDTH, MIX_W),
                              b_conv_dw[l].reshape(1, MIX_W), g_conv_ln[l].reshape(1, MIX_W),
                              b_conv_ln[l].reshape(1, MIX_W), seq)

        x_mid, hp, idx_t, wts_t, rank_t, cnt = _out_projection(
            xf, (y_a, y_b, y_c, y_d), w_out[l].astype(BF16), gt1, g_ffn[l].reshape(1, d),
            sc2, sh2, gt2, w_router[l].T, b_router[l].reshape(N_EXPERTS, 1),
            w_sh_gate[l].astype(BF16), w_sh_up[l].astype(BF16), w_sh_down[l].astype(BF16), seq)

        dest, block_e, block_valid = _route_tables(idx_t, rank_t, cnt[:, 0], p_rows)
        xs = _dispatch(dest, hp, p_rows)
        ys = _experts(block_e, block_valid, xs, w_exp_gate, w_exp_up, w_exp_down, l)
        routed = (dest, wts_t.T, x_mid, gt2, g_final.reshape(1, d), ys)

    out = _combine(*routed, seq=seq, final=True)
    return out.reshape(batch, seq, d)
```

```python
import functools

import jax
import jax.numpy as jnp
from jax import lax
from jax.experimental import pallas as pl
from jax.experimental.pallas import tpu as pltpu

F32 = jnp.float32
BF16 = jnp.bfloat16
I32 = jnp.int32
U32 = jnp.uint32

D_MODEL = 2048
HEAD_DIM = 128
N_HEADS = 4
MIX_W = N_HEADS * HEAD_DIM
MOBA_BLOCK = 256
MOBA_TOPK = 3
GMLP_CHUNK = 128
MLA_Q_RANK = 448
MLA_KV_RANK = 160
MLA_NOPE = 128
MLA_ROPE = 64
MLA_QK = MLA_NOPE + MLA_ROPE
ROPE_THETA = 10000.0
CONV_WIDTH = 31
CONV_HALO = 32
SUBLANES = 8
N_EXPERTS = 64
TOP_K = 8
D_EXPERT = 512
D_SHARED = 512
ROUTED_SCALE = 2.5
MOE_BLOCK = 512
N_MOD = 6
EPS = 1e-6
NEG = -1e30
LOG2E = 1.4426950408889634

ATT_TQ = 2048
ATT_SUB = 256
ATT_TK = 1024
ATT_W = 2 * HEAD_DIM

ZA_W = 3 * MIX_W
ZB_W = 2 * MIX_W
CQ_PAD = 512
CKV_PAD = 256
KR_PAD = 128
ZC_W = CQ_PAD + CKV_PAD + KR_PAD
ZD_W = 2 * MIX_W

VMEM_LIMIT = 56 * 1024 * 1024
PACK_W = D_MODEL // 2


def _cparams(sem):
    return pltpu.CompilerParams(dimension_semantics=sem, vmem_limit_bytes=VMEM_LIMIT)


def _resident(shape, index_map):
    return pl.BlockSpec(shape, index_map, pipeline_mode=pl.Buffered(1))


def _dot(a, b):
    return jnp.dot(a, b, preferred_element_type=F32)


def _dot_nt(a, b, precision=None):
    return lax.dot_general(a, b, (((1,), (1,)), ((), ())), preferred_element_type=F32,
                           precision=precision)


def _pack_rows(y):
    half = y.shape[1] // 2
    lo = lax.bitcast_convert_type(y[:, :half].astype(BF16).astype(F32), U32)
    hi = lax.bitcast_convert_type(y[:, half:].astype(BF16).astype(F32), U32)
    return (lo >> 16) | hi


def _unpack_rows(u):
    lo = lax.bitcast_convert_type(u << 16, F32)
    hi = lax.bitcast_convert_type(u & jnp.uint32(0xFFFF0000), F32)
    return lo, hi


def _mod_kernel(c_ref, w_ref, b_ref, o_ref):
    c = c_ref[...]
    a = c * jax.nn.sigmoid(c)
    o_ref[...] = jnp.dot(a, w_ref[...], preferred_element_type=F32,
                         precision=lax.Precision.HIGHEST) + b_ref[...]


def _modulation(c, w_mod, b_mod):
    n_layers, d, _ = w_mod.shape
    b = c.shape[0]
    return pl.pallas_call(
        _mod_kernel,
        grid=(n_layers, N_MOD),
        in_specs=[pl.BlockSpec((b, d), lambda l, j: (0, 0)),
                  pl.BlockSpec((None, d, d), lambda l, j: (l, 0, j)),
                  pl.BlockSpec((None, None, 1, d), lambda l, j: (l, j, 0, 0))],
        out_specs=pl.BlockSpec((None, None, b, d), lambda l, j: (l, j, 0, 0)),
        out_shape=jax.ShapeDtypeStruct((n_layers, N_MOD, b, d), F32),
        compiler_params=_cparams(("arbitrary", "arbitrary")),
        name="modulation",
    )(c, w_mod, b_mod.reshape(n_layers, N_MOD, 1, d))


def _inproj_kernel(x_ref, g_ref, sc_ref, sh_ref, w_ref, q_ref, ka_ref, va_ref, zb_ref, zc_ref, zd_ref,
                   *, seq):
    tm = x_ref.shape[0]
    x = x_ref[...]
    ms = jnp.mean(x * x, axis=-1, keepdims=True)
    h = (x * lax.rsqrt(ms + EPS) * g_ref[...]) * (1.0 + sc_ref[...]) + sh_ref[...]
    hb = h.astype(BF16)

    q_ref[...] = _dot(hb, w_ref[:, 0:MIX_W]).astype(BF16)
    pos = (pl.program_id(0) * tm) % seq + lax.broadcasted_iota(I32, (tm, HEAD_DIM), 0)
    lane = lax.broadcasted_iota(I32, (tm, HEAD_DIM), 1)
    onehot = jnp.where(pos // MOBA_BLOCK == lane, 1.0, 0.0).astype(BF16)
    ones = jnp.ones((tm, HEAD_DIM), BF16)
    k = _dot(hb, w_ref[:, MIX_W:2 * MIX_W]).astype(BF16)
    v = _dot(hb, w_ref[:, 2 * MIX_W:ZA_W]).astype(BF16)
    for hd in range(N_HEADS):
        src = slice(hd * HEAD_DIM, (hd + 1) * HEAD_DIM)
        ka_ref[:, hd * ATT_W:hd * ATT_W + HEAD_DIM] = k[:, src]
        ka_ref[:, hd * ATT_W + HEAD_DIM:(hd + 1) * ATT_W] = onehot
        va_ref[:, hd * ATT_W:hd * ATT_W + HEAD_DIM] = v[:, src]
        va_ref[:, hd * ATT_W + HEAD_DIM:(hd + 1) * ATT_W] = ones

    off = ZA_W
    for ref in (zb_ref, zc_ref, zd_ref):
        w = ref.shape[1]
        ref[...] = _dot(hb, w_ref[:, off:off + w]).astype(BF16)
        off += w


def _in_projection(x, g, sc, sh, w_in_p, seq, tm=512):
    t, d = x.shape
    per_seq = seq // tm
    widths = (MIX_W, N_HEADS * ATT_W, N_HEADS * ATT_W, ZB_W, ZC_W, ZD_W)
    mod_spec = pl.BlockSpec((None, 1, d), lambda i: (i // per_seq, 0, 0))
    return pl.pallas_call(
        functools.partial(_inproj_kernel, seq=seq),
        grid=(t // tm,),
        in_specs=[pl.BlockSpec((tm, d), lambda i: (i, 0)),
                  pl.BlockSpec((1, d), lambda i: (0, 0)),
                  mod_spec, mod_spec,
                  _resident((d, ZA_W + ZB_W + ZC_W + ZD_W), lambda i: (0, 0))],
        out_specs=[pl.BlockSpec((tm, w), lambda i: (i, 0)) for w in widths],
        out_shape=[jax.ShapeDtypeStruct((t, w), BF16) for w in widths],
        compiler_params=_cparams(("arbitrary",)),
        name="in_projection",
    )(x, g, sc, sh, w_in_p)


def _attn_kernel(*refs, moba):
    if moba:
        slope_ref, q_ref, k_ref, v_ref, pq_ref, pk_ref, o_ref, kmean_ref, qa_ref, m_ref, acc_ref = refs
    else:
        q_ref, k_ref, v_ref, o_ref, m_ref, acc_ref = refs
        qa_ref = q_ref
    tq, sub, tk = ATT_TQ, ATT_SUB, ATT_TK
    nsub = tq // sub
    i = pl.program_id(2)

    if moba:
        nb_pad = kmean_ref.shape[0]
        nb = k_ref.shape[0] // MOBA_BLOCK

        @pl.when(i == 0)
        def _():
            kmean_ref[...] = jnp.zeros_like(kmean_ref)

            def body(n, carry):
                start = pl.multiple_of(n * MOBA_BLOCK, MOBA_BLOCK)
                kb = k_ref[pl.ds(start, MOBA_BLOCK), 0:HEAD_DIM].astype(F32)
                kmean_ref[pl.ds(n, 1), :] = jnp.mean(kb, axis=0, keepdims=True)
                return carry
            lax.fori_loop(0, nb, body, 0)

        slope = slope_ref[pl.program_id(1)] * LOG2E
        col = lax.broadcasted_iota(I32, (sub, nb_pad), 1)
        colf = col.astype(F32)
        pq_s = []
        for c in range(nsub):
            rows = slice(c * sub, (c + 1) * sub)
            own = i * nsub + c
            qf = q_ref[rows, :].astype(F32)
            gate = _dot_nt(qf, kmean_ref[...], precision=lax.Precision.HIGHEST)
            gate = jnp.where(col < own, gate, NEG)
            sel = col == own
            for _ in range(MOBA_TOPK):
                gmax = jnp.max(gate, axis=-1, keepdims=True)
                first = jnp.min(jnp.where(gate == gmax, colf, float(nb_pad)), axis=-1, keepdims=True)
                hit = colf == first
                sel = sel | (hit & (col < own))
                gate = jnp.where(hit, -jnp.inf, gate)
            qa_ref[rows, 0:HEAD_DIM] = (qf * (HEAD_DIM ** -0.5 * LOG2E)).astype(BF16)
            qa_ref[rows, HEAD_DIM:] = jnp.where(sel, 0.0, NEG).astype(BF16)
            pq_s.append(slope * pq_ref[rows, :].astype(F32))

    m_ref[...] = jnp.full(m_ref.shape, NEG, F32)
    acc_ref[...] = jnp.zeros_like(acc_ref)

    def step(c, chunk, mask_off):
        rows = slice(c * sub, (c + 1) * sub)
        start = pl.multiple_of(chunk * tk, tk)
        s = _dot_nt(qa_ref[rows, :], k_ref[pl.ds(start, tk), :])
        if moba:
            s = s - jnp.abs(pq_s[c] - slope * pk_ref[pl.ds(chunk, 1), :].astype(F32))
        if mask_off is not None:
            qi = lax.broadcasted_iota(I32, (sub, tk), 0)
            ki = lax.broadcasted_iota(I32, (sub, tk), 1)
            s = jnp.where(ki <= qi + mask_off, s, NEG)
        m_old = m_ref[rows, :]
        m_new = jnp.maximum(m_old, jnp.max(s, axis=-1, keepdims=True))
        p = jnp.exp2(s - m_new).astype(BF16)
        acc_ref[rows, :] = jnp.exp2(m_old - m_new) * acc_ref[rows, :] + _dot(p, v_ref[pl.ds(start, tk), :])
        m_ref[rows, :] = m_new

    per_tile = tq // tk
    for c in range(nsub):
        q_lo = c * sub
        for jj in reversed(range(per_tile)):
            k_lo = jj * tk
            if k_lo > q_lo + sub - 1:
                continue
            needs_mask = k_lo + tk - 1 > q_lo
            step(c, i * per_tile + jj, q_lo - k_lo if needs_mask else None)

    def body(j, carry):
        for c in range(nsub):
            step(c, j, None)
        return carry
    lax.fori_loop(0, i * per_tile, body, 0)

    acc = acc_ref[...]
    o_ref[...] = (acc[:, :HEAD_DIM] / acc[:, HEAD_DIM:]).astype(o_ref.dtype)


def _attn_scratch():
    return [pltpu.VMEM((ATT_TQ, 1), F32), pltpu.VMEM((ATT_TQ, ATT_W), F32)]


def _moba_attention(q, ka, va, pos_col, pos_chunks, slopes, batch, seq):
    t = q.shape[0]
    nq = seq // ATT_TQ
    kv_spec = pl.BlockSpec((seq, ATT_W), lambda b, h, i, s: (b, h))
    return pl.pallas_call(
        functools.partial(_attn_kernel, moba=True),
        grid_spec=pltpu.PrefetchScalarGridSpec(
            num_scalar_prefetch=1,
            grid=(batch, N_HEADS, nq),
            in_specs=[pl.BlockSpec((ATT_TQ, HEAD_DIM), lambda b, h, i, s: (b * nq + i, h)),
                      kv_spec, kv_spec,
                      pl.BlockSpec((ATT_TQ, 1), lambda b, h, i, s: (b * nq + i, 0)),
                      pl.BlockSpec((None, seq // ATT_TK, ATT_TK), lambda b, h, i, s: (b, 0, 0))],
            out_specs=pl.BlockSpec((ATT_TQ, HEAD_DIM), lambda b, h, i, s: (b * nq + i, h)),
            scratch_shapes=[pltpu.VMEM((HEAD_DIM, HEAD_DIM), F32),
                            pltpu.VMEM((ATT_TQ, ATT_W), BF16)] + _attn_scratch()),
        out_shape=jax.ShapeDtypeStruct((t, MIX_W), BF16),
        compiler_params=_cparams(("arbitrary", "arbitrary", "arbitrary")),
        name="moba_attention",
    )(slopes, q, ka, va, pos_col, pos_chunks)


def _mla_attention(q, k, va, batch, seq):
    t = q.shape[0]
    nq = seq // ATT_TQ
    kv_spec = pl.BlockSpec((seq, ATT_W), lambda b, h, i: (b, h))
    return pl.pallas_call(
        functools.partial(_attn_kernel, moba=False),
        grid=(batch, N_HEADS, nq),
        in_specs=[pl.BlockSpec((ATT_TQ, ATT_W), lambda b, h, i: (b * nq + i, h)),
                  kv_spec, kv_spec],
        out_specs=pl.BlockSpec((ATT_TQ, HEAD_DIM), lambda b, h, i: (b * nq + i, h)),
        out_shape=jax.ShapeDtypeStruct((t, MIX_W), BF16),
        scratch_shapes=_attn_scratch(),
        compiler_params=_cparams(("arbitrary", "arbitrary", "arbitrary")),
        name="mla_attention",
    )(q, k, va)


def _gelu_tanh(x):
    return 0.5 * x * (1.0 + jnp.tanh(0.7978845608028654 * (x + 0.044715 * x * x * x)))


def _gmlp_kernel(z_ref, gv_ref, ws_ref, bs_ref, o_ref):
    tm = z_ref.shape[0]
    ck = GMLP_CHUNK
    z = _gelu_tanh(z_ref[...].astype(F32))
    row = lax.broadcasted_iota(I32, (ck, ck), 0)
    colm = lax.broadcasted_iota(I32, (ck, ck), 1)
    for g in range(N_HEADS):
        lanes = slice(g * HEAD_DIM, (g + 1) * HEAD_DIM)
        u = z[:, lanes]
        vv = z[:, MIX_W + g * HEAD_DIM:MIX_W + (g + 1) * HEAD_DIM]
        ms = jnp.mean(vv * vv, axis=-1, keepdims=True)
        vn = (vv * lax.rsqrt(ms + EPS) * gv_ref[g:g + 1, :]).astype(BF16)
        w = jnp.where(colm <= row, ws_ref[g], 0.0).astype(BF16)
        bias = bs_ref[g]
        for c in range(tm // ck):
            rows = slice(c * ck, (c + 1) * ck)
            mixed = _dot(w, vn[rows]) + bias
            o_ref[rows, lanes] = (u[rows] * mixed).astype(o_ref.dtype)


def _spatial_gating(zb, g_v, w_s, b_s, tm=512):
    t = zb.shape[0]
    ck = GMLP_CHUNK
    return pl.pallas_call(
        _gmlp_kernel,
        grid=(t // tm,),
        in_specs=[pl.BlockSpec((tm, ZB_W), lambda i: (i, 0)),
                  pl.BlockSpec((N_HEADS, HEAD_DIM), lambda i: (0, 0)),
                  pl.BlockSpec((N_HEADS, ck, ck), lambda i: (0, 0, 0)),
                  pl.BlockSpec((N_HEADS, ck, 1), lambda i: (0, 0, 0))],
        out_specs=pl.BlockSpec((tm, MIX_W), lambda i: (i, 0)),
        out_shape=jax.ShapeDtypeStruct((t, MIX_W), BF16),
        compiler_params=_cparams(("arbitrary",)),
        name="spatial_gating",
    )(zb, g_v, w_s, b_s.reshape(N_HEADS, ck, 1))


def _mla_prep_kernel(z_ref, pos_ref, invf_ref, gq_ref, gkv_ref, wq_ref, wk_ref, wv_ref,
                     q_ref, k_ref, v_ref):
    tm = z_ref.shape[0]
    z = z_ref[...].astype(F32)
    cq = z[:, :CQ_PAD]
    ckv = z[:, CQ_PAD:CQ_PAD + CKV_PAD]
    kr = z[:, CQ_PAD + CKV_PAD:]
    qn = cq * lax.rsqrt(jnp.sum(cq * cq, -1, keepdims=True) * (1.0 / MLA_Q_RANK) + EPS)
    qn = (qn * gq_ref[...]).astype(BF16)
    kvn = ckv * lax.rsqrt(jnp.sum(ckv * ckv, -1, keepdims=True) * (1.0 / MLA_KV_RANK) + EPS)
    kvn = (kvn * gkv_ref[...]).astype(BF16)

    ang = pos_ref[...].astype(F32) * invf_ref[...]
    lane = lax.broadcasted_iota(I32, ang.shape, 1)
    half = MLA_ROPE // 2
    cos = jnp.cos(ang)
    sin = jnp.sin(ang)
    sin_lo = jnp.where(lane < half, -sin, 0.0)
    sin_hi = jnp.where((lane >= half) & (lane < 2 * half), sin, 0.0)

    def rope(r):
        return (r * cos + pltpu.roll(r, KR_PAD - half, 1) * sin_lo
                + pltpu.roll(r, half, 1) * sin_hi)

    q = _dot(qn, wq_ref[...])
    kn = _dot(kvn, wk_ref[...])
    v = _dot(kvn, wv_ref[...]).astype(BF16)
    k_rope = rope(kr)
    scale = MLA_QK ** -0.5 * LOG2E
    ones = jnp.ones((tm, HEAD_DIM), BF16)
    for h in range(N_HEADS):
        a = h * ATT_W
        b = a + MLA_NOPE
        q_ref[:, a:b] = (q[:, a:b] * scale).astype(q_ref.dtype)
        q_ref[:, b:a + ATT_W] = (rope(q[:, b:a + ATT_W]) * scale).astype(q_ref.dtype)
        k_ref[:, a:b] = kn[:, a:b].astype(k_ref.dtype)
        k_ref[:, b:a + ATT_W] = k_rope.astype(k_ref.dtype)
        v_ref[:, a:b] = v[:, h * HEAD_DIM:(h + 1) * HEAD_DIM]
        v_ref[:, b:a + ATT_W] = ones


def _mla_prep(zc, pos_col, invf, gq, gkv, wq, wk, wv, tm=512):
    t = zc.shape[0]
    qk_w = N_HEADS * ATT_W
    full = lambda i: (0, 0)
    out_spec = pl.BlockSpec((tm, qk_w), lambda i: (i, 0))
    out_shape = jax.ShapeDtypeStruct((t, qk_w), BF16)
    return pl.pallas_call(
        _mla_prep_kernel,
        grid=(t // tm,),
        in_specs=[pl.BlockSpec((tm, ZC_W), lambda i: (i, 0)),
                  pl.BlockSpec((tm, 1), lambda i: (i, 0)),
                  pl.BlockSpec((1, KR_PAD), full),
                  pl.BlockSpec((1, CQ_PAD), full),
                  pl.BlockSpec((1, CKV_PAD), full),
                  pl.BlockSpec((CQ_PAD, qk_w), full),
                  pl.BlockSpec((CKV_PAD, qk_w), full),
                  pl.BlockSpec((CKV_PAD, MIX_W), full)],
        out_specs=[out_spec, out_spec, out_spec],
        out_shape=[out_shape, out_shape, out_shape],
        compiler_params=_cparams(("arbitrary",)),
        name="mla_prep",
    )(zc, pos_col, invf, gq, gkv, wq, wk, wv)


def _glu(z):
    z = z.astype(F32)
    return z[:, :MIX_W] * jax.nn.sigmoid(z[:, MIX_W:])


def _conv_kernel(z_ref, zprev_ref, w_ref, b_ref, g_ref, beta_ref, o_ref, ybuf_ref, ysh_ref, *, per_seq):
    tm = z_ref.shape[0]
    first = (pl.program_id(0) % per_seq) == 0
    ybuf_ref[0:CONV_HALO, :] = jnp.where(first, 0.0, _glu(zprev_ref[...]))
    ybuf_ref[CONV_HALO:, :] = _glu(z_ref[...])
    span = tm + CONV_HALO - SUBLANES
    for o in range(1, SUBLANES):
        ysh_ref[o - 1, :, :] = ybuf_ref[pl.ds(o, span), :]
    rows = 64
    shift = CONV_HALO - (CONV_WIDTH - 1)
    for r in range(tm // rows):
        acc = jnp.zeros((rows, MIX_W), F32) + b_ref[...]
        for j in range(CONV_WIDTH):
            o = (shift + j) % SUBLANES
            base = r * rows + shift + j - o
            tap = ybuf_ref[pl.ds(base, rows), :] if o == 0 else ysh_ref[o - 1, pl.ds(base, rows), :]
            acc = acc + w_ref[j:j + 1, :] * tap
        mu = jnp.mean(acc, axis=-1, keepdims=True)
        xc = acc - mu
        y = xc * lax.rsqrt(jnp.mean(xc * xc, axis=-1, keepdims=True) + EPS)
        y = y * g_ref[...] + beta_ref[...]
        o_ref[r * rows:(r + 1) * rows, :] = (y * jax.nn.sigmoid(y)).astype(o_ref.dtype)


def _conformer_conv(zd, w_dw, b_dw, g_ln, b_ln, seq, tm=512):
    t = zd.shape[0]
    per_seq = seq // tm
    halo_blocks = tm // CONV_HALO
    full = lambda i: (0, 0)
    return pl.pallas_call(
        functools.partial(_conv_kernel, per_seq=per_seq),
        grid=(t // tm,),
        in_specs=[pl.BlockSpec((tm, ZD_W), lambda i: (i, 0)),
                  pl.BlockSpec((CONV_HALO, ZD_W),
                               lambda i: (jnp.maximum(i * halo_blocks - 1, 0), 0)),
                  pl.BlockSpec((CONV_WIDTH, MIX_W), full),
                  pl.BlockSpec((1, MIX_W), full),
                  pl.BlockSpec((1, MIX_W), full),
                  pl.BlockSpec((1, MIX_W), full)],
        out_specs=pl.BlockSpec((tm, MIX_W), lambda i: (i, 0)),
        out_shape=jax.ShapeDtypeStruct((t, MIX_W), BF16),
        scratch_shapes=[pltpu.VMEM((tm + CONV_HALO, MIX_W), F32),
                        pltpu.VMEM((SUBLANES - 1, tm + CONV_HALO - SUBLANES, MIX_W), F32)],
        compiler_params=_cparams(("arbitrary",)),
        name="conformer_conv",
    )(zd, zd, w_dw, b_dw, g_ln, b_ln)


def _outproj_kernel(x_ref, ya_ref, yb_ref, yc_ref, yd_ref, wo_ref, gt1_ref, g_ref, sc_ref, sh_ref,
                    gt2_ref, wr_ref, br_ref, wsg_ref, wsu_ref, wsd_ref,
                    xo_ref, hp_ref, idx_ref, wts_ref, rank_ref, cnt_ref, carry_ref):
    tm = x_ref.shape[0]

    @pl.when(pl.program_id(0) == 0)
    def _():
        carry_ref[...] = jnp.zeros_like(carry_ref)

    y = _dot(ya_ref[...], wo_ref[0:MIX_W, :])
    y = y + _dot(yb_ref[...], wo_ref[MIX_W:2 * MIX_W, :])
    y = y + _dot(yc_ref[...], wo_ref[2 * MIX_W:3 * MIX_W, :])
    y = y + _dot(yd_ref[...], wo_ref[3 * MIX_W:, :])
    x = x_ref[...] + gt1_ref[...] * y
    ms = jnp.mean(x * x, axis=-1, keepdims=True)
    h = (x * lax.rsqrt(ms + EPS) * g_ref[...]) * (1.0 + sc_ref[...]) + sh_ref[...]
    hb = h.astype(BF16)
    hp_ref[...] = _pack_rows(h)

    act = _dot(hb, wsg_ref[...])
    act = (act * jax.nn.sigmoid(act) * _dot(hb, wsu_ref[...])).astype(BF16)
    xo_ref[...] = x + gt2_ref[...] * _dot(act, wsd_ref[...])

    logits = _dot_nt(wr_ref[...], h, precision=lax.Precision.HIGHEST)
    scores = jax.nn.sigmoid(logits)
    cur = scores + br_ref[...]
    erow = lax.broadcasted_iota(I32, (N_EXPERTS, tm), 0).astype(F32)
    picked = jnp.zeros((N_EXPERTS, tm), F32)
    hits, idxs, wts = [], [], []
    for _ in range(TOP_K):
        cmax = jnp.max(cur, axis=0, keepdims=True)
        first = jnp.min(jnp.where(cur == cmax, erow, float(N_EXPERTS)), axis=0, keepdims=True)
        hit = erow == first
        hits.append(hit)
        idxs.append(first)
        wts.append(jnp.sum(jnp.where(hit, scores, 0.0), axis=0, keepdims=True))
        picked = jnp.where(hit, 1.0, picked)
        cur = jnp.where(hit, -jnp.inf, cur)
    wsum = wts[0]
    for w in wts[1:]:
        wsum = wsum + w
    ti = lax.broadcasted_iota(I32, (tm, tm), 0)
    tj = lax.broadcasted_iota(I32, (tm, tm), 1)
    before = jnp.where(ti < tj, 1.0, 0.0).astype(BF16)
    prior = _dot(picked.astype(BF16), before) + carry_ref[...]
    for k in range(TOP_K):
        idx_ref[k:k + 1, :] = idxs[k].astype(I32)
        wts_ref[k:k + 1, :] = wts[k] / wsum * ROUTED_SCALE
        rank_ref[k:k + 1, :] = jnp.sum(jnp.where(hits[k], prior, 0.0), axis=0,
                                       keepdims=True).astype(I32)
    carry_ref[...] = carry_ref[...] + jnp.sum(picked, axis=1, keepdims=True)
    cnt_ref[...] = jnp.broadcast_to(carry_ref[...], cnt_ref.shape).astype(I32)


def _out_projection(x, ys, w_out, gt1, g_ffn, sc2, sh2, gt2, w_rt, b_r, wsg, wsu, wsd, seq, tm=512):
    t, d = x.shape
    per_seq = seq // tm
    full = lambda i: (0, 0)
    mod_spec = pl.BlockSpec((None, 1, d), lambda i: (i // per_seq, 0, 0))
    y_spec = pl.BlockSpec((tm, MIX_W), lambda i: (i, 0))
    tok_spec = pl.BlockSpec((TOP_K, tm), lambda i: (0, i))
    return pl.pallas_call(
        _outproj_kernel,
        grid=(t // tm,),
        in_specs=[pl.BlockSpec((tm, d), lambda i: (i, 0)),
                  y_spec, y_spec, y_spec, y_spec,
                  _resident((4 * MIX_W, d), full),
                  mod_spec,
                  pl.BlockSpec((1, d), full),
                  mod_spec, mod_spec, mod_spec,
                  pl.BlockSpec((N_EXPERTS, d), full),
                  pl.BlockSpec((N_EXPERTS, 1), full),
                  _resident((d, D_SHARED), full),
                  _resident((d, D_SHARED), full),
                  _resident((D_SHARED, d), full)],
        out_specs=[pl.BlockSpec((tm, d), lambda i: (i, 0)),
                   pl.BlockSpec((tm, PACK_W), lambda i: (i, 0)),
                   tok_spec, tok_spec, tok_spec,
                   pl.BlockSpec((N_EXPERTS, 128), full)],
        out_shape=[jax.ShapeDtypeStruct((t, d), F32),
                   jax.ShapeDtypeStruct((t, PACK_W), U32),
                   jax.ShapeDtypeStruct((TOP_K, t), I32),
                   jax.ShapeDtypeStruct((TOP_K, t), F32),
                   jax.ShapeDtypeStruct((TOP_K, t), I32),
                   jax.ShapeDtypeStruct((N_EXPERTS, 128), I32)],
        scratch_shapes=[pltpu.VMEM((N_EXPERTS, 1), F32)],
        compiler_params=_cparams(("arbitrary",)),
        name="out_projection_router",
    )(x, *ys, w_out, gt1, g_ffn, sc2, sh2, gt2, w_rt, b_r, wsg, wsu, wsd)


def _row_copy(src_ref, src_row, dst_ref, dst_row, sem):
    return pltpu.make_async_copy(src_ref.at[pl.ds(src_row, 1)], dst_ref.at[pl.ds(dst_row, 1)], sem)


def _dispatch_kernel(dest_ref, hp_ref, xs_ref, sem):
    tm = hp_ref.shape[0]

    def start(t, carry):
        for k in range(TOP_K):
            _row_copy(hp_ref, t, xs_ref, dest_ref[t * TOP_K + k], sem).start(priority=k % 2)
        return carry
    lax.fori_loop(0, tm, start, 0)

    def wait(t, carry):
        for k in range(TOP_K):
            _row_copy(hp_ref, t, xs_ref, dest_ref[t * TOP_K + k], sem).wait()
        return carry
    lax.fori_loop(0, tm, wait, 0)


def _dispatch(dest_t, hp, p_rows, tm=512):
    t = hp.shape[0]
    return pl.pallas_call(
        _dispatch_kernel,
        grid=(t // tm,),
        in_specs=[pl.BlockSpec((tm * TOP_K,), lambda i: (i,), memory_space=pltpu.SMEM),
                  pl.BlockSpec((tm, PACK_W), lambda i: (i, 0))],
        out_specs=pl.BlockSpec(memory_space=pl.ANY),
        out_shape=jax.ShapeDtypeStruct((p_rows, PACK_W), U32),
        scratch_shapes=[pltpu.SemaphoreType.DMA(())],
        compiler_params=_cparams(("arbitrary",)),
        name="moe_dispatch",
    )(dest_t, hp)


def _expert_kernel(be_ref, bv_ref, x_ref, wg_ref, wu_ref, wd_ref, y_ref, wgb_ref, wub_ref, wdb_ref):
    i = pl.program_id(0)
    expert = be_ref[i]
    prev = be_ref[jnp.maximum(i - 1, 0)]

    @pl.when((i == 0) | (expert != prev))
    def _():
        wgb_ref[...] = wg_ref[...].astype(BF16)
        wub_ref[...] = wu_ref[...].astype(BF16)
        wdb_ref[...] = wd_ref[...].astype(BF16)

    valid = bv_ref[i]

    @pl.when(valid > 0)
    def _():
        lo, hi = _unpack_rows(x_ref[...])
        rows = lax.broadcasted_iota(I32, (x_ref.shape[0], 1), 0)
        xb = jnp.where(rows < valid, jnp.concatenate([lo, hi], axis=1), 0.0).astype(BF16)
        gate = _dot(xb, wgb_ref[...])
        act = (gate * jax.nn.sigmoid(gate) * _dot(xb, wub_ref[...])).astype(BF16)
        y_ref[...] = _pack_rows(_dot(act, wdb_ref[...]))

    @pl.when(valid == 0)
    def _():
        y_ref[...] = jnp.zeros_like(y_ref)


def _experts(block_e, block_valid, xs, w_gate, w_up, w_down, layer):
    p_rows = xs.shape[0]
    d = D_MODEL
    return pl.pallas_call(
        _expert_kernel,
        grid_spec=pltpu.PrefetchScalarGridSpec(
            num_scalar_prefetch=2,
            grid=(p_rows // MOE_BLOCK,),
            in_specs=[pl.BlockSpec((MOE_BLOCK, PACK_W), lambda i, be, bv: (i, 0)),
                      pl.BlockSpec((None, None, d, D_EXPERT), lambda i, be, bv: (layer, be[i], 0, 0)),
                      pl.BlockSpec((None, None, d, D_EXPERT), lambda i, be, bv: (layer, be[i], 0, 0)),
                      pl.BlockSpec((None, None, D_EXPERT, d), lambda i, be, bv: (layer, be[i], 0, 0))],
            out_specs=pl.BlockSpec((MOE_BLOCK, PACK_W), lambda i, be, bv: (i, 0)),
            scratch_shapes=[pltpu.VMEM((d, D_EXPERT), BF16),
                            pltpu.VMEM((d, D_EXPERT), BF16),
                            pltpu.VMEM((D_EXPERT, d), BF16)]),
        out_shape=jax.ShapeDtypeStruct((p_rows, PACK_W), U32),
        compiler_params=_cparams(("arbitrary",)),
        name="moe_experts",
    )(block_e, block_valid, xs, w_gate, w_up, w_down)


def _combine_kernel(dest_ref, dnext_ref, wts_ref, x_ref, gt2_ref, gf_ref, ys_ref, o_ref, buf_ref, sem,
                    *, final, n_steps):
    tm = x_ref.shape[0]
    i = pl.program_id(0)
    slot = lax.rem(i, 2)

    def start_rows(rows_ref, s):
        def body(g, carry):
            base = pl.multiple_of(g * SUBLANES, SUBLANES)
            for r in range(SUBLANES):
                for k in range(TOP_K):
                    _row_copy(ys_ref, rows_ref[(base + r) * TOP_K + k], buf_ref.at[s, k], base + r,
                              sem.at[s]).start(priority=k % 2)
            return carry
        lax.fori_loop(0, tm // SUBLANES, body, 0)

    def wait_rows(s):
        def body(t, carry):
            for k in range(TOP_K):
                _row_copy(ys_ref, dest_ref[t * TOP_K + k], buf_ref.at[s, k], t, sem.at[s]).wait()
            return carry
        lax.fori_loop(0, tm, body, 0)

    @pl.when(i == 0)
    def _():
        start_rows(dest_ref, 0)

    for s in range(2):
        @pl.when((i + 1 < n_steps) & (slot != s))
        def _():
            start_rows(dnext_ref, s)

    for s in range(2):
        @pl.when(slot == s)
        def _():
            wait_rows(s)

    acc_lo = jnp.zeros((tm, PACK_W), F32)
    acc_hi = jnp.zeros((tm, PACK_W), F32)
    for k in range(TOP_K):
        lo, hi = _unpack_rows(buf_ref[slot, k])
        w = wts_ref[:, k:k + 1]
        acc_lo = acc_lo + w * lo
        acc_hi = acc_hi + w * hi
    x = x_ref[...] + gt2_ref[...] * jnp.concatenate([acc_lo, acc_hi], axis=1)
    if final:
        ms = jnp.mean(x * x, axis=-1, keepdims=True)
        x = x * lax.rsqrt(ms + EPS) * gf_ref[...]
    o_ref[...] = x


def _combine(dest_t, wts, x_mid, gt2, g_final, ys, seq, final, tm=256):
    t, d = x_mid.shape
    per_seq = seq // tm
    steps = t // tm
    return pl.pallas_call(
        functools.partial(_combine_kernel, final=final, n_steps=steps),
        grid=(steps,),
        in_specs=[pl.BlockSpec((tm * TOP_K,), lambda i: (i,), memory_space=pltpu.SMEM),
                  pl.BlockSpec((tm * TOP_K,), lambda i: (jnp.minimum(i + 1, steps - 1),),
                               memory_space=pltpu.SMEM),
                  pl.BlockSpec((tm, TOP_K), lambda i: (i, 0)),
                  pl.BlockSpec((tm, d), lambda i: (i, 0)),
                  pl.BlockSpec((None, 1, d), lambda i: (i // per_seq, 0, 0)),
                  pl.BlockSpec((1, d), lambda i: (0, 0)),
                  pl.BlockSpec(memory_space=pl.ANY)],
        out_specs=pl.BlockSpec((tm, d), lambda i: (i, 0)),
        out_shape=jax.ShapeDtypeStruct((t, d), F32),
        scratch_shapes=[pltpu.VMEM((2, TOP_K, tm, PACK_W), U32),
                        pltpu.SemaphoreType.DMA((2,))],
        compiler_params=_cparams(("arbitrary",)),
        name="moe_combine",
    )(dest_t, dest_t, wts, x_mid, gt2, g_final, ys)


def _pad_to(a, axis, size):
    pad = [(0, 0)] * a.ndim
    pad[axis] = (0, size - a.shape[axis])
    return jnp.pad(a, pad)


def _in_weight(w_in):
    off_b = ZA_W
    off_cq = off_b + ZB_W
    off_ckv = off_cq + MLA_Q_RANK
    off_kr = off_ckv + MLA_KV_RANK
    off_d = off_kr + MLA_ROPE
    parts = [w_in[:, :off_cq],
             _pad_to(w_in[:, off_cq:off_ckv], 1, CQ_PAD),
             _pad_to(w_in[:, off_ckv:off_kr], 1, CKV_PAD),
             _pad_to(w_in[:, off_kr:off_d], 1, KR_PAD),
             w_in[:, off_d:]]
    return jnp.concatenate(parts, axis=1).astype(BF16)


def _mla_weights(w_uq, w_ukv):
    wq = w_uq.reshape(MLA_Q_RANK, N_HEADS, MLA_QK)
    wq = _pad_to(_pad_to(wq, 2, ATT_W), 0, CQ_PAD).reshape(CQ_PAD, N_HEADS * ATT_W)
    wkv = _pad_to(w_ukv.reshape(MLA_KV_RANK, N_HEADS, MLA_NOPE + HEAD_DIM), 0, CKV_PAD)
    wk = _pad_to(wkv[:, :, :MLA_NOPE], 2, ATT_W).reshape(CKV_PAD, N_HEADS * ATT_W)
    wv = wkv[:, :, MLA_NOPE:].reshape(CKV_PAD, MIX_W)
    return wq.astype(BF16), wk.astype(BF16), wv.astype(BF16)


def _lookup(table, idx):
    ids = jnp.arange(table.shape[0], dtype=I32).reshape((-1,) + (1,) * idx.ndim)
    vals = table.reshape(ids.shape)
    return jnp.sum(jnp.where(idx[None] == ids, vals, 0), axis=0)


def _route_tables(idx_t, rank_t, counts, p_rows):
    m = MOE_BLOCK
    padded = (counts + m - 1) // m * m
    pend = jnp.cumsum(padded)
    pstart = pend - padded
    dest_t = _lookup(pstart, idx_t) + rank_t
    blk_start = jnp.arange(p_rows // m, dtype=I32) * m
    block_e = jnp.minimum(jnp.sum(pend[None, :] <= blk_start[:, None], axis=1), N_EXPERTS - 1)
    block_e = block_e.astype(I32)
    block_valid = jnp.clip(_lookup(pstart + counts, block_e) - blk_start, 0, m).astype(I32)
    dest = dest_t.astype(I32).T.reshape(-1)
    return dest, block_e, block_valid


def kernel(x, c, positions, w_mod, b_mod, g_attn, w_in, g_gmlp_v, w_spatial, b_spatial, g_mla_q, w_mla_uq, g_mla_kv, w_mla_ukv, w_conv_dw, b_conv_dw, g_conv_ln, b_conv_ln, w_out, g_ffn, w_router, b_router, w_exp_gate, w_exp_up, w_exp_down, w_sh_gate, w_sh_up, w_sh_down, g_final):
    batch, seq, d = x.shape
    n_layers = w_mod.shape[0]
    t = batch * seq
    p_rows = t * TOP_K + N_EXPERTS * MOE_BLOCK

    xf = x.reshape(t, d)
    pos_col = positions.reshape(t, 1)
    pos_chunks = positions.reshape(batch, seq // ATT_TK, ATT_TK)
    slopes = 2.0 ** (-8.0 * jnp.arange(1, N_HEADS + 1, dtype=F32) / N_HEADS)
    half = MLA_ROPE // 2
    inv_freq = ROPE_THETA ** (-jnp.arange(half, dtype=F32) * (2.0 / MLA_ROPE))
    invf = _pad_to(jnp.concatenate([inv_freq, inv_freq]), 0, KR_PAD).reshape(1, KR_PAD)

    mod = _modulation(c, w_mod, b_mod).reshape(n_layers, N_MOD, batch, 1, d)

    routed = None
    for l in range(n_layers):
        sh1, sc1, gt1, sh2, sc2, gt2 = (mod[l, j] for j in range(N_MOD))
        if routed is not None:
            xf = _combine(*routed, seq=seq, final=False)

        qm, ka, va, zb, zc, zd = _in_projection(xf, g_attn[l].reshape(1, d), sc1, sh1,
                                                _in_weight(w_in[l]), seq)
        y_a = _moba_attention(qm, ka, va, pos_col, pos_chunks, slopes, batch, seq)
        y_b = _spatial_gating(zb, g_gmlp_v[l], w_spatial[l], b_spatial[l])
        wq, wk, wv = _mla_weights(w_mla_uq[l], w_mla_ukv[l])
        q, k, v = _mla_prep(zc, pos_col, invf,
                            _pad_to(g_mla_q[l], 0, CQ_PAD).reshape(1, CQ_PAD),
                            _pad_to(g_mla_kv[l], 0, CKV_PAD).reshape(1, CKV_PAD), wq, wk, wv)
        y_c = _mla_attention(q, k, v, batch, seq)
        y_d = _conformer_conv(zd, w_conv_dw[l].reshape(CONV_WIDTH, MIX_W),
                              b_conv_dw[l].reshape(1, MIX_W), g_conv_ln[l].reshape(1, MIX_W),
                              b_conv_ln[l].reshape(1, MIX_W), seq)

        x_mid, hp, idx_t, wts_t, rank_t, cnt = _out_projection(
            xf, (y_a, y_b, y_c, y_d), w_out[l].astype(BF16), gt1, g_ffn[l].reshape(1, d),
            sc2, sh2, gt2, w_router[l].T, b_router[l].reshape(N_EXPERTS, 1),
            w_sh_gate[l].astype(BF16), w_sh_up[l].astype(BF16), w_sh_down[l].astype(BF16), seq)

        dest, block_e, block_valid = _route_tables(idx_t, rank_t, cnt[:, 0], p_rows)
        xs = _dispatch(dest, hp, p_rows)
        ys = _experts(block_e, block_valid, xs, w_exp_gate, w_exp_up, w_exp_down, l)
        routed = (dest, wts_t.T, x_mid, gt2, g_final.reshape(1, d), ys)

    out = _combine(*routed, seq=seq, final=True)
    return out.reshape(batch, seq, d)
```

```python
import functools

import jax
import jax.numpy as jnp
from jax import lax
from jax.experimental import pallas as pl
from jax.experimental.pallas import tpu as pltpu

F32 = jnp.float32
BF16 = jnp.bfloat16
I32 = jnp.int32
U32 = jnp.uint32

D_MODEL = 2048
HEAD_DIM = 128
N_HEADS = 4
MIX_W = N_HEADS * HEAD_DIM
MOBA_BLOCK = 256
MOBA_TOPK = 3
GMLP_CHUNK = 128
MLA_Q_RANK = 448
MLA_KV_RANK = 160
MLA_NOPE = 128
MLA_ROPE = 64
MLA_QK = MLA_NOPE + MLA_ROPE
ROPE_THETA = 10000.0
CONV_WIDTH = 31
CONV_HALO = 32
SUBLANES = 8
N_EXPERTS = 64
TOP_K = 8
D_EXPERT = 512
D_SHARED = 512
ROUTED_SCALE = 2.5
MOE_BLOCK = 512
N_MOD = 6
EPS = 1e-6
NEG = -1e30
LOG2E = 1.4426950408889634

ATT_TQ = 2048
ATT_SUB = 256
ATT_TK = 1024
ATT_W = 2 * HEAD_DIM

ZA_W = 3 * MIX_W
ZB_W = 2 * MIX_W
CQ_PAD = 512
CKV_PAD = 256
KR_PAD = 128
ZC_W = CQ_PAD + CKV_PAD + KR_PAD
ZD_W = 2 * MIX_W

VMEM_LIMIT = 56 * 1024 * 1024
PACK_W = D_MODEL // 2


def _cparams(sem):
    return pltpu.CompilerParams(dimension_semantics=sem, vmem_limit_bytes=VMEM_LIMIT)


def _resident(shape, index_map):
    return pl.BlockSpec(shape, index_map, pipeline_mode=pl.Buffered(1))


def _dot(a, b):
    return jnp.dot(a, b, preferred_element_type=F32)


def _dot_nt(a, b, precision=None):
    return lax.dot_general(a, b, (((1,), (1,)), ((), ())), preferred_element_type=F32,
                           precision=precision)


def _pack_rows(y):
    half = y.shape[1] // 2
    lo = lax.bitcast_convert_type(y[:, :half].astype(BF16).astype(F32), U32)
    hi = lax.bitcast_convert_type(y[:, half:].astype(BF16).astype(F32), U32)
    return (lo >> 16) | hi


def _unpack_rows(u):
    lo = lax.bitcast_convert_type(u << 16, F32)
    hi = lax.bitcast_convert_type(u & jnp.uint32(0xFFFF0000), F32)
    return lo, hi


def _mod_kernel(c_ref, w_ref, b_ref, o_ref):
    c = c_ref[...]
    a = c * jax.nn.sigmoid(c)
    o_ref[...] = jnp.dot(a, w_ref[...], preferred_element_type=F32,
                         precision=lax.Precision.HIGHEST) + b_ref[...]


def _modulation(c, w_mod, b_mod):
    n_layers, d, _ = w_mod.shape
    b = c.shape[0]
    return pl.pallas_call(
        _mod_kernel,
        grid=(n_layers, N_MOD),
        in_specs=[pl.BlockSpec((b, d), lambda l, j: (0, 0)),
                  pl.BlockSpec((None, d, d), lambda l, j: (l, 0, j)),
                  pl.BlockSpec((None, None, 1, d), lambda l, j: (l, j, 0, 0))],
        out_specs=pl.BlockSpec((None, None, b, d), lambda l, j: (l, j, 0, 0)),
        out_shape=jax.ShapeDtypeStruct((n_layers, N_MOD, b, d), F32),
        compiler_params=_cparams(("arbitrary", "arbitrary")),
        name="modulation",
    )(c, w_mod, b_mod.reshape(n_layers, N_MOD, 1, d))


def _inproj_kernel(x_ref, g_ref, sc_ref, sh_ref, w_ref, q_ref, ka_ref, va_ref, zb_ref, zc_ref, zd_ref,
                   *, seq):
    tm = x_ref.shape[0]
    x = x_ref[...]
    ms = jnp.mean(x * x, axis=-1, keepdims=True)
    h = (x * lax.rsqrt(ms + EPS) * g_ref[...]) * (1.0 + sc_ref[...]) + sh_ref[...]
    hb = h.astype(BF16)

    q_ref[...] = _dot(hb, w_ref[:, 0:MIX_W]).astype(BF16)
    pos = (pl.program_id(0) * tm) % seq + lax.broadcasted_iota(I32, (tm, HEAD_DIM), 0)
    lane = lax.broadcasted_iota(I32, (tm, HEAD_DIM), 1)
    onehot = jnp.where(pos // MOBA_BLOCK == lane, 1.0, 0.0).astype(BF16)
    ones = jnp.ones((tm, HEAD_DIM), BF16)
    k = _dot(hb, w_ref[:, MIX_W:2 * MIX_W]).astype(BF16)
    v = _dot(hb, w_ref[:, 2 * MIX_W:ZA_W]).astype(BF16)
    for hd in range(N_HEADS):
        src = slice(hd * HEAD_DIM, (hd + 1) * HEAD_DIM)
        ka_ref[:, hd * ATT_W:hd * ATT_W + HEAD_DIM] = k[:, src]
        ka_ref[:, hd * ATT_W + HEAD_DIM:(hd + 1) * ATT_W] = onehot
        va_ref[:, hd * ATT_W:hd * ATT_W + HEAD_DIM] = v[:, src]
        va_ref[:, hd * ATT_W + HEAD_DIM:(hd + 1) * ATT_W] = ones

    off = ZA_W
    for ref in (zb_ref, zc_ref, zd_ref):
        w = ref.shape[1]
        ref[...] = _dot(hb, w_ref[:, off:off + w]).astype(BF16)
        off += w


def _in_projection(x, g, sc, sh, w_in_p, seq, tm=512):
    t, d = x.shape
    per_seq = seq // tm
    widths = (MIX_W, N_HEADS * ATT_W, N_HEADS * ATT_W, ZB_W, ZC_W, ZD_W)
    mod_spec = pl.BlockSpec((None, 1, d), lambda i: (i // per_seq, 0, 0))
    return pl.pallas_call(
        functools.partial(_inproj_kernel, seq=seq),
        grid=(t // tm,),
        in_specs=[pl.BlockSpec((tm, d), lambda i: (i, 0)),
                  pl.BlockSpec((1, d), lambda i: (0, 0)),
                  mod_spec, mod_spec,
                  _resident((d, ZA_W + ZB_W + ZC_W + ZD_W), lambda i: (0, 0))],
        out_specs=[pl.BlockSpec((tm, w), lambda i: (i, 0)) for w in widths],
        out_shape=[jax.ShapeDtypeStruct((t, w), BF16) for w in widths],
        compiler_params=_cparams(("arbitrary",)),
        name="in_projection",
    )(x, g, sc, sh, w_in_p)


def _attn_kernel(*refs, moba):
    if moba:
        slope_ref, q_ref, k_ref, v_ref, pq_ref, pk_ref, o_ref, kmean_ref, qa_ref, m_ref, acc_ref = refs
    else:
        q_ref, k_ref, v_ref, o_ref, m_ref, acc_ref = refs
        qa_ref = q_ref
    tq, sub, tk = ATT_TQ, ATT_SUB, ATT_TK
    nsub = tq // sub
    i = pl.program_id(2)

    if moba:
        nb_pad = kmean_ref.shape[0]
        nb = k_ref.shape[0] // MOBA_BLOCK

        @pl.when(i == 0)
        def _():
            kmean_ref[...] = jnp.zeros_like(kmean_ref)

            def body(n, carry):
                start = pl.multiple_of(n * MOBA_BLOCK, MOBA_BLOCK)
                kb = k_ref[pl.ds(start, MOBA_BLOCK), 0:HEAD_DIM].astype(F32)
                kmean_ref[pl.ds(n, 1), :] = jnp.mean(kb, axis=0, keepdims=True)
                return carry
            lax.fori_loop(0, nb, body, 0)

        slope = slope_ref[pl.program_id(1)] * LOG2E
        col = lax.broadcasted_iota(I32, (sub, nb_pad), 1)
        colf = col.astype(F32)
        pq_s = []
        for c in range(nsub):
            rows = slice(c * sub, (c + 1) * sub)
            own = i * nsub + c
            qf = q_ref[rows, :].astype(F32)
            gate = _dot_nt(qf, kmean_ref[...], precision=lax.Precision.HIGHEST)
            gate = jnp.where(col < own, gate, NEG)
            sel = col == own
            for _ in range(MOBA_TOPK):
                gmax = jnp.max(gate, axis=-1, keepdims=True)
                first = jnp.min(jnp.where(gate == gmax, colf, float(nb_pad)), axis=-1, keepdims=True)
                hit = colf == first
                sel = sel | (hit & (col < own))
                gate = jnp.where(hit, -jnp.inf, gate)
            qa_ref[rows, 0:HEAD_DIM] = (qf * (HEAD_DIM ** -0.5 * LOG2E)).astype(BF16)
            qa_ref[rows, HEAD_DIM:] = jnp.where(sel, 0.0, NEG).astype(BF16)
            pq_s.append(slope * pq_ref[rows, :].astype(F32))

    m_ref[...] = jnp.full(m_ref.shape, NEG, F32)
    acc_ref[...] = jnp.zeros_like(acc_ref)

    def step(c, chunk, mask_off):
        rows = slice(c * sub, (c + 1) * sub)
        start = pl.multiple_of(chunk * tk, tk)
        s = _dot_nt(qa_ref[rows, :], k_ref[pl.ds(start, tk), :])
        if moba:
            s = s - jnp.abs(pq_s[c] - slope * pk_ref[pl.ds(chunk, 1), :].astype(F32))
        if mask_off is not None:
            qi = lax.broadcasted_iota(I32, (sub, tk), 0)
            ki = lax.broadcasted_iota(I32, (sub, tk), 1)
            s = jnp.where(ki <= qi + mask_off, s, NEG)
        m_old = m_ref[rows, :]
        m_new = jnp.maximum(m_old, jnp.max(s, axis=-1, keepdims=True))
        p = jnp.exp2(s - m_new).astype(BF16)
        acc_ref[rows, :] = jnp.exp2(m_old - m_new) * acc_ref[rows, :] + _dot(p, v_ref[pl.ds(start, tk), :])
        m_ref[rows, :] = m_new

    per_tile = tq // tk
    for c in range(nsub):
        q_lo = c * sub
        for jj in reversed(range(per_tile)):
            k_lo = jj * tk
            if k_lo > q_lo + sub - 1:
                continue
            needs_mask = k_lo + tk - 1 > q_lo
            step(c, i * per_tile + jj, q_lo - k_lo if needs_mask else None)

    def body(j, carry):
        for c in range(nsub):
            step(c, j, None)
        return carry
    lax.fori_loop(0, i * per_tile, body, 0)

    acc = acc_ref[...]
    o_ref[...] = (acc[:, :HEAD_DIM] / acc[:, HEAD_DIM:]).astype(o_ref.dtype)


def _attn_scratch():
    return [pltpu.VMEM((ATT_TQ, 1), F32), pltpu.VMEM((ATT_TQ, ATT_W), F32)]


def _moba_attention(q, ka, va, pos_col, pos_chunks, slopes, batch, seq):
    t = q.shape[0]
    nq = seq // ATT_TQ
    kv_spec = pl.BlockSpec((seq, ATT_W), lambda b, h, i, s: (b, h))
    return pl.pallas_call(
        functools.partial(_attn_kernel, moba=True),
        grid_spec=pltpu.PrefetchScalarGridSpec(
            num_scalar_prefetch=1,
            grid=(batch, N_HEADS, nq),
            in_specs=[pl.BlockSpec((ATT_TQ, HEAD_DIM), lambda b, h, i, s: (b * nq + i, h)),
                      kv_spec, kv_spec,
                      pl.BlockSpec((ATT_TQ, 1), lambda b, h, i, s: (b * nq + i, 0)),
                      pl.BlockSpec((None, seq // ATT_TK, ATT_TK), lambda b, h, i, s: (b, 0, 0))],
            out_specs=pl.BlockSpec((ATT_TQ, HEAD_DIM), lambda b, h, i, s: (b * nq + i, h)),
            scratch_shapes=[pltpu.VMEM((HEAD_DIM, HEAD_DIM), F32),
                            pltpu.VMEM((ATT_TQ, ATT_W), BF16)] + _attn_scratch()),
        out_shape=jax.ShapeDtypeStruct((t, MIX_W), BF16),
        compiler_params=_cparams(("arbitrary", "arbitrary", "arbitrary")),
        name="moba_attention",
    )(slopes, q, ka, va, pos_col, pos_chunks)


def _mla_attention(q, k, va, batch, seq):
    t = q.shape[0]
    nq = seq // ATT_TQ
    kv_spec = pl.BlockSpec((seq, ATT_W), lambda b, h, i: (b, h))
    return pl.pallas_call(
        functools.partial(_attn_kernel, moba=False),
        grid=(batch, N_HEADS, nq),
        in_specs=[pl.BlockSpec((ATT_TQ, ATT_W), lambda b, h, i: (b * nq + i, h)),
                  kv_spec, kv_spec],
        out_specs=pl.BlockSpec((ATT_TQ, HEAD_DIM), lambda b, h, i: (b * nq + i, h)),
        out_shape=jax.ShapeDtypeStruct((t, MIX_W), BF16),
        scratch_shapes=_attn_scratch(),
        compiler_params=_cparams(("arbitrary", "arbitrary", "arbitrary")),
        name="mla_attention",
    )(q, k, va)


def _gelu_tanh(x):
    return 0.5 * x * (1.0 + jnp.tanh(0.7978845608028654 * (x + 0.044715 * x * x * x)))


def _gmlp_kernel(z_ref, gv_ref, ws_ref, bs_ref, o_ref):
    tm = z_ref.shape[0]
    ck = GMLP_CHUNK
    z = _gelu_tanh(z_ref[...].astype(F32))
    row = lax.broadcasted_iota(I32, (ck, ck), 0)
    colm = lax.broadcasted_iota(I32, (ck, ck), 1)
    for g in range(N_HEADS):
        lanes = slice(g * HEAD_DIM, (g + 1) * HEAD_DIM)
        u = z[:, lanes]
        vv = z[:, MIX_W + g * HEAD_DIM:MIX_W + (g + 1) * HEAD_DIM]
        ms = jnp.mean(vv * vv, axis=-1, keepdims=True)
        vn = (vv * lax.rsqrt(ms + EPS) * gv_ref[g:g + 1, :]).astype(BF16)
        w = jnp.where(colm <= row, ws_ref[g], 0.0).astype(BF16)
        bias = bs_ref[g]
        for c in range(tm // ck):
            rows = slice(c * ck, (c + 1) * ck)
            mixed = _dot(w, vn[rows]) + bias
            o_ref[rows, lanes] = (u[rows] * mixed).astype(o_ref.dtype)


def _spatial_gating(zb, g_v, w_s, b_s, tm=512):
    t = zb.shape[0]
    ck = GMLP_CHUNK
    return pl.pallas_call(
        _gmlp_kernel,
        grid=(t // tm,),
        in_specs=[pl.BlockSpec((tm, ZB_W), lambda i: (i, 0)),
                  pl.BlockSpec((N_HEADS, HEAD_DIM), lambda i: (0, 0)),
                  pl.BlockSpec((N_HEADS, ck, ck), lambda i: (0, 0, 0)),
                  pl.BlockSpec((N_HEADS, ck, 1), lambda i: (0, 0, 0))],
        out_specs=pl.BlockSpec((tm, MIX_W), lambda i: (i, 0)),
        out_shape=jax.ShapeDtypeStruct((t, MIX_W), BF16),
        compiler_params=_cparams(("arbitrary",)),
        name="spatial_gating",
    )(zb, g_v, w_s, b_s.reshape(N_HEADS, ck, 1))


def _mla_prep_kernel(z_ref, pos_ref, invf_ref, gq_ref, gkv_ref, wq_ref, wk_ref, wv_ref,
                     q_ref, k_ref, v_ref):
    tm = z_ref.shape[0]
    z = z_ref[...].astype(F32)
    cq = z[:, :CQ_PAD]
    ckv = z[:, CQ_PAD:CQ_PAD + CKV_PAD]
    kr = z[:, CQ_PAD + CKV_PAD:]
    qn = cq * lax.rsqrt(jnp.sum(cq * cq, -1, keepdims=True) * (1.0 / MLA_Q_RANK) + EPS)
    qn = (qn * gq_ref[...]).astype(BF16)
    kvn = ckv * lax.rsqrt(jnp.sum(ckv * ckv, -1, keepdims=True) * (1.0 / MLA_KV_RANK) + EPS)
    kvn = (kvn * gkv_ref[...]).astype(BF16)

    ang = pos_ref[...].astype(F32) * invf_ref[...]
    lane = lax.broadcasted_iota(I32, ang.shape, 1)
    half = MLA_ROPE // 2
    cos = jnp.cos(ang)
    sin = jnp.sin(ang)
    sin_lo = jnp.where(lane < half, -sin, 0.0)
    sin_hi = jnp.where((lane >= half) & (lane < 2 * half), sin, 0.0)

    def rope(r):
        return (r * cos + pltpu.roll(r, KR_PAD - half, 1) * sin_lo
                + pltpu.roll(r, half, 1) * sin_hi)

    q = _dot(qn, wq_ref[...])
    kn = _dot(kvn, wk_ref[...])
    v = _dot(kvn, wv_ref[...]).astype(BF16)
    k_rope = rope(kr)
    scale = MLA_QK ** -0.5 * LOG2E
    ones = jnp.ones((tm, HEAD_DIM), BF16)
    for h in range(N_HEADS):
        a = h * ATT_W
        b = a + MLA_NOPE
        q_ref[:, a:b] = (q[:, a:b] * scale).astype(q_ref.dtype)
        q_ref[:, b:a + ATT_W] = (rope(q[:, b:a + ATT_W]) * scale).astype(q_ref.dtype)
        k_ref[:, a:b] = kn[:, a:b].astype(k_ref.dtype)
        k_ref[:, b:a + ATT_W] = k_rope.astype(k_ref.dtype)
        v_ref[:, a:b] = v[:, h * HEAD_DIM:(h + 1) * HEAD_DIM]
        v_ref[:, b:a + ATT_W] = ones


def _mla_prep(zc, pos_col, invf, gq, gkv, wq, wk, wv, tm=512):
    t = zc.shape[0]
    qk_w = N_HEADS * ATT_W
    full = lambda i: (0, 0)
    out_spec = pl.BlockSpec((tm, qk_w), lambda i: (i, 0))
    out_shape = jax.ShapeDtypeStruct((t, qk_w), BF16)
    return pl.pallas_call(
        _mla_prep_kernel,
        grid=(t // tm,),
        in_specs=[pl.BlockSpec((tm, ZC_W), lambda i: (i, 0)),
                  pl.BlockSpec((tm, 1), lambda i: (i, 0)),
                  pl.BlockSpec((1, KR_PAD), full),
                  pl.BlockSpec((1, CQ_PAD), full),
                  pl.BlockSpec((1, CKV_PAD), full),
                  pl.BlockSpec((CQ_PAD, qk_w), full),
                  pl.BlockSpec((CKV_PAD, qk_w), full),
                  pl.BlockSpec((CKV_PAD, MIX_W), full)],
        out_specs=[out_spec, out_spec, out_spec],
        out_shape=[out_shape, out_shape, out_shape],
        compiler_params=_cparams(("arbitrary",)),
        name="mla_prep",
    )(zc, pos_col, invf, gq, gkv, wq, wk, wv)


def _glu(z):
    z = z.astype(F32)
    return z[:, :MIX_W] * jax.nn.sigmoid(z[:, MIX_W:])


def _conv_kernel(z_ref, zprev_ref, w_ref, b_ref, g_ref, beta_ref, o_ref, ybuf_ref, ysh_ref, *, per_seq):
    tm = z_ref.shape[0]
    first = (pl.program_id(0) % per_seq) == 0
    ybuf_ref[0:CONV_HALO, :] = jnp.where(first, 0.0, _glu(zprev_ref[...]))
    ybuf_ref[CONV_HALO:, :] = _glu(z_ref[...])
    span = tm + CONV_HALO - SUBLANES
    for o in range(1, SUBLANES):
        ysh_ref[o - 1, :, :] = ybuf_ref[pl.ds(o, span), :]
    rows = 64
    shift = CONV_HALO - (CONV_WIDTH - 1)
    for r in range(tm // rows):
        acc = jnp.zeros((rows, MIX_W), F32) + b_ref[...]
        for j in range(CONV_WIDTH):
            o = (shift + j) % SUBLANES
            base = r * rows + shift + j - o
            tap = ybuf_ref[pl.ds(base, rows), :] if o == 0 else ysh_ref[o - 1, pl.ds(base, rows), :]
            acc = acc + w_ref[j:j + 1, :] * tap
        mu = jnp.mean(acc, axis=-1, keepdims=True)
        xc = acc - mu
        y = xc * lax.rsqrt(jnp.mean(xc * xc, axis=-1, keepdims=True) + EPS)
        y = y * g_ref[...] + beta_ref[...]
        o_ref[r * rows:(r + 1) * rows, :] = (y * jax.nn.sigmoid(y)).astype(o_ref.dtype)


def _conformer_conv(zd, w_dw, b_dw, g_ln, b_ln, seq, tm=512):
    t = zd.shape[0]
    per_seq = seq // tm
    halo_blocks = tm // CONV_HALO
    full = lambda i: (0, 0)
    return pl.pallas_call(
        functools.partial(_conv_kernel, per_seq=per_seq),
        grid=(t // tm,),
        in_specs=[pl.BlockSpec((tm, ZD_W), lambda i: (i, 0)),
                  pl.BlockSpec((CONV_HALO, ZD_W),
                               lambda i: (jnp.maximum(i * halo_blocks - 1, 0), 0)),
                  pl.BlockSpec((CONV_WIDTH, MIX_W), full),
                  pl.BlockSpec((1, MIX_W), full),
                  pl.BlockSpec((1, MIX_W), full),
                  pl.BlockSpec((1, MIX_W), full)],
        out_specs=pl.BlockSpec((tm, MIX_W), lambda i: (i, 0)),
        out_shape=jax.ShapeDtypeStruct((t, MIX_W), BF16),
        scratch_shapes=[pltpu.VMEM((tm + CONV_HALO, MIX_W), F32),
                        pltpu.VMEM((SUBLANES - 1, tm + CONV_HALO - SUBLANES, MIX_W), F32)],
        compiler_params=_cparams(("arbitrary",)),
        name="conformer_conv",
    )(zd, zd, w_dw, b_dw, g_ln, b_ln)


def _outproj_kernel(x_ref, ya_ref, yb_ref, yc_ref, yd_ref, wo_ref, gt1_ref, g_ref, sc_ref, sh_ref,
                    gt2_ref, wr_ref, br_ref, wsg_ref, wsu_ref, wsd_ref,
                    xo_ref, hp_ref, idx_ref, wts_ref, rank_ref, cnt_ref, carry_ref):
    tm = x_ref.shape[0]

    @pl.when(pl.program_id(0) == 0)
    def _():
        carry_ref[...] = jnp.zeros_like(carry_ref)

    y = _dot(ya_ref[...], wo_ref[0:MIX_W, :])
    y = y + _dot(yb_ref[...], wo_ref[MIX_W:2 * MIX_W, :])
    y = y + _dot(yc_ref[...], wo_ref[2 * MIX_W:3 * MIX_W, :])
    y = y + _dot(yd_ref[...], wo_ref[3 * MIX_W:, :])
    x = x_ref[...] + gt1_ref[...] * y
    ms = jnp.mean(x * x, axis=-1, keepdims=True)
    h = (x * lax.rsqrt(ms + EPS) * g_ref[...]) * (1.0 + sc_ref[...]) + sh_ref[...]
    hb = h.astype(BF16)
    hp_ref[...] = _pack_rows(h)

    act = _dot(hb, wsg_ref[...])
    act = (act * jax.nn.sigmoid(act) * _dot(hb, wsu_ref[...])).astype(BF16)
    xo_ref[...] = x + gt2_ref[...] * _dot(act, wsd_ref[...])

    logits = _dot_nt(wr_ref[...], h, precision=lax.Precision.HIGHEST)
    scores = jax.nn.sigmoid(logits)
    cur = scores + br_ref[...]
    erow = lax.broadcasted_iota(I32, (N_EXPERTS, tm), 0).astype(F32)
    picked = jnp.zeros((N_EXPERTS, tm), F32)
    hits, idxs, wts = [], [], []
    for _ in range(TOP_K):
        cmax = jnp.max(cur, axis=0, keepdims=True)
        first = jnp.min(jnp.where(cur == cmax, erow, float(N_EXPERTS)), axis=0, keepdims=True)
        hit = erow == first
        hits.append(hit)
        idxs.append(first)
        wts.append(jnp.sum(jnp.where(hit, scores, 0.0), axis=0, keepdims=True))
        picked = jnp.where(hit, 1.0, picked)
        cur = jnp.where(hit, -jnp.inf, cur)
    wsum = wts[0]
    for w in wts[1:]:
        wsum = wsum + w
    ti = lax.broadcasted_iota(I32, (tm, tm), 0)
    tj = lax.broadcasted_iota(I32, (tm, tm), 1)
    before = jnp.where(ti < tj, 1.0, 0.0).astype(BF16)
    prior = _dot(picked.astype(BF16), before) + carry_ref[...]
    for k in range(TOP_K):
        idx_ref[k:k + 1, :] = idxs[k].astype(I32)
        wts_ref[k:k + 1, :] = wts[k] / wsum * ROUTED_SCALE
        rank_ref[k:k + 1, :] = jnp.sum(jnp.where(hits[k], prior, 0.0), axis=0,
                                       keepdims=True).astype(I32)
    carry_ref[...] = carry_ref[...] + jnp.sum(picked, axis=1, keepdims=True)
    cnt_ref[...] = jnp.broadcast_to(carry_ref[...], cnt_ref.shape).astype(I32)


def _out_projection(x, ys, w_out, gt1, g_ffn, sc2, sh2, gt2, w_rt, b_r, wsg, wsu, wsd, seq, tm=512):
    t, d = x.shape
    per_seq = seq // tm
    full = lambda i: (0, 0)
    mod_spec = pl.BlockSpec((None, 1, d), lambda i: (i // per_seq, 0, 0))
    y_spec = pl.BlockSpec((tm, MIX_W), lambda i: (i, 0))
    tok_spec = pl.BlockSpec((TOP_K, tm), lambda i: (0, i))
    return pl.pallas_call(
        _outproj_kernel,
        grid=(t // tm,),
        in_specs=[pl.BlockSpec((tm, d), lambda i: (i, 0)),
                  y_spec, y_spec, y_spec, y_spec,
                  _resident((4 * MIX_W, d), full),
                  mod_spec,
                  pl.BlockSpec((1, d), full),
                  mod_spec, mod_spec, mod_spec,
                  pl.BlockSpec((N_EXPERTS, d), full),
                  pl.BlockSpec((N_EXPERTS, 1), full),
                  _resident((d, D_SHARED), full),
                  _resident((d, D_SHARED), full),
                  _resident((D_SHARED, d), full)],
        out_specs=[pl.BlockSpec((tm, d), lambda i: (i, 0)),
                   pl.BlockSpec((tm, PACK_W), lambda i: (i, 0)),
                   tok_spec, tok_spec, tok_spec,
                   pl.BlockSpec((N_EXPERTS, 128), full)],
        out_shape=[jax.ShapeDtypeStruct((t, d), F32),
                   jax.ShapeDtypeStruct((t, PACK_W), U32),
                   jax.ShapeDtypeStruct((TOP_K, t), I32),
                   jax.ShapeDtypeStruct((TOP_K, t), F32),
                   jax.ShapeDtypeStruct((TOP_K, t), I32),
                   jax.ShapeDtypeStruct((N_EXPERTS, 128), I32)],
        scratch_shapes=[pltpu.VMEM((N_EXPERTS, 1), F32)],
        compiler_params=_cparams(("arbitrary",)),
        name="out_projection_router",
    )(x, *ys, w_out, gt1, g_ffn, sc2, sh2, gt2, w_rt, b_r, wsg, wsu, wsd)


def _row_copy(src_ref, src_row, dst_ref, dst_row, sem):
    return pltpu.make_async_copy(src_ref.at[pl.ds(src_row, 1)], dst_ref.at[pl.ds(dst_row, 1)], sem)


def _dispatch_kernel(dest_ref, hp_ref, xs_ref, sem):
    tm = hp_ref.shape[0]

    def start(t, carry):
        for k in range(TOP_K):
            _row_copy(hp_ref, t, xs_ref, dest_ref[t * TOP_K + k], sem).start()
        return carry
    lax.fori_loop(0, tm, start, 0)

    def wait(t, carry):
        for k in range(TOP_K):
            _row_copy(hp_ref, t, xs_ref, dest_ref[t * TOP_K + k], sem).wait()
        return carry
    lax.fori_loop(0, tm, wait, 0)


def _dispatch(dest_t, hp, p_rows, tm=512):
    t = hp.shape[0]
    return pl.pallas_call(
        _dispatch_kernel,
        grid=(t // tm,),
        in_specs=[pl.BlockSpec((tm * TOP_K,), lambda i: (i,), memory_space=pltpu.SMEM),
                  pl.BlockSpec((tm, PACK_W), lambda i: (i, 0))],
        out_specs=pl.BlockSpec(memory_space=pl.ANY),
        out_shape=jax.ShapeDtypeStruct((p_rows, PACK_W), U32),
        scratch_shapes=[pltpu.SemaphoreType.DMA(())],
        compiler_params=_cparams(("arbitrary",)),
        name="moe_dispatch",
    )(dest_t, hp)


def _expert_kernel(be_ref, bv_ref, x_ref, wg_ref, wu_ref, wd_ref, y_ref, wgb_ref, wub_ref, wdb_ref):
    i = pl.program_id(0)
    expert = be_ref[i]
    prev = be_ref[jnp.maximum(i - 1, 0)]

    @pl.when((i == 0) | (expert != prev))
    def _():
        wgb_ref[...] = wg_ref[...].astype(BF16)
        wub_ref[...] = wu_ref[...].astype(BF16)
        wdb_ref[...] = wd_ref[...].astype(BF16)

    valid = bv_ref[i]

    @pl.when(valid > 0)
    def _():
        lo, hi = _unpack_rows(x_ref[...])
        rows = lax.broadcasted_iota(I32, (x_ref.shape[0], 1), 0)
        xb = jnp.where(rows < valid, jnp.concatenate([lo, hi], axis=1), 0.0).astype(BF16)
        gate = _dot(xb, wgb_ref[...])
        act = (gate * jax.nn.sigmoid(gate) * _dot(xb, wub_ref[...])).astype(BF16)
        y_ref[...] = _pack_rows(_dot(act, wdb_ref[...]))

    @pl.when(valid == 0)
    def _():
        y_ref[...] = jnp.zeros_like(y_ref)


def _experts(block_e, block_valid, xs, w_gate, w_up, w_down, layer):
    p_rows = xs.shape[0]
    d = D_MODEL
    return pl.pallas_call(
        _expert_kernel,
        grid_spec=pltpu.PrefetchScalarGridSpec(
            num_scalar_prefetch=2,
            grid=(p_rows // MOE_BLOCK,),
            in_specs=[pl.BlockSpec((MOE_BLOCK, PACK_W), lambda i, be, bv: (i, 0)),
                      pl.BlockSpec((None, None, d, D_EXPERT), lambda i, be, bv: (layer, be[i], 0, 0)),
                      pl.BlockSpec((None, None, d, D_EXPERT), lambda i, be, bv: (layer, be[i], 0, 0)),
                      pl.BlockSpec((None, None, D_EXPERT, d), lambda i, be, bv: (layer, be[i], 0, 0))],
            out_specs=pl.BlockSpec((MOE_BLOCK, PACK_W), lambda i, be, bv: (i, 0)),
            scratch_shapes=[pltpu.VMEM((d, D_EXPERT), BF16),
                            pltpu.VMEM((d, D_EXPERT), BF16),
                            pltpu.VMEM((D_EXPERT, d), BF16)]),
        out_shape=jax.ShapeDtypeStruct((p_rows, PACK_W), U32),
        compiler_params=_cparams(("arbitrary",)),
        name="moe_experts",
    )(block_e, block_valid, xs, w_gate, w_up, w_down)


def _combine_kernel(dest_ref, wts_ref, x_ref, gt2_ref, gf_ref, ys_ref, o_ref, buf_ref, sem, *, final):
    tm = x_ref.shape[0]

    def start(g, carry):
        base = pl.multiple_of(g * SUBLANES, SUBLANES)
        for r in range(SUBLANES):
            for k in range(TOP_K):
                _row_copy(ys_ref, dest_ref[(base + r) * TOP_K + k], buf_ref.at[k], base + r,
                          sem).start()
        return carry
    lax.fori_loop(0, tm // SUBLANES, start, 0)

    def wait(t, carry):
        for k in range(TOP_K):
            _row_copy(ys_ref, dest_ref[t * TOP_K + k], buf_ref.at[k], t, sem).wait()
        return carry
    lax.fori_loop(0, tm, wait, 0)

    acc_lo = jnp.zeros((tm, PACK_W), F32)
    acc_hi = jnp.zeros((tm, PACK_W), F32)
    for k in range(TOP_K):
        lo, hi = _unpack_rows(buf_ref[k])
        w = wts_ref[:, k:k + 1]
        acc_lo = acc_lo + w * lo
        acc_hi = acc_hi + w * hi
    x = x_ref[...] + gt2_ref[...] * jnp.concatenate([acc_lo, acc_hi], axis=1)
    if final:
        ms = jnp.mean(x * x, axis=-1, keepdims=True)
        x = x * lax.rsqrt(ms + EPS) * gf_ref[...]
    o_ref[...] = x


def _combine(dest_t, wts, x_mid, gt2, g_final, ys, seq, final, tm=256):
    t, d = x_mid.shape
    per_seq = seq // tm
    return pl.pallas_call(
        functools.partial(_combine_kernel, final=final),
        grid=(t // tm,),
        in_specs=[pl.BlockSpec((tm * TOP_K,), lambda i: (i,), memory_space=pltpu.SMEM),
                  pl.BlockSpec((tm, TOP_K), lambda i: (i, 0)),
                  pl.BlockSpec((tm, d), lambda i: (i, 0)),
                  pl.BlockSpec((None, 1, d), lambda i: (i // per_seq, 0, 0)),
                  pl.BlockSpec((1, d), lambda i: (0, 0)),
                  pl.BlockSpec(memory_space=pl.ANY)],
        out_specs=pl.BlockSpec((tm, d), lambda i: (i, 0)),
        out_shape=jax.ShapeDtypeStruct((t, d), F32),
        scratch_shapes=[pltpu.VMEM((TOP_K, tm, PACK_W), U32),
                        pltpu.SemaphoreType.DMA(())],
        compiler_params=_cparams(("arbitrary",)),
        name="moe_combine",
    )(dest_t, wts, x_mid, gt2, g_final, ys)


def _pad_to(a, axis, size):
    pad = [(0, 0)] * a.ndim
    pad[axis] = (0, size - a.shape[axis])
    return jnp.pad(a, pad)


def _in_weight(w_in):
    off_b = ZA_W
    off_cq = off_b + ZB_W
    off_ckv = off_cq + MLA_Q_RANK
    off_kr = off_ckv + MLA_KV_RANK
    off_d = off_kr + MLA_ROPE
    parts = [w_in[:, :off_cq],
             _pad_to(w_in[:, off_cq:off_ckv], 1, CQ_PAD),
             _pad_to(w_in[:, off_ckv:off_kr], 1, CKV_PAD),
             _pad_to(w_in[:, off_kr:off_d], 1, KR_PAD),
             w_in[:, off_d:]]
    return jnp.concatenate(parts, axis=1).astype(BF16)


def _mla_weights(w_uq, w_ukv):
    wq = w_uq.reshape(MLA_Q_RANK, N_HEADS, MLA_QK)
    wq = _pad_to(_pad_to(wq, 2, ATT_W), 0, CQ_PAD).reshape(CQ_PAD, N_HEADS * ATT_W)
    wkv = _pad_to(w_ukv.reshape(MLA_KV_RANK, N_HEADS, MLA_NOPE + HEAD_DIM), 0, CKV_PAD)
    wk = _pad_to(wkv[:, :, :MLA_NOPE], 2, ATT_W).reshape(CKV_PAD, N_HEADS * ATT_W)
    wv = wkv[:, :, MLA_NOPE:].reshape(CKV_PAD, MIX_W)
    return wq.astype(BF16), wk.astype(BF16), wv.astype(BF16)


def _lookup(table, idx):
    ids = jnp.arange(table.shape[0], dtype=I32).reshape((-1,) + (1,) * idx.ndim)
    vals = table.reshape(ids.shape)
    return jnp.sum(jnp.where(idx[None] == ids, vals, 0), axis=0)


def _route_tables(idx_t, rank_t, counts, p_rows):
    m = MOE_BLOCK
    padded = (counts + m - 1) // m * m
    pend = jnp.cumsum(padded)
    pstart = pend - padded
    dest_t = _lookup(pstart, idx_t) + rank_t
    blk_start = jnp.arange(p_rows // m, dtype=I32) * m
    block_e = jnp.minimum(jnp.sum(pend[None, :] <= blk_start[:, None], axis=1), N_EXPERTS - 1)
    block_e = block_e.astype(I32)
    block_valid = jnp.clip(_lookup(pstart + counts, block_e) - blk_start, 0, m).astype(I32)
    dest = dest_t.astype(I32).T.reshape(-1)
    return dest, block_e, block_valid


def kernel(x, c, positions, w_mod, b_mod, g_attn, w_in, g_gmlp_v, w_spatial, b_spatial, g_mla_q, w_mla_uq, g_mla_kv, w_mla_ukv, w_conv_dw, b_conv_dw, g_conv_ln, b_conv_ln, w_out, g_ffn, w_router, b_router, w_exp_gate, w_exp_up, w_exp_down, w_sh_gate, w_sh_up, w_sh_down, g_final):
    batch, seq, d = x.shape
    n_layers = w_mod.shape[0]
    t = batch * seq
    p_rows = t * TOP_K + N_EXPERTS * MOE_BLOCK

    xf = x.reshape(t, d)
    pos_col = positions.reshape(t, 1)
    pos_chunks = positions.reshape(batch, seq // ATT_TK, ATT_TK)
    slopes = 2.0 ** (-8.0 * jnp.arange(1, N_HEADS + 1, dtype=F32) / N_HEADS)
    half = MLA_ROPE // 2
    inv_freq = ROPE_THETA ** (-jnp.arange(half, dtype=F32) * (2.0 / MLA_ROPE))
    invf = _pad_to(jnp.concatenate([inv_freq, inv_freq]), 0, KR_PAD).reshape(1, KR_PAD)

    mod = _modulation(c, w_mod, b_mod).reshape(n_layers, N_MOD, batch, 1, d)

    routed = None
    for l in range(n_layers):
        sh1, sc1, gt1, sh2, sc2, gt2 = (mod[l, j] for j in range(N_MOD))
        if routed is not None:
            xf = _combine(*routed, seq=seq, final=False)

        qm, ka, va, zb, zc, zd = _in_projection(xf, g_attn[l].reshape(1, d), sc1, sh1,
                                                _in_weight(w_in[l]), seq)
        y_a = _moba_attention(qm, ka, va, pos_col, pos_chunks, slopes, batch, seq)
        y_b = _spatial_gating(zb, g_gmlp_v[l], w_spatial[l], b_spatial[l])
        wq, wk, wv = _mla_weights(w_mla_uq[l], w_mla_ukv[l])
        q, k, v = _mla_prep(zc, pos_col, invf,
                            _pad_to(g_mla_q[l], 0, CQ_PAD).reshape(1, CQ_PAD),
                            _pad_to(g_mla_kv[l], 0, CKV_PAD).reshape(1, CKV_PAD), wq, wk, wv)
        y_c = _mla_attention(q, k, v, batch, seq)
        y_d = _conformer_conv(zd, w_conv_dw[l].reshape(CONV_WIDTH, MIX_W),
                              b_conv_dw[l].reshape(1, MIX_W), g_conv_ln[l].reshape(1, MIX_W),
                              b_conv_ln[l].reshape(1, MIX_W), seq)

        x_mid, hp, idx_t, wts_t, rank_t, cnt = _out_projection(
            xf, (y_a, y_b, y_c, y_d), w_out[l].astype(BF16), gt1, g_ffn[l].reshape(1, d),
            sc2, sh2, gt2, w_router[l].T, b_router[l].reshape(N_EXPERTS, 1),
            w_sh_gate[l].astype(BF16), w_sh_up[l].astype(BF16), w_sh_down[l].astype(BF16), seq)

        dest, block_e, block_valid = _route_tables(idx_t, rank_t, cnt[:, 0], p_rows)
        xs = _dispatch(dest, hp, p_rows)
        ys = _experts(block_e, block_valid, xs, w_exp_gate, w_exp_up, w_exp_down, l)
        routed = (dest, wts_t.T, x_mid, gt2, g_final.reshape(1, d), ys)

    out = _combine(*routed, seq=seq, final=True)
    return out.reshape(batch, seq, d)
```

```python
import functools

import jax
import jax.numpy as jnp
from jax import lax
from jax.experimental import pallas as pl
from jax.experimental.pallas import tpu as pltpu

F32 = jnp.float32
BF16 = jnp.bfloat16
I32 = jnp.int32
U32 = jnp.uint32

D_MODEL = 2048
HEAD_DIM = 128
N_HEADS = 4
MIX_W = N_HEADS * HEAD_DIM
MOBA_BLOCK = 256
MOBA_TOPK = 3
GMLP_CHUNK = 128
MLA_Q_RANK = 448
MLA_KV_RANK = 160
MLA_NOPE = 128
MLA_ROPE = 64
MLA_QK = MLA_NOPE + MLA_ROPE
ROPE_THETA = 10000.0
CONV_WIDTH = 31
CONV_HALO = 32
SUBLANES = 8
N_EXPERTS = 64
TOP_K = 8
D_EXPERT = 512
D_SHARED = 512
ROUTED_SCALE = 2.5
MOE_BLOCK = 512
N_MOD = 6
EPS = 1e-6
NEG = -1e30
LOG2E = 1.4426950408889634

ATT_TQ = 2048
ATT_SUB = 256
ATT_TK = 1024
ATT_W = 2 * HEAD_DIM

ZA_W = 3 * MIX_W
ZB_W = 2 * MIX_W
CQ_PAD = 512
CKV_PAD = 256
KR_PAD = 128
ZC_W = CQ_PAD + CKV_PAD + KR_PAD
ZD_W = 2 * MIX_W

VMEM_LIMIT = 56 * 1024 * 1024
PACK_W = D_MODEL // 2


def _cparams(sem):
    return pltpu.CompilerParams(dimension_semantics=sem, vmem_limit_bytes=VMEM_LIMIT)


def _resident(shape, index_map):
    return pl.BlockSpec(shape, index_map, pipeline_mode=pl.Buffered(1))


def _dot(a, b):
    return jnp.dot(a, b, preferred_element_type=F32)


def _dot_nt(a, b, precision=None):
    return lax.dot_general(a, b, (((1,), (1,)), ((), ())), preferred_element_type=F32,
                           precision=precision)


def _pack_rows(y):
    half = y.shape[1] // 2
    lo = lax.bitcast_convert_type(y[:, :half].astype(BF16).astype(F32), U32)
    hi = lax.bitcast_convert_type(y[:, half:].astype(BF16).astype(F32), U32)
    return (lo >> 16) | hi


def _unpack_rows(u):
    lo = lax.bitcast_convert_type(u << 16, F32)
    hi = lax.bitcast_convert_type(u & jnp.uint32(0xFFFF0000), F32)
    return lo, hi


def _mod_kernel(c_ref, w_ref, b_ref, o_ref):
    c = c_ref[...]
    a = c * jax.nn.sigmoid(c)
    o_ref[...] = jnp.dot(a, w_ref[...], preferred_element_type=F32,
                         precision=lax.Precision.HIGHEST) + b_ref[...]


def _modulation(c, w_mod, b_mod):
    n_layers, d, _ = w_mod.shape
    b = c.shape[0]
    return pl.pallas_call(
        _mod_kernel,
        grid=(n_layers, N_MOD),
        in_specs=[pl.BlockSpec((b, d), lambda l, j: (0, 0)),
                  pl.BlockSpec((None, d, d), lambda l, j: (l, 0, j)),
                  pl.BlockSpec((None, None, 1, d), lambda l, j: (l, j, 0, 0))],
        out_specs=pl.BlockSpec((None, None, b, d), lambda l, j: (l, j, 0, 0)),
        out_shape=jax.ShapeDtypeStruct((n_layers, N_MOD, b, d), F32),
        compiler_params=_cparams(("arbitrary", "arbitrary")),
        name="modulation",
    )(c, w_mod, b_mod.reshape(n_layers, N_MOD, 1, d))


def _inproj_kernel(x_ref, g_ref, sc_ref, sh_ref, w_ref, q_ref, ka_ref, va_ref, zb_ref, zc_ref, zd_ref,
                   *, seq):
    tm = x_ref.shape[0]
    x = x_ref[...]
    ms = jnp.mean(x * x, axis=-1, keepdims=True)
    h = (x * lax.rsqrt(ms + EPS) * g_ref[...]) * (1.0 + sc_ref[...]) + sh_ref[...]
    hb = h.astype(BF16)

    q_ref[...] = _dot(hb, w_ref[:, 0:MIX_W]).astype(BF16)
    pos = (pl.program_id(0) * tm) % seq + lax.broadcasted_iota(I32, (tm, HEAD_DIM), 0)
    lane = lax.broadcasted_iota(I32, (tm, HEAD_DIM), 1)
    onehot = jnp.where(pos // MOBA_BLOCK == lane, 1.0, 0.0).astype(BF16)
    ones = jnp.ones((tm, HEAD_DIM), BF16)
    k = _dot(hb, w_ref[:, MIX_W:2 * MIX_W]).astype(BF16)
    v = _dot(hb, w_ref[:, 2 * MIX_W:ZA_W]).astype(BF16)
    for hd in range(N_HEADS):
        src = slice(hd * HEAD_DIM, (hd + 1) * HEAD_DIM)
        ka_ref[:, hd * ATT_W:hd * ATT_W + HEAD_DIM] = k[:, src]
        ka_ref[:, hd * ATT_W + HEAD_DIM:(hd + 1) * ATT_W] = onehot
        va_ref[:, hd * ATT_W:hd * ATT_W + HEAD_DIM] = v[:, src]
        va_ref[:, hd * ATT_W + HEAD_DIM:(hd + 1) * ATT_W] = ones

    off = ZA_W
    for ref in (zb_ref, zc_ref, zd_ref):
        w = ref.shape[1]
        ref[...] = _dot(hb, w_ref[:, off:off + w]).astype(BF16)
        off += w


def _in_projection(x, g, sc, sh, w_in_p, seq, tm=512):
    t, d = x.shape
    per_seq = seq // tm
    widths = (MIX_W, N_HEADS * ATT_W, N_HEADS * ATT_W, ZB_W, ZC_W, ZD_W)
    mod_spec = pl.BlockSpec((None, 1, d), lambda i: (i // per_seq, 0, 0))
    return pl.pallas_call(
        functools.partial(_inproj_kernel, seq=seq),
        grid=(t // tm,),
        in_specs=[pl.BlockSpec((tm, d), lambda i: (i, 0)),
                  pl.BlockSpec((1, d), lambda i: (0, 0)),
                  mod_spec, mod_spec,
                  _resident((d, ZA_W + ZB_W + ZC_W + ZD_W), lambda i: (0, 0))],
        out_specs=[pl.BlockSpec((tm, w), lambda i: (i, 0)) for w in widths],
        out_shape=[jax.ShapeDtypeStruct((t, w), BF16) for w in widths],
        compiler_params=_cparams(("arbitrary",)),
        name="in_projection",
    )(x, g, sc, sh, w_in_p)


def _attn_kernel(*refs, moba):
    if moba:
        slope_ref, q_ref, k_ref, v_ref, pq_ref, pk_ref, o_ref, kmean_ref, qa_ref, m_ref, acc_ref = refs
    else:
        q_ref, k_ref, v_ref, o_ref, m_ref, acc_ref = refs
        qa_ref = q_ref
    tq, sub, tk = ATT_TQ, ATT_SUB, ATT_TK
    nsub = tq // sub
    i = pl.program_id(2)

    if moba:
        nb_pad = kmean_ref.shape[0]
        nb = k_ref.shape[0] // MOBA_BLOCK

        @pl.when(i == 0)
        def _():
            kmean_ref[...] = jnp.zeros_like(kmean_ref)

            def body(n, carry):
                start = pl.multiple_of(n * MOBA_BLOCK, MOBA_BLOCK)
                kb = k_ref[pl.ds(start, MOBA_BLOCK), 0:HEAD_DIM].astype(F32)
                kmean_ref[pl.ds(n, 1), :] = jnp.mean(kb, axis=0, keepdims=True)
                return carry
            lax.fori_loop(0, nb, body, 0)

        slope = slope_ref[pl.program_id(1)] * LOG2E
        col = lax.broadcasted_iota(I32, (sub, nb_pad), 1)
        colf = col.astype(F32)
        pq_s = []
        for c in range(nsub):
            rows = slice(c * sub, (c + 1) * sub)
            own = i * nsub + c
            qf = q_ref[rows, :].astype(F32)
            gate = _dot_nt(qf, kmean_ref[...], precision=lax.Precision.HIGHEST)
            gate = jnp.where(col < own, gate, NEG)
            sel = col == own
            for _ in range(MOBA_TOPK):
                gmax = jnp.max(gate, axis=-1, keepdims=True)
                first = jnp.min(jnp.where(gate == gmax, colf, float(nb_pad)), axis=-1, keepdims=True)
                hit = colf == first
                sel = sel | (hit & (col < own))
                gate = jnp.where(hit, -jnp.inf, gate)
            qa_ref[rows, 0:HEAD_DIM] = (qf * (HEAD_DIM ** -0.5 * LOG2E)).astype(BF16)
            qa_ref[rows, HEAD_DIM:] = jnp.where(sel, 0.0, NEG).astype(BF16)
            pq_s.append(slope * pq_ref[rows, :].astype(F32))

    m_ref[...] = jnp.full(m_ref.shape, NEG, F32)
    acc_ref[...] = jnp.zeros_like(acc_ref)

    def step(c, chunk, mask_off):
        rows = slice(c * sub, (c + 1) * sub)
        start = pl.multiple_of(chunk * tk, tk)
        s = _dot_nt(qa_ref[rows, :], k_ref[pl.ds(start, tk), :])
        if moba:
            s = s - jnp.abs(pq_s[c] - slope * pk_ref[pl.ds(chunk, 1), :].astype(F32))
        if mask_off is not None:
            qi = lax.broadcasted_iota(I32, (sub, tk), 0)
            ki = lax.broadcasted_iota(I32, (sub, tk), 1)
            s = jnp.where(ki <= qi + mask_off, s, NEG)
        m_old = m_ref[rows, :]
        m_new = jnp.maximum(m_old, jnp.max(s, axis=-1, keepdims=True))
        p = jnp.exp2(s - m_new).astype(BF16)
        acc_ref[rows, :] = jnp.exp2(m_old - m_new) * acc_ref[rows, :] + _dot(p, v_ref[pl.ds(start, tk), :])
        m_ref[rows, :] = m_new

    per_tile = tq // tk
    for c in range(nsub):
        q_lo = c * sub
        for jj in reversed(range(per_tile)):
            k_lo = jj * tk
            if k_lo > q_lo + sub - 1:
                continue
            needs_mask = k_lo + tk - 1 > q_lo
            step(c, i * per_tile + jj, q_lo - k_lo if needs_mask else None)

    def body(j, carry):
        for c in range(nsub):
            step(c, j, None)
        return carry
    lax.fori_loop(0, i * per_tile, body, 0)

    acc = acc_ref[...]
    o_ref[...] = (acc[:, :HEAD_DIM] / acc[:, HEAD_DIM:]).astype(o_ref.dtype)


def _attn_scratch():
    return [pltpu.VMEM((ATT_TQ, 1), F32), pltpu.VMEM((ATT_TQ, ATT_W), F32)]


def _moba_attention(q, ka, va, pos_col, pos_chunks, slopes, batch, seq):
    t = q.shape[0]
    nq = seq // ATT_TQ
    kv_spec = pl.BlockSpec((seq, ATT_W), lambda b, h, i, s: (b, h))
    return pl.pallas_call(
        functools.partial(_attn_kernel, moba=True),
        grid_spec=pltpu.PrefetchScalarGridSpec(
            num_scalar_prefetch=1,
            grid=(batch, N_HEADS, nq),
            in_specs=[pl.BlockSpec((ATT_TQ, HEAD_DIM), lambda b, h, i, s: (b * nq + i, h)),
                      kv_spec, kv_spec,
                      pl.BlockSpec((ATT_TQ, 1), lambda b, h, i, s: (b * nq + i, 0)),
                      pl.BlockSpec((None, seq // ATT_TK, ATT_TK), lambda b, h, i, s: (b, 0, 0))],
            out_specs=pl.BlockSpec((ATT_TQ, HEAD_DIM), lambda b, h, i, s: (b * nq + i, h)),
            scratch_shapes=[pltpu.VMEM((HEAD_DIM, HEAD_DIM), F32),
                            pltpu.VMEM((ATT_TQ, ATT_W), BF16)] + _attn_scratch()),
        out_shape=jax.ShapeDtypeStruct((t, MIX_W), BF16),
        compiler_params=_cparams(("arbitrary", "arbitrary", "arbitrary")),
        name="moba_attention",
    )(slopes, q, ka, va, pos_col, pos_chunks)


def _mla_attention(q, k, va, batch, seq):
    t = q.shape[0]
    nq = seq // ATT_TQ
    kv_spec = pl.BlockSpec((seq, ATT_W), lambda b, h, i: (b, h))
    return pl.pallas_call(
        functools.partial(_attn_kernel, moba=False),
        grid=(batch, N_HEADS, nq),
        in_specs=[pl.BlockSpec((ATT_TQ, ATT_W), lambda b, h, i: (b * nq + i, h)),
                  kv_spec, kv_spec],
        out_specs=pl.BlockSpec((ATT_TQ, HEAD_DIM), lambda b, h, i: (b * nq + i, h)),
        out_shape=jax.ShapeDtypeStruct((t, MIX_W), BF16),
        scratch_shapes=_attn_scratch(),
        compiler_params=_cparams(("arbitrary", "arbitrary", "arbitrary")),
        name="mla_attention",
    )(q, k, va)


def _gelu_tanh(x):
    return 0.5 * x * (1.0 + jnp.tanh(0.7978845608028654 * (x + 0.044715 * x * x * x)))


def _gmlp_kernel(z_ref, gv_ref, ws_ref, bs_ref, o_ref):
    tm = z_ref.shape[0]
    ck = GMLP_CHUNK
    z = _gelu_tanh(z_ref[...].astype(F32))
    row = lax.broadcasted_iota(I32, (ck, ck), 0)
    colm = lax.broadcasted_iota(I32, (ck, ck), 1)
    for g in range(N_HEADS):
        lanes = slice(g * HEAD_DIM, (g + 1) * HEAD_DIM)
        u = z[:, lanes]
        vv = z[:, MIX_W + g * HEAD_DIM:MIX_W + (g + 1) * HEAD_DIM]
        ms = jnp.mean(vv * vv, axis=-1, keepdims=True)
        vn = (vv * lax.rsqrt(ms + EPS) * gv_ref[g:g + 1, :]).astype(BF16)
        w = jnp.where(colm <= row, ws_ref[g], 0.0).astype(BF16)
        bias = bs_ref[g]
        for c in range(tm // ck):
            rows = slice(c * ck, (c + 1) * ck)
            mixed = _dot(w, vn[rows]) + bias
            o_ref[rows, lanes] = (u[rows] * mixed).astype(o_ref.dtype)


def _spatial_gating(zb, g_v, w_s, b_s, tm=512):
    t = zb.shape[0]
    ck = GMLP_CHUNK
    return pl.pallas_call(
        _gmlp_kernel,
        grid=(t // tm,),
        in_specs=[pl.BlockSpec((tm, ZB_W), lambda i: (i, 0)),
                  pl.BlockSpec((N_HEADS, HEAD_DIM), lambda i: (0, 0)),
                  pl.BlockSpec((N_HEADS, ck, ck), lambda i: (0, 0, 0)),
                  pl.BlockSpec((N_HEADS, ck, 1), lambda i: (0, 0, 0))],
        out_specs=pl.BlockSpec((tm, MIX_W), lambda i: (i, 0)),
        out_shape=jax.ShapeDtypeStruct((t, MIX_W), BF16),
        compiler_params=_cparams(("arbitrary",)),
        name="spatial_gating",
    )(zb, g_v, w_s, b_s.reshape(N_HEADS, ck, 1))


def _mla_prep_kernel(z_ref, pos_ref, invf_ref, gq_ref, gkv_ref, wq_ref, wk_ref, wv_ref,
                     q_ref, k_ref, v_ref):
    tm = z_ref.shape[0]
    z = z_ref[...].astype(F32)
    cq = z[:, :CQ_PAD]
    ckv = z[:, CQ_PAD:CQ_PAD + CKV_PAD]
    kr = z[:, CQ_PAD + CKV_PAD:]
    qn = cq * lax.rsqrt(jnp.sum(cq * cq, -1, keepdims=True) * (1.0 / MLA_Q_RANK) + EPS)
    qn = (qn * gq_ref[...]).astype(BF16)
    kvn = ckv * lax.rsqrt(jnp.sum(ckv * ckv, -1, keepdims=True) * (1.0 / MLA_KV_RANK) + EPS)
    kvn = (kvn * gkv_ref[...]).astype(BF16)

    ang = pos_ref[...].astype(F32) * invf_ref[...]
    lane = lax.broadcasted_iota(I32, ang.shape, 1)
    half = MLA_ROPE // 2
    cos = jnp.cos(ang)
    sin = jnp.sin(ang)
    sin_lo = jnp.where(lane < half, -sin, 0.0)
    sin_hi = jnp.where((lane >= half) & (lane < 2 * half), sin, 0.0)

    def rope(r):
        return (r * cos + pltpu.roll(r, KR_PAD - half, 1) * sin_lo
                + pltpu.roll(r, half, 1) * sin_hi)

    q = _dot(qn, wq_ref[...])
    kn = _dot(kvn, wk_ref[...])
    v = _dot(kvn, wv_ref[...]).astype(BF16)
    k_rope = rope(kr)
    scale = MLA_QK ** -0.5 * LOG2E
    ones = jnp.ones((tm, HEAD_DIM), BF16)
    for h in range(N_HEADS):
        a = h * ATT_W
        b = a + MLA_NOPE
        q_ref[:, a:b] = (q[:, a:b] * scale).astype(q_ref.dtype)
        q_ref[:, b:a + ATT_W] = (rope(q[:, b:a + ATT_W]) * scale).astype(q_ref.dtype)
        k_ref[:, a:b] = kn[:, a:b].astype(k_ref.dtype)
        k_ref[:, b:a + ATT_W] = k_rope.astype(k_ref.dtype)
        v_ref[:, a:b] = v[:, h * HEAD_DIM:(h + 1) * HEAD_DIM]
        v_ref[:, b:a + ATT_W] = ones


def _mla_prep(zc, pos_col, invf, gq, gkv, wq, wk, wv, tm=512):
    t = zc.shape[0]
    qk_w = N_HEADS * ATT_W
    full = lambda i: (0, 0)
    out_spec = pl.BlockSpec((tm, qk_w), lambda i: (i, 0))
    out_shape = jax.ShapeDtypeStruct((t, qk_w), BF16)
    return pl.pallas_call(
        _mla_prep_kernel,
        grid=(t // tm,),
        in_specs=[pl.BlockSpec((tm, ZC_W), lambda i: (i, 0)),
                  pl.BlockSpec((tm, 1), lambda i: (i, 0)),
                  pl.BlockSpec((1, KR_PAD), full),
                  pl.BlockSpec((1, CQ_PAD), full),
                  pl.BlockSpec((1, CKV_PAD), full),
                  pl.BlockSpec((CQ_PAD, qk_w), full),
                  pl.BlockSpec((CKV_PAD, qk_w), full),
                  pl.BlockSpec((CKV_PAD, MIX_W), full)],
        out_specs=[out_spec, out_spec, out_spec],
        out_shape=[out_shape, out_shape, out_shape],
        compiler_params=_cparams(("arbitrary",)),
        name="mla_prep",
    )(zc, pos_col, invf, gq, gkv, wq, wk, wv)


def _glu(z):
    z = z.astype(F32)
    return z[:, :MIX_W] * jax.nn.sigmoid(z[:, MIX_W:])


def _conv_kernel(z_ref, zprev_ref, w_ref, b_ref, g_ref, beta_ref, o_ref, ybuf_ref, ysh_ref, *, per_seq):
    tm = z_ref.shape[0]
    first = (pl.program_id(0) % per_seq) == 0
    ybuf_ref[0:CONV_HALO, :] = jnp.where(first, 0.0, _glu(zprev_ref[...]))
    ybuf_ref[CONV_HALO:, :] = _glu(z_ref[...])
    span = tm + CONV_HALO - SUBLANES
    for o in range(1, SUBLANES):
        ysh_ref[o - 1, :, :] = ybuf_ref[pl.ds(o, span), :]
    rows = 64
    shift = CONV_HALO - (CONV_WIDTH - 1)
    for r in range(tm // rows):
        acc = jnp.zeros((rows, MIX_W), F32) + b_ref[...]
        for j in range(CONV_WIDTH):
            o = (shift + j) % SUBLANES
            base = r * rows + shift + j - o
            tap = ybuf_ref[pl.ds(base, rows), :] if o == 0 else ysh_ref[o - 1, pl.ds(base, rows), :]
            acc = acc + w_ref[j:j + 1, :] * tap
        mu = jnp.mean(acc, axis=-1, keepdims=True)
        xc = acc - mu
        y = xc * lax.rsqrt(jnp.mean(xc * xc, axis=-1, keepdims=True) + EPS)
        y = y * g_ref[...] + beta_ref[...]
        o_ref[r * rows:(r + 1) * rows, :] = (y * jax.nn.sigmoid(y)).astype(o_ref.dtype)


def _conformer_conv(zd, w_dw, b_dw, g_ln, b_ln, seq, tm=512):
    t = zd.shape[0]
    per_seq = seq // tm
    halo_blocks = tm // CONV_HALO
    full = lambda i: (0, 0)
    return pl.pallas_call(
        functools.partial(_conv_kernel, per_seq=per_seq),
        grid=(t // tm,),
        in_specs=[pl.BlockSpec((tm, ZD_W), lambda i: (i, 0)),
                  pl.BlockSpec((CONV_HALO, ZD_W),
                               lambda i: (jnp.maximum(i * halo_blocks - 1, 0), 0)),
                  pl.BlockSpec((CONV_WIDTH, MIX_W), full),
                  pl.BlockSpec((1, MIX_W), full),
                  pl.BlockSpec((1, MIX_W), full),
                  pl.BlockSpec((1, MIX_W), full)],
        out_specs=pl.BlockSpec((tm, MIX_W), lambda i: (i, 0)),
        out_shape=jax.ShapeDtypeStruct((t, MIX_W), BF16),
        scratch_shapes=[pltpu.VMEM((tm + CONV_HALO, MIX_W), F32),
                        pltpu.VMEM((SUBLANES - 1, tm + CONV_HALO - SUBLANES, MIX_W), F32)],
        compiler_params=_cparams(("arbitrary",)),
        name="conformer_conv",
    )(zd, zd, w_dw, b_dw, g_ln, b_ln)


def _outproj_kernel(x_ref, ya_ref, yb_ref, yc_ref, yd_ref, wo_ref, gt1_ref, g_ref, sc_ref, sh_ref,
                    gt2_ref, wr_ref, br_ref, wsg_ref, wsu_ref, wsd_ref,
                    xo_ref, hp_ref, idx_ref, wts_ref, rank_ref, cnt_ref, carry_ref):
    tm = x_ref.shape[0]

    @pl.when(pl.program_id(0) == 0)
    def _():
        carry_ref[...] = jnp.zeros_like(carry_ref)

    y = _dot(ya_ref[...], wo_ref[0:MIX_W, :])
    y = y + _dot(yb_ref[...], wo_ref[MIX_W:2 * MIX_W, :])
    y = y + _dot(yc_ref[...], wo_ref[2 * MIX_W:3 * MIX_W, :])
    y = y + _dot(yd_ref[...], wo_ref[3 * MIX_W:, :])
    x = x_ref[...] + gt1_ref[...] * y
    ms = jnp.mean(x * x, axis=-1, keepdims=True)
    h = (x * lax.rsqrt(ms + EPS) * g_ref[...]) * (1.0 + sc_ref[...]) + sh_ref[...]
    hb = h.astype(BF16)
    hp_ref[...] = _pack_rows(h)

    act = _dot(hb, wsg_ref[...])
    act = (act * jax.nn.sigmoid(act) * _dot(hb, wsu_ref[...])).astype(BF16)
    xo_ref[...] = x + gt2_ref[...] * _dot(act, wsd_ref[...])

    logits = _dot_nt(wr_ref[...], h, precision=lax.Precision.HIGHEST)
    scores = jax.nn.sigmoid(logits)
    cur = scores + br_ref[...]
    erow = lax.broadcasted_iota(I32, (N_EXPERTS, tm), 0).astype(F32)
    picked = jnp.zeros((N_EXPERTS, tm), F32)
    hits, idxs, wts = [], [], []
    for _ in range(TOP_K):
        cmax = jnp.max(cur, axis=0, keepdims=True)
        first = jnp.min(jnp.where(cur == cmax, erow, float(N_EXPERTS)), axis=0, keepdims=True)
        hit = erow == first
        hits.append(hit)
        idxs.append(first)
        wts.append(jnp.sum(jnp.where(hit, scores, 0.0), axis=0, keepdims=True))
        picked = jnp.where(hit, 1.0, picked)
        cur = jnp.where(hit, -jnp.inf, cur)
    wsum = wts[0]
    for w in wts[1:]:
        wsum = wsum + w
    ti = lax.broadcasted_iota(I32, (tm, tm), 0)
    tj = lax.broadcasted_iota(I32, (tm, tm), 1)
    before = jnp.where(ti < tj, 1.0, 0.0).astype(BF16)
    prior = _dot(picked.astype(BF16), before) + carry_ref[...]
    for k in range(TOP_K):
        idx_ref[k:k + 1, :] = idxs[k].astype(I32)
        wts_ref[k:k + 1, :] = wts[k] / wsum * ROUTED_SCALE
        rank_ref[k:k + 1, :] = jnp.sum(jnp.where(hits[k], prior, 0.0), axis=0,
                                       keepdims=True).astype(I32)
    carry_ref[...] = carry_ref[...] + jnp.sum(picked, axis=1, keepdims=True)
    cnt_ref[...] = jnp.broadcast_to(carry_ref[...], cnt_ref.shape).astype(I32)


def _out_projection(x, ys, w_out, gt1, g_ffn, sc2, sh2, gt2, w_rt, b_r, wsg, wsu, wsd, seq, tm=512):
    t, d = x.shape
    per_seq = seq // tm
    full = lambda i: (0, 0)
    mod_spec = pl.BlockSpec((None, 1, d), lambda i: (i // per_seq, 0, 0))
    y_spec = pl.BlockSpec((tm, MIX_W), lambda i: (i, 0))
    tok_spec = pl.BlockSpec((TOP_K, tm), lambda i: (0, i))
    return pl.pallas_call(
        _outproj_kernel,
        grid=(t // tm,),
        in_specs=[pl.BlockSpec((tm, d), lambda i: (i, 0)),
                  y_spec, y_spec, y_spec, y_spec,
                  _resident((4 * MIX_W, d), full),
                  mod_spec,
                  pl.BlockSpec((1, d), full),
                  mod_spec, mod_spec, mod_spec,
                  pl.BlockSpec((N_EXPERTS, d), full),
                  pl.BlockSpec((N_EXPERTS, 1), full),
                  _resident((d, D_SHARED), full),
                  _resident((d, D_SHARED), full),
                  _resident((D_SHARED, d), full)],
        out_specs=[pl.BlockSpec((tm, d), lambda i: (i, 0)),
                   pl.BlockSpec((tm, PACK_W), lambda i: (i, 0)),
                   tok_spec, tok_spec, tok_spec,
                   pl.BlockSpec((N_EXPERTS, 128), full)],
        out_shape=[jax.ShapeDtypeStruct((t, d), F32),
                   jax.ShapeDtypeStruct((t, PACK_W), U32),
                   jax.ShapeDtypeStruct((TOP_K, t), I32),
                   jax.ShapeDtypeStruct((TOP_K, t), F32),
                   jax.ShapeDtypeStruct((TOP_K, t), I32),
                   jax.ShapeDtypeStruct((N_EXPERTS, 128), I32)],
        scratch_shapes=[pltpu.VMEM((N_EXPERTS, 1), F32)],
        compiler_params=_cparams(("arbitrary",)),
        name="out_projection_router",
    )(x, *ys, w_out, gt1, g_ffn, sc2, sh2, gt2, w_rt, b_r, wsg, wsu, wsd)


def _row_copy(src_ref, src_row, dst_ref, dst_row, sem):
    return pltpu.make_async_copy(src_ref.at[pl.ds(src_row, 1)], dst_ref.at[pl.ds(dst_row, 1)], sem)


def _dispatch_kernel(dest_ref, hp_ref, xs_ref, sem):
    tm = hp_ref.shape[0]

    def start(t, carry):
        for k in range(TOP_K):
            _row_copy(hp_ref, t, xs_ref, dest_ref[t * TOP_K + k], sem).start(priority=k % 2)
        return carry
    lax.fori_loop(0, tm, start, 0)

    def wait(t, carry):
        for k in range(TOP_K):
            _row_copy(hp_ref, t, xs_ref, dest_ref[t * TOP_K + k], sem).wait()
        return carry
    lax.fori_loop(0, tm, wait, 0)


def _dispatch(dest_t, hp, p_rows, tm=512):
    t = hp.shape[0]
    return pl.pallas_call(
        _dispatch_kernel,
        grid=(t // tm,),
        in_specs=[pl.BlockSpec((tm * TOP_K,), lambda i: (i,), memory_space=pltpu.SMEM),
                  pl.BlockSpec((tm, PACK_W), lambda i: (i, 0))],
        out_specs=pl.BlockSpec(memory_space=pl.ANY),
        out_shape=jax.ShapeDtypeStruct((p_rows, PACK_W), U32),
        scratch_shapes=[pltpu.SemaphoreType.DMA(())],
        compiler_params=_cparams(("arbitrary",)),
        name="moe_dispatch",
    )(dest_t, hp)


def _expert_kernel(be_ref, bv_ref, x_ref, wg_ref, wu_ref, wd_ref, y_ref, wgb_ref, wub_ref, wdb_ref):
    i = pl.program_id(0)
    expert = be_ref[i]
    prev = be_ref[jnp.maximum(i - 1, 0)]

    @pl.when((i == 0) | (expert != prev))
    def _():
        wgb_ref[...] = wg_ref[...].astype(BF16)
        wub_ref[...] = wu_ref[...].astype(BF16)
        wdb_ref[...] = wd_ref[...].astype(BF16)

    valid = bv_ref[i]

    @pl.when(valid > 0)
    def _():
        lo, hi = _unpack_rows(x_ref[...])
        rows = lax.broadcasted_iota(I32, (x_ref.shape[0], 1), 0)
        xb = jnp.where(rows < valid, jnp.concatenate([lo, hi], axis=1), 0.0).astype(BF16)
        gate = _dot(xb, wgb_ref[...])
        act = (gate * jax.nn.sigmoid(gate) * _dot(xb, wub_ref[...])).astype(BF16)
        y_ref[...] = _pack_rows(_dot(act, wdb_ref[...]))

    @pl.when(valid == 0)
    def _():
        y_ref[...] = jnp.zeros_like(y_ref)


def _experts(block_e, block_valid, xs, w_gate, w_up, w_down, layer):
    p_rows = xs.shape[0]
    d = D_MODEL
    return pl.pallas_call(
        _expert_kernel,
        grid_spec=pltpu.PrefetchScalarGridSpec(
            num_scalar_prefetch=2,
            grid=(p_rows // MOE_BLOCK,),
            in_specs=[pl.BlockSpec((MOE_BLOCK, PACK_W), lambda i, be, bv: (i, 0)),
                      pl.BlockSpec((None, None, d, D_EXPERT), lambda i, be, bv: (layer, be[i], 0, 0)),
                      pl.BlockSpec((None, None, d, D_EXPERT), lambda i, be, bv: (layer, be[i], 0, 0)),
                      pl.BlockSpec((None, None, D_EXPERT, d), lambda i, be, bv: (layer, be[i], 0, 0))],
            out_specs=pl.BlockSpec((MOE_BLOCK, PACK_W), lambda i, be, bv: (i, 0)),
            scratch_shapes=[pltpu.VMEM((d, D_EXPERT), BF16),
                            pltpu.VMEM((d, D_EXPERT), BF16),
                            pltpu.VMEM((D_EXPERT, d), BF16)]),
        out_shape=jax.ShapeDtypeStruct((p_rows, PACK_W), U32),
        compiler_params=_cparams(("arbitrary",)),
        name="moe_experts",
    )(block_e, block_valid, xs, w_gate, w_up, w_down)


def _combine_kernel(dest_ref, wts_ref, x_ref, gt2_ref, gf_ref, ys_ref, o_ref, buf_ref, sem, *, final):
    tm = x_ref.shape[0]

    def start(g, carry):
        base = pl.multiple_of(g * SUBLANES, SUBLANES)
        for r in range(SUBLANES):
            for k in range(TOP_K):
                _row_copy(ys_ref, dest_ref[(base + r) * TOP_K + k], buf_ref.at[k], base + r,
                          sem).start(priority=k % 2)
        return carry
    lax.fori_loop(0, tm // SUBLANES, start, 0)

    def wait(t, carry):
        for k in range(TOP_K):
            _row_copy(ys_ref, dest_ref[t * TOP_K + k], buf_ref.at[k], t, sem).wait()
        return carry
    lax.fori_loop(0, tm, wait, 0)

    acc_lo = jnp.zeros((tm, PACK_W), F32)
    acc_hi = jnp.zeros((tm, PACK_W), F32)
    for k in range(TOP_K):
        lo, hi = _unpack_rows(buf_ref[k])
        w = wts_ref[:, k:k + 1]
        acc_lo = acc_lo + w * lo
        acc_hi = acc_hi + w * hi
    x = x_ref[...] + gt2_ref[...] * jnp.concatenate([acc_lo, acc_hi], axis=1)
    if final:
        ms = jnp.mean(x * x, axis=-1, keepdims=True)
        x = x * lax.rsqrt(ms + EPS) * gf_ref[...]
    o_ref[...] = x


def _combine(dest_t, wts, x_mid, gt2, g_final, ys, seq, final, tm=512):
    t, d = x_mid.shape
    per_seq = seq // tm
    return pl.pallas_call(
        functools.partial(_combine_kernel, final=final),
        grid=(t // tm,),
        in_specs=[pl.BlockSpec((tm * TOP_K,), lambda i: (i,), memory_space=pltpu.SMEM),
                  pl.BlockSpec((tm, TOP_K), lambda i: (i, 0)),
                  pl.BlockSpec((tm, d), lambda i: (i, 0)),
                  pl.BlockSpec((None, 1, d), lambda i: (i // per_seq, 0, 0)),
                  pl.BlockSpec((1, d), lambda i: (0, 0)),
                  pl.BlockSpec(memory_space=pl.ANY)],
        out_specs=pl.BlockSpec((tm, d), lambda i: (i, 0)),
        out_shape=jax.ShapeDtypeStruct((t, d), F32),
        scratch_shapes=[pltpu.VMEM((TOP_K, tm, PACK_W), U32),
                        pltpu.SemaphoreType.DMA(())],
        compiler_params=_cparams(("arbitrary",)),
        name="moe_combine",
    )(dest_t, wts, x_mid, gt2, g_final, ys)


def _pad_to(a, axis, size):
    pad = [(0, 0)] * a.ndim
    pad[axis] = (0, size - a.shape[axis])
    return jnp.pad(a, pad)


def _in_weight(w_in):
    off_b = ZA_W
    off_cq = off_b + ZB_W
    off_ckv = off_cq + MLA_Q_RANK
    off_kr = off_ckv + MLA_KV_RANK
    off_d = off_kr + MLA_ROPE
    parts = [w_in[:, :off_cq],
             _pad_to(w_in[:, off_cq:off_ckv], 1, CQ_PAD),
             _pad_to(w_in[:, off_ckv:off_kr], 1, CKV_PAD),
             _pad_to(w_in[:, off_kr:off_d], 1, KR_PAD),
             w_in[:, off_d:]]
    return jnp.concatenate(parts, axis=1).astype(BF16)


def _mla_weights(w_uq, w_ukv):
    wq = w_uq.reshape(MLA_Q_RANK, N_HEADS, MLA_QK)
    wq = _pad_to(_pad_to(wq, 2, ATT_W), 0, CQ_PAD).reshape(CQ_PAD, N_HEADS * ATT_W)
    wkv = _pad_to(w_ukv.reshape(MLA_KV_RANK, N_HEADS, MLA_NOPE + HEAD_DIM), 0, CKV_PAD)
    wk = _pad_to(wkv[:, :, :MLA_NOPE], 2, ATT_W).reshape(CKV_PAD, N_HEADS * ATT_W)
    wv = wkv[:, :, MLA_NOPE:].reshape(CKV_PAD, MIX_W)
    return wq.astype(BF16), wk.astype(BF16), wv.astype(BF16)


def _lookup(table, idx):
    ids = jnp.arange(table.shape[0], dtype=I32).reshape((-1,) + (1,) * idx.ndim)
    vals = table.reshape(ids.shape)
    return jnp.sum(jnp.where(idx[None] == ids, vals, 0), axis=0)


def _route_tables(idx_t, rank_t, counts, p_rows):
    m = MOE_BLOCK
    padded = (counts + m - 1) // m * m
    pend = jnp.cumsum(padded)
    pstart = pend - padded
    dest_t = _lookup(pstart, idx_t) + rank_t
    blk_start = jnp.arange(p_rows // m, dtype=I32) * m
    block_e = jnp.minimum(jnp.sum(pend[None, :] <= blk_start[:, None], axis=1), N_EXPERTS - 1)
    block_e = block_e.astype(I32)
    block_valid = jnp.clip(_lookup(pstart + counts, block_e) - blk_start, 0, m).astype(I32)
    dest = dest_t.astype(I32).T.reshape(-1)
    return dest, block_e, block_valid


def kernel(x, c, positions, w_mod, b_mod, g_attn, w_in, g_gmlp_v, w_spatial, b_spatial, g_mla_q, w_mla_uq, g_mla_kv, w_mla_ukv, w_conv_dw, b_conv_dw, g_conv_ln, b_conv_ln, w_out, g_ffn, w_router, b_router, w_exp_gate, w_exp_up, w_exp_down, w_sh_gate, w_sh_up, w_sh_down, g_final):
    batch, seq, d = x.shape
    n_layers = w_mod.shape[0]
    t = batch * seq
    p_rows = t * TOP_K + N_EXPERTS * MOE_BLOCK

    xf = x.reshape(t, d)
    pos_col = positions.reshape(t, 1)
    pos_chunks = positions.reshape(batch, seq // ATT_TK, ATT_TK)
    slopes = 2.0 ** (-8.0 * jnp.arange(1, N_HEADS + 1, dtype=F32) / N_HEADS)
    half = MLA_ROPE // 2
    inv_freq = ROPE_THETA ** (-jnp.arange(half, dtype=F32) * (2.0 / MLA_ROPE))
    invf = _pad_to(jnp.concatenate([inv_freq, inv_freq]), 0, KR_PAD).reshape(1, KR_PAD)

    mod = _modulation(c, w_mod, b_mod).reshape(n_layers, N_MOD, batch, 1, d)

    routed = None
    for l in range(n_layers):
        sh1, sc1, gt1, sh2, sc2, gt2 = (mod[l, j] for j in range(N_MOD))
        if routed is not None:
            xf = _combine(*routed, seq=seq, final=False)

        qm, ka, va, zb, zc, zd = _in_projection(xf, g_attn[l].reshape(1, d), sc1, sh1,
                                                _in_weight(w_in[l]), seq)
        y_a = _moba_attention(qm, ka, va, pos_col, pos_chunks, slopes, batch, seq)
        y_b = _spatial_gating(zb, g_gmlp_v[l], w_spatial[l], b_spatial[l])
        wq, wk, wv = _mla_weights(w_mla_uq[l], w_mla_ukv[l])
        q, k, v = _mla_prep(zc, pos_col, invf,
                            _pad_to(g_mla_q[l], 0, CQ_PAD).reshape(1, CQ_PAD),
                            _pad_to(g_mla_kv[l], 0, CKV_PAD).reshape(1, CKV_PAD), wq, wk, wv)
        y_c = _mla_attention(q, k, v, batch, seq)
        y_d = _conformer_conv(zd, w_conv_dw[l].reshape(CONV_WIDTH, MIX_W),
                              b_conv_dw[l].reshape(1, MIX_W), g_conv_ln[l].reshape(1, MIX_W),
                              b_conv_ln[l].reshape(1, MIX_W), seq)

        x_mid, hp, idx_t, wts_t, rank_t, cnt = _out_projection(
            xf, (y_a, y_b, y_c, y_d), w_out[l].astype(BF16), gt1, g_ffn[l].reshape(1, d),
            sc2, sh2, gt2, w_router[l].T, b_router[l].reshape(N_EXPERTS, 1),
            w_sh_gate[l].astype(BF16), w_sh_up[l].astype(BF16), w_sh_down[l].astype(BF16), seq)

        dest, block_e, block_valid = _route_tables(idx_t, rank_t, cnt[:, 0], p_rows)
        xs = _dispatch(dest, hp, p_rows)
        ys = _experts(block_e, block_valid, xs, w_exp_gate, w_exp_up, w_exp_down, l)
        routed = (dest, wts_t.T, x_mid, gt2, g_final.reshape(1, d), ys)

    out = _combine(*routed, seq=seq, final=True)
    return out.reshape(batch, seq, d)
```

```python
import functools

import jax
import jax.numpy as jnp
from jax import lax
from jax.experimental import pallas as pl
from jax.experimental.pallas import tpu as pltpu

F32 = jnp.float32
BF16 = jnp.bfloat16
I32 = jnp.int32
U32 = jnp.uint32

D_MODEL = 2048
HEAD_DIM = 128
N_HEADS = 4
MIX_W = N_HEADS * HEAD_DIM
MOBA_BLOCK = 256
MOBA_TOPK = 3
GMLP_CHUNK = 128
MLA_Q_RANK = 448
MLA_KV_RANK = 160
MLA_NOPE = 128
MLA_ROPE = 64
MLA_QK = MLA_NOPE + MLA_ROPE
ROPE_THETA = 10000.0
CONV_WIDTH = 31
CONV_HALO = 32
SUBLANES = 8
N_EXPERTS = 64
TOP_K = 8
D_EXPERT = 512
D_SHARED = 512
ROUTED_SCALE = 2.5
MOE_BLOCK = 512
N_MOD = 6
EPS = 1e-6
NEG = -1e30
LOG2E = 1.4426950408889634

ATT_TQ = 2048
ATT_SUB = 256
ATT_TK = 1024
ATT_W = 2 * HEAD_DIM

ZA_W = 3 * MIX_W
ZB_W = 2 * MIX_W
CQ_PAD = 512
CKV_PAD = 256
KR_PAD = 128
ZC_W = CQ_PAD + CKV_PAD + KR_PAD
ZD_W = 2 * MIX_W

VMEM_LIMIT = 56 * 1024 * 1024
PACK_W = D_MODEL // 2


def _cparams(sem):
    return pltpu.CompilerParams(dimension_semantics=sem, vmem_limit_bytes=VMEM_LIMIT)


def _resident(shape, index_map):
    return pl.BlockSpec(shape, index_map, pipeline_mode=pl.Buffered(1))


def _dot(a, b):
    return jnp.dot(a, b, preferred_element_type=F32)


def _dot_nt(a, b, precision=None):
    return lax.dot_general(a, b, (((1,), (1,)), ((), ())), preferred_element_type=F32,
                           precision=precision)


def _pack_rows(y):
    half = y.shape[1] // 2
    lo = lax.bitcast_convert_type(y[:, :half].astype(BF16).astype(F32), U32)
    hi = lax.bitcast_convert_type(y[:, half:].astype(BF16).astype(F32), U32)
    return (lo >> 16) | hi


def _unpack_rows(u):
    lo = lax.bitcast_convert_type(u << 16, F32)
    hi = lax.bitcast_convert_type(u & jnp.uint32(0xFFFF0000), F32)
    return lo, hi


def _mod_kernel(c_ref, w_ref, b_ref, o_ref):
    c = c_ref[...]
    a = c * jax.nn.sigmoid(c)
    o_ref[...] = jnp.dot(a, w_ref[...], preferred_element_type=F32,
                         precision=lax.Precision.HIGHEST) + b_ref[...]


def _modulation(c, w_mod, b_mod):
    n_layers, d, _ = w_mod.shape
    b = c.shape[0]
    return pl.pallas_call(
        _mod_kernel,
        grid=(n_layers, N_MOD),
        in_specs=[pl.BlockSpec((b, d), lambda l, j: (0, 0)),
                  pl.BlockSpec((None, d, d), lambda l, j: (l, 0, j)),
                  pl.BlockSpec((None, None, 1, d), lambda l, j: (l, j, 0, 0))],
        out_specs=pl.BlockSpec((None, None, b, d), lambda l, j: (l, j, 0, 0)),
        out_shape=jax.ShapeDtypeStruct((n_layers, N_MOD, b, d), F32),
        compiler_params=_cparams(("arbitrary", "arbitrary")),
        name="modulation",
    )(c, w_mod, b_mod.reshape(n_layers, N_MOD, 1, d))


def _inproj_kernel(x_ref, g_ref, sc_ref, sh_ref, w_ref, q_ref, ka_ref, va_ref, zb_ref, zc_ref, zd_ref,
                   *, seq):
    tm = x_ref.shape[0]
    x = x_ref[...]
    ms = jnp.mean(x * x, axis=-1, keepdims=True)
    h = (x * lax.rsqrt(ms + EPS) * g_ref[...]) * (1.0 + sc_ref[...]) + sh_ref[...]
    hb = h.astype(BF16)

    q_ref[...] = _dot(hb, w_ref[:, 0:MIX_W]).astype(BF16)
    pos = (pl.program_id(0) * tm) % seq + lax.broadcasted_iota(I32, (tm, HEAD_DIM), 0)
    lane = lax.broadcasted_iota(I32, (tm, HEAD_DIM), 1)
    onehot = jnp.where(pos // MOBA_BLOCK == lane, 1.0, 0.0).astype(BF16)
    ones = jnp.ones((tm, HEAD_DIM), BF16)
    k = _dot(hb, w_ref[:, MIX_W:2 * MIX_W]).astype(BF16)
    v = _dot(hb, w_ref[:, 2 * MIX_W:ZA_W]).astype(BF16)
    for hd in range(N_HEADS):
        src = slice(hd * HEAD_DIM, (hd + 1) * HEAD_DIM)
        ka_ref[:, hd * ATT_W:hd * ATT_W + HEAD_DIM] = k[:, src]
        ka_ref[:, hd * ATT_W + HEAD_DIM:(hd + 1) * ATT_W] = onehot
        va_ref[:, hd * ATT_W:hd * ATT_W + HEAD_DIM] = v[:, src]
        va_ref[:, hd * ATT_W + HEAD_DIM:(hd + 1) * ATT_W] = ones

    off = ZA_W
    for ref in (zb_ref, zc_ref, zd_ref):
        w = ref.shape[1]
        ref[...] = _dot(hb, w_ref[:, off:off + w]).astype(BF16)
        off += w


def _in_projection(x, g, sc, sh, w_in_p, seq, tm=512):
    t, d = x.shape
    per_seq = seq // tm
    widths = (MIX_W, N_HEADS * ATT_W, N_HEADS * ATT_W, ZB_W, ZC_W, ZD_W)
    mod_spec = pl.BlockSpec((None, 1, d), lambda i: (i // per_seq, 0, 0))
    return pl.pallas_call(
        functools.partial(_inproj_kernel, seq=seq),
        grid=(t // tm,),
        in_specs=[pl.BlockSpec((tm, d), lambda i: (i, 0)),
                  pl.BlockSpec((1, d), lambda i: (0, 0)),
                  mod_spec, mod_spec,
                  _resident((d, ZA_W + ZB_W + ZC_W + ZD_W), lambda i: (0, 0))],
        out_specs=[pl.BlockSpec((tm, w), lambda i: (i, 0)) for w in widths],
        out_shape=[jax.ShapeDtypeStruct((t, w), BF16) for w in widths],
        compiler_params=_cparams(("arbitrary",)),
        name="in_projection",
    )(x, g, sc, sh, w_in_p)


def _attn_kernel(*refs, moba):
    if moba:
        slope_ref, q_ref, k_ref, v_ref, pq_ref, pk_ref, o_ref, kmean_ref, qa_ref, m_ref, acc_ref = refs
    else:
        q_ref, k_ref, v_ref, o_ref, m_ref, acc_ref = refs
        qa_ref = q_ref
    tq, sub, tk = ATT_TQ, ATT_SUB, ATT_TK
    nsub = tq // sub
    i = pl.program_id(2)

    if moba:
        nb_pad = kmean_ref.shape[0]
        nb = k_ref.shape[0] // MOBA_BLOCK

        @pl.when(i == 0)
        def _():
            kmean_ref[...] = jnp.zeros_like(kmean_ref)

            def body(n, carry):
                start = pl.multiple_of(n * MOBA_BLOCK, MOBA_BLOCK)
                kb = k_ref[pl.ds(start, MOBA_BLOCK), 0:HEAD_DIM].astype(F32)
                kmean_ref[pl.ds(n, 1), :] = jnp.mean(kb, axis=0, keepdims=True)
                return carry
            lax.fori_loop(0, nb, body, 0)

        slope = slope_ref[pl.program_id(1)] * LOG2E
        col = lax.broadcasted_iota(I32, (sub, nb_pad), 1)
        colf = col.astype(F32)
        pq_s = []
        for c in range(nsub):
            rows = slice(c * sub, (c + 1) * sub)
            own = i * nsub + c
            qf = q_ref[rows, :].astype(F32)
            gate = _dot_nt(qf, kmean_ref[...], precision=lax.Precision.HIGHEST)
            gate = jnp.where(col < own, gate, NEG)
            sel = col == own
            for _ in range(MOBA_TOPK):
                gmax = jnp.max(gate, axis=-1, keepdims=True)
                first = jnp.min(jnp.where(gate == gmax, colf, float(nb_pad)), axis=-1, keepdims=True)
                hit = colf == first
                sel = sel | (hit & (col < own))
                gate = jnp.where(hit, -jnp.inf, gate)
            qa_ref[rows, 0:HEAD_DIM] = (qf * (HEAD_DIM ** -0.5 * LOG2E)).astype(BF16)
            qa_ref[rows, HEAD_DIM:] = jnp.where(sel, 0.0, NEG).astype(BF16)
            pq_s.append(slope * pq_ref[rows, :].astype(F32))

    m_ref[...] = jnp.full(m_ref.shape, NEG, F32)
    acc_ref[...] = jnp.zeros_like(acc_ref)

    def step(c, chunk, mask_off):
        rows = slice(c * sub, (c + 1) * sub)
        start = pl.multiple_of(chunk * tk, tk)
        s = _dot_nt(qa_ref[rows, :], k_ref[pl.ds(start, tk), :])
        if moba:
            s = s - jnp.abs(pq_s[c] - slope * pk_ref[pl.ds(chunk, 1), :].astype(F32))
        if mask_off is not None:
            qi = lax.broadcasted_iota(I32, (sub, tk), 0)
            ki = lax.broadcasted_iota(I32, (sub, tk), 1)
            s = jnp.where(ki <= qi + mask_off, s, NEG)
        m_old = m_ref[rows, :]
        m_new = jnp.maximum(m_old, jnp.max(s, axis=-1, keepdims=True))
        p = jnp.exp2(s - m_new).astype(BF16)
        acc_ref[rows, :] = jnp.exp2(m_old - m_new) * acc_ref[rows, :] + _dot(p, v_ref[pl.ds(start, tk), :])
        m_ref[rows, :] = m_new

    per_tile = tq // tk
    for c in range(nsub):
        q_lo = c * sub
        for jj in reversed(range(per_tile)):
            k_lo = jj * tk
            if k_lo > q_lo + sub - 1:
                continue
            needs_mask = k_lo + tk - 1 > q_lo
            step(c, i * per_tile + jj, q_lo - k_lo if needs_mask else None)

    def body(j, carry):
        for c in range(nsub):
            step(c, j, None)
        return carry
    lax.fori_loop(0, i * per_tile, body, 0)

    acc = acc_ref[...]
    o_ref[...] = (acc[:, :HEAD_DIM] / acc[:, HEAD_DIM:]).astype(o_ref.dtype)


def _attn_scratch():
    return [pltpu.VMEM((ATT_TQ, 1), F32), pltpu.VMEM((ATT_TQ, ATT_W), F32)]


def _moba_attention(q, ka, va, pos_col, pos_chunks, slopes, batch, seq):
    t = q.shape[0]
    nq = seq // ATT_TQ
    kv_spec = pl.BlockSpec((seq, ATT_W), lambda b, h, i, s: (b, h))
    return pl.pallas_call(
        functools.partial(_attn_kernel, moba=True),
        grid_spec=pltpu.PrefetchScalarGridSpec(
            num_scalar_prefetch=1,
            grid=(batch, N_HEADS, nq),
            in_specs=[pl.BlockSpec((ATT_TQ, HEAD_DIM), lambda b, h, i, s: (b * nq + i, h)),
                      kv_spec, kv_spec,
                      pl.BlockSpec((ATT_TQ, 1), lambda b, h, i, s: (b * nq + i, 0)),
                      pl.BlockSpec((None, seq // ATT_TK, ATT_TK), lambda b, h, i, s: (b, 0, 0))],
            out_specs=pl.BlockSpec((ATT_TQ, HEAD_DIM), lambda b, h, i, s: (b * nq + i, h)),
            scratch_shapes=[pltpu.VMEM((HEAD_DIM, HEAD_DIM), F32),
                            pltpu.VMEM((ATT_TQ, ATT_W), BF16)] + _attn_scratch()),
        out_shape=jax.ShapeDtypeStruct((t, MIX_W), BF16),
        compiler_params=_cparams(("arbitrary", "arbitrary", "arbitrary")),
        name="moba_attention",
    )(slopes, q, ka, va, pos_col, pos_chunks)


def _mla_attention(q, k, va, batch, seq):
    t = q.shape[0]
    nq = seq // ATT_TQ
    kv_spec = pl.BlockSpec((seq, ATT_W), lambda b, h, i: (b, h))
    return pl.pallas_call(
        functools.partial(_attn_kernel, moba=False),
        grid=(batch, N_HEADS, nq),
        in_specs=[pl.BlockSpec((ATT_TQ, ATT_W), lambda b, h, i: (b * nq + i, h)),
                  kv_spec, kv_spec],
        out_specs=pl.BlockSpec((ATT_TQ, HEAD_DIM), lambda b, h, i: (b * nq + i, h)),
        out_shape=jax.ShapeDtypeStruct((t, MIX_W), BF16),
        scratch_shapes=_attn_scratch(),
        compiler_params=_cparams(("arbitrary", "arbitrary", "arbitrary")),
        name="mla_attention",
    )(q, k, va)


def _gelu_tanh(x):
    return 0.5 * x * (1.0 + jnp.tanh(0.7978845608028654 * (x + 0.044715 * x * x * x)))


def _gmlp_kernel(z_ref, gv_ref, ws_ref, bs_ref, o_ref):
    tm = z_ref.shape[0]
    ck = GMLP_CHUNK
    z = _gelu_tanh(z_ref[...].astype(F32))
    row = lax.broadcasted_iota(I32, (ck, ck), 0)
    colm = lax.broadcasted_iota(I32, (ck, ck), 1)
    for g in range(N_HEADS):
        lanes = slice(g * HEAD_DIM, (g + 1) * HEAD_DIM)
        u = z[:, lanes]
        vv = z[:, MIX_W + g * HEAD_DIM:MIX_W + (g + 1) * HEAD_DIM]
        ms = jnp.mean(vv * vv, axis=-1, keepdims=True)
        vn = (vv * lax.rsqrt(ms + EPS) * gv_ref[g:g + 1, :]).astype(BF16)
        w = jnp.where(colm <= row, ws_ref[g], 0.0).astype(BF16)
        bias = bs_ref[g]
        for c in range(tm // ck):
            rows = slice(c * ck, (c + 1) * ck)
            mixed = _dot(w, vn[rows]) + bias
            o_ref[rows, lanes] = (u[rows] * mixed).astype(o_ref.dtype)


def _spatial_gating(zb, g_v, w_s, b_s, tm=512):
    t = zb.shape[0]
    ck = GMLP_CHUNK
    return pl.pallas_call(
        _gmlp_kernel,
        grid=(t // tm,),
        in_specs=[pl.BlockSpec((tm, ZB_W), lambda i: (i, 0)),
                  pl.BlockSpec((N_HEADS, HEAD_DIM), lambda i: (0, 0)),
                  pl.BlockSpec((N_HEADS, ck, ck), lambda i: (0, 0, 0)),
                  pl.BlockSpec((N_HEADS, ck, 1), lambda i: (0, 0, 0))],
        out_specs=pl.BlockSpec((tm, MIX_W), lambda i: (i, 0)),
        out_shape=jax.ShapeDtypeStruct((t, MIX_W), BF16),
        compiler_params=_cparams(("arbitrary",)),
        name="spatial_gating",
    )(zb, g_v, w_s, b_s.reshape(N_HEADS, ck, 1))


def _rope_table_kernel(pos_ref, invf_ref, cos_ref, slo_ref, shi_ref):
    ang = pos_ref[...].astype(F32) * invf_ref[...]
    lane = lax.broadcasted_iota(I32, ang.shape, 1)
    half = MLA_ROPE // 2
    sin = jnp.sin(ang)
    cos_ref[...] = jnp.cos(ang)
    slo_ref[...] = jnp.where(lane < half, -sin, 0.0)
    shi_ref[...] = jnp.where((lane >= half) & (lane < 2 * half), sin, 0.0)


def _rope_tables(pos_col, invf, tm=512):
    t = pos_col.shape[0]
    spec = pl.BlockSpec((tm, KR_PAD), lambda i: (i, 0))
    shape = jax.ShapeDtypeStruct((t, KR_PAD), F32)
    return pl.pallas_call(
        _rope_table_kernel,
        grid=(t // tm,),
        in_specs=[pl.BlockSpec((tm, 1), lambda i: (i, 0)),
                  pl.BlockSpec((1, KR_PAD), lambda i: (0, 0))],
        out_specs=[spec, spec, spec],
        out_shape=[shape, shape, shape],
        compiler_params=_cparams(("arbitrary",)),
        name="rope_tables",
    )(pos_col, invf)


def _mla_prep_kernel(z_ref, cos_ref, slo_ref, shi_ref, gq_ref, gkv_ref, wq_ref, wk_ref, wv_ref,
                     q_ref, k_ref, v_ref):
    tm = z_ref.shape[0]
    z = z_ref[...].astype(F32)
    cq = z[:, :CQ_PAD]
    ckv = z[:, CQ_PAD:CQ_PAD + CKV_PAD]
    kr = z[:, CQ_PAD + CKV_PAD:]
    qn = cq * lax.rsqrt(jnp.sum(cq * cq, -1, keepdims=True) * (1.0 / MLA_Q_RANK) + EPS)
    qn = (qn * gq_ref[...]).astype(BF16)
    kvn = ckv * lax.rsqrt(jnp.sum(ckv * ckv, -1, keepdims=True) * (1.0 / MLA_KV_RANK) + EPS)
    kvn = (kvn * gkv_ref[...]).astype(BF16)

    half = MLA_ROPE // 2
    cos = cos_ref[...]
    sin_lo = slo_ref[...]
    sin_hi = shi_ref[...]

    def rope(r):
        return (r * cos + pltpu.roll(r, KR_PAD - half, 1) * sin_lo
                + pltpu.roll(r, half, 1) * sin_hi)

    q = _dot(qn, wq_ref[...])
    kn = _dot(kvn, wk_ref[...])
    v = _dot(kvn, wv_ref[...]).astype(BF16)
    k_rope = rope(kr)
    scale = MLA_QK ** -0.5 * LOG2E
    ones = jnp.ones((tm, HEAD_DIM), BF16)
    for h in range(N_HEADS):
        a = h * ATT_W
        b = a + MLA_NOPE
        q_ref[:, a:b] = (q[:, a:b] * scale).astype(q_ref.dtype)
        q_ref[:, b:a + ATT_W] = (rope(q[:, b:a + ATT_W]) * scale).astype(q_ref.dtype)
        k_ref[:, a:b] = kn[:, a:b].astype(k_ref.dtype)
        k_ref[:, b:a + ATT_W] = k_rope.astype(k_ref.dtype)
        v_ref[:, a:b] = v[:, h * HEAD_DIM:(h + 1) * HEAD_DIM]
        v_ref[:, b:a + ATT_W] = ones


def _mla_prep(zc, rope, gq, gkv, wq, wk, wv, tm=512):
    t = zc.shape[0]
    qk_w = N_HEADS * ATT_W
    full = lambda i: (0, 0)
    out_spec = pl.BlockSpec((tm, qk_w), lambda i: (i, 0))
    out_shape = jax.ShapeDtypeStruct((t, qk_w), BF16)
    rope_spec = pl.BlockSpec((tm, KR_PAD), lambda i: (i, 0))
    return pl.pallas_call(
        _mla_prep_kernel,
        grid=(t // tm,),
        in_specs=[pl.BlockSpec((tm, ZC_W), lambda i: (i, 0)),
                  rope_spec, rope_spec, rope_spec,
                  pl.BlockSpec((1, CQ_PAD), full),
                  pl.BlockSpec((1, CKV_PAD), full),
                  pl.BlockSpec((CQ_PAD, qk_w), full),
                  pl.BlockSpec((CKV_PAD, qk_w), full),
                  pl.BlockSpec((CKV_PAD, MIX_W), full)],
        out_specs=[out_spec, out_spec, out_spec],
        out_shape=[out_shape, out_shape, out_shape],
        compiler_params=_cparams(("arbitrary",)),
        name="mla_prep",
    )(zc, *rope, gq, gkv, wq, wk, wv)


def _glu(z):
    z = z.astype(F32)
    return z[:, :MIX_W] * jax.nn.sigmoid(z[:, MIX_W:])


def _conv_kernel(z_ref, zprev_ref, w_ref, b_ref, g_ref, beta_ref, o_ref, ybuf_ref, ysh_ref, *, per_seq):
    tm = z_ref.shape[0]
    first = (pl.program_id(0) % per_seq) == 0
    ybuf_ref[0:CONV_HALO, :] = jnp.where(first, 0.0, _glu(zprev_ref[...]))
    ybuf_ref[CONV_HALO:, :] = _glu(z_ref[...])
    span = tm + CONV_HALO - SUBLANES
    for o in range(1, SUBLANES):
        ysh_ref[o - 1, :, :] = ybuf_ref[pl.ds(o, span), :]
    rows = 64
    shift = CONV_HALO - (CONV_WIDTH - 1)
    for r in range(tm // rows):
        acc = jnp.zeros((rows, MIX_W), F32) + b_ref[...]
        for j in range(CONV_WIDTH):
            o = (shift + j) % SUBLANES
            base = r * rows + shift + j - o
            tap = ybuf_ref[pl.ds(base, rows), :] if o == 0 else ysh_ref[o - 1, pl.ds(base, rows), :]
            acc = acc + w_ref[j:j + 1, :] * tap
        mu = jnp.mean(acc, axis=-1, keepdims=True)
        xc = acc - mu
        y = xc * lax.rsqrt(jnp.mean(xc * xc, axis=-1, keepdims=True) + EPS)
        y = y * g_ref[...] + beta_ref[...]
        o_ref[r * rows:(r + 1) * rows, :] = (y * jax.nn.sigmoid(y)).astype(o_ref.dtype)


def _conformer_conv(zd, w_dw, b_dw, g_ln, b_ln, seq, tm=512):
    t = zd.shape[0]
    per_seq = seq // tm
    halo_blocks = tm // CONV_HALO
    full = lambda i: (0, 0)
    return pl.pallas_call(
        functools.partial(_conv_kernel, per_seq=per_seq),
        grid=(t // tm,),
        in_specs=[pl.BlockSpec((tm, ZD_W), lambda i: (i, 0)),
                  pl.BlockSpec((CONV_HALO, ZD_W),
                               lambda i: (jnp.maximum(i * halo_blocks - 1, 0), 0)),
                  pl.BlockSpec((CONV_WIDTH, MIX_W), full),
                  pl.BlockSpec((1, MIX_W), full),
                  pl.BlockSpec((1, MIX_W), full),
                  pl.BlockSpec((1, MIX_W), full)],
        out_specs=pl.BlockSpec((tm, MIX_W), lambda i: (i, 0)),
        out_shape=jax.ShapeDtypeStruct((t, MIX_W), BF16),
        scratch_shapes=[pltpu.VMEM((tm + CONV_HALO, MIX_W), F32),
                        pltpu.VMEM((SUBLANES - 1, tm + CONV_HALO - SUBLANES, MIX_W), F32)],
        compiler_params=_cparams(("arbitrary",)),
        name="conformer_conv",
    )(zd, zd, w_dw, b_dw, g_ln, b_ln)


def _outproj_kernel(x_ref, ya_ref, yb_ref, yc_ref, yd_ref, wo_ref, gt1_ref, g_ref, sc_ref, sh_ref,
                    gt2_ref, wr_ref, br_ref, wsg_ref, wsu_ref, wsd_ref,
                    xo_ref, hp_ref, idx_ref, wts_ref, rank_ref, cnt_ref, carry_ref):
    tm = x_ref.shape[0]

    @pl.when(pl.program_id(0) == 0)
    def _():
        carry_ref[...] = jnp.zeros_like(carry_ref)

    y = _dot(ya_ref[...], wo_ref[0:MIX_W, :])
    y = y + _dot(yb_ref[...], wo_ref[MIX_W:2 * MIX_W, :])
    y = y + _dot(yc_ref[...], wo_ref[2 * MIX_W:3 * MIX_W, :])
    y = y + _dot(yd_ref[...], wo_ref[3 * MIX_W:, :])
    x = x_ref[...] + gt1_ref[...] * y
    ms = jnp.mean(x * x, axis=-1, keepdims=True)
    h = (x * lax.rsqrt(ms + EPS) * g_ref[...]) * (1.0 + sc_ref[...]) + sh_ref[...]
    hb = h.astype(BF16)
    hp_ref[...] = _pack_rows(h)

    act = _dot(hb, wsg_ref[...])
    act = (act * jax.nn.sigmoid(act) * _dot(hb, wsu_ref[...])).astype(BF16)
    xo_ref[...] = x + gt2_ref[...] * _dot(act, wsd_ref[...])

    logits = _dot_nt(wr_ref[...], h, precision=lax.Precision.HIGHEST)
    scores = jax.nn.sigmoid(logits)
    cur = scores + br_ref[...]
    erow = lax.broadcasted_iota(I32, (N_EXPERTS, tm), 0).astype(F32)
    picked = jnp.zeros((N_EXPERTS, tm), F32)
    hits, idxs, wts = [], [], []
    for _ in range(TOP_K):
        cmax = jnp.max(cur, axis=0, keepdims=True)
        first = jnp.min(jnp.where(cur == cmax, erow, float(N_EXPERTS)), axis=0, keepdims=True)
        hit = erow == first
        hits.append(hit)
        idxs.append(first)
        wts.append(jnp.sum(jnp.where(hit, scores, 0.0), axis=0, keepdims=True))
        picked = jnp.where(hit, 1.0, picked)
        cur = jnp.where(hit, -jnp.inf, cur)
    wsum = wts[0]
    for w in wts[1:]:
        wsum = wsum + w
    ti = lax.broadcasted_iota(I32, (tm, tm), 0)
    tj = lax.broadcasted_iota(I32, (tm, tm), 1)
    before = jnp.where(ti < tj, 1.0, 0.0).astype(BF16)
    prior = _dot(picked.astype(BF16), before) + carry_ref[...]
    for k in range(TOP_K):
        idx_ref[k:k + 1, :] = idxs[k].astype(I32)
        wts_ref[k:k + 1, :] = wts[k] / wsum * ROUTED_SCALE
        rank_ref[k:k + 1, :] = jnp.sum(jnp.where(hits[k], prior, 0.0), axis=0,
                                       keepdims=True).astype(I32)
    carry_ref[...] = carry_ref[...] + jnp.sum(picked, axis=1, keepdims=True)
    cnt_ref[...] = jnp.broadcast_to(carry_ref[...], cnt_ref.shape).astype(I32)


def _out_projection(x, ys, w_out, gt1, g_ffn, sc2, sh2, gt2, w_rt, b_r, wsg, wsu, wsd, seq, tm=512):
    t, d = x.shape
    per_seq = seq // tm
    full = lambda i: (0, 0)
    mod_spec = pl.BlockSpec((None, 1, d), lambda i: (i // per_seq, 0, 0))
    y_spec = pl.BlockSpec((tm, MIX_W), lambda i: (i, 0))
    tok_spec = pl.BlockSpec((TOP_K, tm), lambda i: (0, i))
    return pl.pallas_call(
        _outproj_kernel,
        grid=(t // tm,),
        in_specs=[pl.BlockSpec((tm, d), lambda i: (i, 0)),
                  y_spec, y_spec, y_spec, y_spec,
                  _resident((4 * MIX_W, d), full),
                  mod_spec,
                  pl.BlockSpec((1, d), full),
                  mod_spec, mod_spec, mod_spec,
                  pl.BlockSpec((N_EXPERTS, d), full),
                  pl.BlockSpec((N_EXPERTS, 1), full),
                  _resident((d, D_SHARED), full),
                  _resident((d, D_SHARED), full),
                  _resident((D_SHARED, d), full)],
        out_specs=[pl.BlockSpec((tm, d), lambda i: (i, 0)),
                   pl.BlockSpec((tm, PACK_W), lambda i: (i, 0)),
                   tok_spec, tok_spec, tok_spec,
                   pl.BlockSpec((N_EXPERTS, 128), full)],
        out_shape=[jax.ShapeDtypeStruct((t, d), F32),
                   jax.ShapeDtypeStruct((t, PACK_W), U32),
                   jax.ShapeDtypeStruct((TOP_K, t), I32),
                   jax.ShapeDtypeStruct((TOP_K, t), F32),
                   jax.ShapeDtypeStruct((TOP_K, t), I32),
                   jax.ShapeDtypeStruct((N_EXPERTS, 128), I32)],
        scratch_shapes=[pltpu.VMEM((N_EXPERTS, 1), F32)],
        compiler_params=_cparams(("arbitrary",)),
        name="out_projection_router",
    )(x, *ys, w_out, gt1, g_ffn, sc2, sh2, gt2, w_rt, b_r, wsg, wsu, wsd)


def _row_copy(src_ref, src_row, dst_ref, dst_row, sem):
    return pltpu.make_async_copy(src_ref.at[pl.ds(src_row, 1)], dst_ref.at[pl.ds(dst_row, 1)], sem)


def _dispatch_kernel(dest_ref, hp_ref, xs_ref, sem):
    tm = hp_ref.shape[0]

    def start(t, carry):
        for k in range(TOP_K):
            _row_copy(hp_ref, t, xs_ref, dest_ref[t * TOP_K + k], sem).start(priority=k % 2)
        return carry
    lax.fori_loop(0, tm, start, 0)

    def wait(t, carry):
        for k in range(TOP_K):
            _row_copy(hp_ref, t, xs_ref, dest_ref[t * TOP_K + k], sem).wait()
        return carry
    lax.fori_loop(0, tm, wait, 0)


def _dispatch(dest_t, hp, p_rows, tm=512):
    t = hp.shape[0]
    return pl.pallas_call(
        _dispatch_kernel,
        grid=(t // tm,),
        in_specs=[pl.BlockSpec((tm * TOP_K,), lambda i: (i,), memory_space=pltpu.SMEM),
                  pl.BlockSpec((tm, PACK_W), lambda i: (i, 0))],
        out_specs=pl.BlockSpec(memory_space=pl.ANY),
        out_shape=jax.ShapeDtypeStruct((p_rows, PACK_W), U32),
        scratch_shapes=[pltpu.SemaphoreType.DMA(())],
        compiler_params=_cparams(("arbitrary",)),
        name="moe_dispatch",
    )(dest_t, hp)


def _expert_kernel(be_ref, bv_ref, x_ref, wg_ref, wu_ref, wd_ref, y_ref, wgb_ref, wub_ref, wdb_ref):
    i = pl.program_id(0)
    expert = be_ref[i]
    prev = be_ref[jnp.maximum(i - 1, 0)]

    @pl.when((i == 0) | (expert != prev))
    def _():
        wgb_ref[...] = wg_ref[...].astype(BF16)
        wub_ref[...] = wu_ref[...].astype(BF16)
        wdb_ref[...] = wd_ref[...].astype(BF16)

    valid = bv_ref[i]

    @pl.when(valid > 0)
    def _():
        lo, hi = _unpack_rows(x_ref[...])
        rows = lax.broadcasted_iota(I32, (x_ref.shape[0], 1), 0)
        xb = jnp.where(rows < valid, jnp.concatenate([lo, hi], axis=1), 0.0).astype(BF16)
        gate = _dot(xb, wgb_ref[...])
        act = (gate * jax.nn.sigmoid(gate) * _dot(xb, wub_ref[...])).astype(BF16)
        y_ref[...] = _pack_rows(_dot(act, wdb_ref[...]))

    @pl.when(valid == 0)
    def _():
        y_ref[...] = jnp.zeros_like(y_ref)


def _experts(block_e, block_valid, xs, w_gate, w_up, w_down, layer):
    p_rows = xs.shape[0]
    d = D_MODEL
    return pl.pallas_call(
        _expert_kernel,
        grid_spec=pltpu.PrefetchScalarGridSpec(
            num_scalar_prefetch=2,
            grid=(p_rows // MOE_BLOCK,),
            in_specs=[pl.BlockSpec((MOE_BLOCK, PACK_W), lambda i, be, bv: (i, 0)),
                      pl.BlockSpec((None, None, d, D_EXPERT), lambda i, be, bv: (layer, be[i], 0, 0)),
                      pl.BlockSpec((None, None, d, D_EXPERT), lambda i, be, bv: (layer, be[i], 0, 0)),
                      pl.BlockSpec((None, None, D_EXPERT, d), lambda i, be, bv: (layer, be[i], 0, 0))],
            out_specs=pl.BlockSpec((MOE_BLOCK, PACK_W), lambda i, be, bv: (i, 0)),
            scratch_shapes=[pltpu.VMEM((d, D_EXPERT), BF16),
                            pltpu.VMEM((d, D_EXPERT), BF16),
                            pltpu.VMEM((D_EXPERT, d), BF16)]),
        out_shape=jax.ShapeDtypeStruct((p_rows, PACK_W), U32),
        compiler_params=_cparams(("arbitrary",)),
        name="moe_experts",
    )(block_e, block_valid, xs, w_gate, w_up, w_down)


def _combine_kernel(dest_ref, wts_ref, x_ref, gt2_ref, gf_ref, ys_ref, o_ref, buf_ref, sem, *, final):
    tm = x_ref.shape[0]

    def start(g, carry):
        base = pl.multiple_of(g * SUBLANES, SUBLANES)
        for r in range(SUBLANES):
            for k in range(TOP_K):
                _row_copy(ys_ref, dest_ref[(base + r) * TOP_K + k], buf_ref.at[k], base + r,
                          sem).start(priority=k % 2)
        return carry
    lax.fori_loop(0, tm // SUBLANES, start, 0)

    def wait(t, carry):
        for k in range(TOP_K):
            _row_copy(ys_ref, dest_ref[t * TOP_K + k], buf_ref.at[k], t, sem).wait()
        return carry
    lax.fori_loop(0, tm, wait, 0)

    acc_lo = jnp.zeros((tm, PACK_W), F32)
    acc_hi = jnp.zeros((tm, PACK_W), F32)
    for k in range(TOP_K):
        lo, hi = _unpack_rows(buf_ref[k])
        w = wts_ref[:, k:k + 1]
        acc_lo = acc_lo + w * lo
        acc_hi = acc_hi + w * hi
    x = x_ref[...] + gt2_ref[...] * jnp.concatenate([acc_lo, acc_hi], axis=1)
    if final:
        ms = jnp.mean(x * x, axis=-1, keepdims=True)
        x = x * lax.rsqrt(ms + EPS) * gf_ref[...]
    o_ref[...] = x


def _combine(dest_t, wts, x_mid, gt2, g_final, ys, seq, final, tm=256):
    t, d = x_mid.shape
    per_seq = seq // tm
    return pl.pallas_call(
        functools.partial(_combine_kernel, final=final),
        grid=(t // tm,),
        in_specs=[pl.BlockSpec((tm * TOP_K,), lambda i: (i,), memory_space=pltpu.SMEM),
                  pl.BlockSpec((tm, TOP_K), lambda i: (i, 0)),
                  pl.BlockSpec((tm, d), lambda i: (i, 0)),
                  pl.BlockSpec((None, 1, d), lambda i: (i // per_seq, 0, 0)),
                  pl.BlockSpec((1, d), lambda i: (0, 0)),
                  pl.BlockSpec(memory_space=pl.ANY)],
        out_specs=pl.BlockSpec((tm, d), lambda i: (i, 0)),
        out_shape=jax.ShapeDtypeStruct((t, d), F32),
        scratch_shapes=[pltpu.VMEM((TOP_K, tm, PACK_W), U32),
                        pltpu.SemaphoreType.DMA(())],
        compiler_params=_cparams(("arbitrary",)),
        name="moe_combine",
    )(dest_t, wts, x_mid, gt2, g_final, ys)


def _pad_to(a, axis, size):
    pad = [(0, 0)] * a.ndim
    pad[axis] = (0, size - a.shape[axis])
    return jnp.pad(a, pad)


def _in_weight(w_in):
    off_b = ZA_W
    off_cq = off_b + ZB_W
    off_ckv = off_cq + MLA_Q_RANK
    off_kr = off_ckv + MLA_KV_RANK
    off_d = off_kr + MLA_ROPE
    parts = [w_in[:, :off_cq],
             _pad_to(w_in[:, off_cq:off_ckv], 1, CQ_PAD),
             _pad_to(w_in[:, off_ckv:off_kr], 1, CKV_PAD),
             _pad_to(w_in[:, off_kr:off_d], 1, KR_PAD),
             w_in[:, off_d:]]
    return jnp.concatenate(parts, axis=1).astype(BF16)


def _mla_weights(w_uq, w_ukv):
    wq = w_uq.reshape(MLA_Q_RANK, N_HEADS, MLA_QK)
    wq = _pad_to(_pad_to(wq, 2, ATT_W), 0, CQ_PAD).reshape(CQ_PAD, N_HEADS * ATT_W)
    wkv = _pad_to(w_ukv.reshape(MLA_KV_RANK, N_HEADS, MLA_NOPE + HEAD_DIM), 0, CKV_PAD)
    wk = _pad_to(wkv[:, :, :MLA_NOPE], 2, ATT_W).reshape(CKV_PAD, N_HEADS * ATT_W)
    wv = wkv[:, :, MLA_NOPE:].reshape(CKV_PAD, MIX_W)
    return wq.astype(BF16), wk.astype(BF16), wv.astype(BF16)


def _lookup(table, idx):
    ids = jnp.arange(table.shape[0], dtype=I32).reshape((-1,) + (1,) * idx.ndim)
    vals = table.reshape(ids.shape)
    return jnp.sum(jnp.where(idx[None] == ids, vals, 0), axis=0)


def _route_tables(idx_t, rank_t, counts, p_rows):
    m = MOE_BLOCK
    padded = (counts + m - 1) // m * m
    pend = jnp.cumsum(padded)
    pstart = pend - padded
    dest_t = _lookup(pstart, idx_t) + rank_t
    blk_start = jnp.arange(p_rows // m, dtype=I32) * m
    block_e = jnp.minimum(jnp.sum(pend[None, :] <= blk_start[:, None], axis=1), N_EXPERTS - 1)
    block_e = block_e.astype(I32)
    block_valid = jnp.clip(_lookup(pstart + counts, block_e) - blk_start, 0, m).astype(I32)
    dest = dest_t.astype(I32).T.reshape(-1)
    return dest, block_e, block_valid


def kernel(x, c, positions, w_mod, b_mod, g_attn, w_in, g_gmlp_v, w_spatial, b_spatial, g_mla_q, w_mla_uq, g_mla_kv, w_mla_ukv, w_conv_dw, b_conv_dw, g_conv_ln, b_conv_ln, w_out, g_ffn, w_router, b_router, w_exp_gate, w_exp_up, w_exp_down, w_sh_gate, w_sh_up, w_sh_down, g_final):
    batch, seq, d = x.shape
    n_layers = w_mod.shape[0]
    t = batch * seq
    p_rows = t * TOP_K + N_EXPERTS * MOE_BLOCK

    xf = x.reshape(t, d)
    pos_col = positions.reshape(t, 1)
    pos_chunks = positions.reshape(batch, seq // ATT_TK, ATT_TK)
    slopes = 2.0 ** (-8.0 * jnp.arange(1, N_HEADS + 1, dtype=F32) / N_HEADS)
    half = MLA_ROPE // 2
    inv_freq = ROPE_THETA ** (-jnp.arange(half, dtype=F32) * (2.0 / MLA_ROPE))
    invf = _pad_to(jnp.concatenate([inv_freq, inv_freq]), 0, KR_PAD).reshape(1, KR_PAD)
    rope = _rope_tables(pos_col, invf)

    mod = _modulation(c, w_mod, b_mod).reshape(n_layers, N_MOD, batch, 1, d)

    routed = None
    for l in range(n_layers):
        sh1, sc1, gt1, sh2, sc2, gt2 = (mod[l, j] for j in range(N_MOD))
        if routed is not None:
            xf = _combine(*routed, seq=seq, final=False)

        qm, ka, va, zb, zc, zd = _in_projection(xf, g_attn[l].reshape(1, d), sc1, sh1,
                                                _in_weight(w_in[l]), seq)
        y_a = _moba_attention(qm, ka, va, pos_col, pos_chunks, slopes, batch, seq)
        y_b = _spatial_gating(zb, g_gmlp_v[l], w_spatial[l], b_spatial[l])
        wq, wk, wv = _mla_weights(w_mla_uq[l], w_mla_ukv[l])
        q, k, v = _mla_prep(zc, rope,
                            _pad_to(g_mla_q[l], 0, CQ_PAD).reshape(1, CQ_PAD),
                            _pad_to(g_mla_kv[l], 0, CKV_PAD).reshape(1, CKV_PAD), wq, wk, wv)
        y_c = _mla_attention(q, k, v, batch, seq)
        y_d = _conformer_conv(zd, w_conv_dw[l].reshape(CONV_WIDTH, MIX_W),
                              b_conv_dw[l].reshape(1, MIX_W), g_conv_ln[l].reshape(1, MIX_W),
                              b_conv_ln[l].reshape(1, MIX_W), seq)

        x_mid, hp, idx_t, wts_t, rank_t, cnt = _out_projection(
            xf, (y_a, y_b, y_c, y_d), w_out[l].astype(BF16), gt1, g_ffn[l].reshape(1, d),
            sc2, sh2, gt2, w_router[l].T, b_router[l].reshape(N_EXPERTS, 1),
            w_sh_gate[l].astype(BF16), w_sh_up[l].astype(BF16), w_sh_down[l].astype(BF16), seq)

        dest, block_e, block_valid = _route_tables(idx_t, rank_t, cnt[:, 0], p_rows)
        xs = _dispatch(dest, hp, p_rows)
        ys = _experts(block_e, block_valid, xs, w_exp_gate, w_exp_up, w_exp_down, l)
        routed = (dest, wts_t.T, x_mid, gt2, g_final.reshape(1, d), ys)

    out = _combine(*routed, seq=seq, final=True)
    return out.reshape(batch, seq, d)
```
